```python
import jax, jax.numpy as jnp
from jax import lax
import numpy as np

D_MODEL = 1024
BATCH = 1
SEQ = 16384
DEPTH = 2

GRID_W = 64
CTX_LEN = 256
HEAD_DIM = 64
ATTN_HEADS = 8
KV_HEADS = 2
GQA_GROUP = ATTN_HEADS // KV_HEADS
WINDOW = 128
BLOCK = 128
N_BAND = 2 * WINDOW // BLOCK + 1
ROPE_BASE = 10000.0
AXIS_DIM = HEAD_DIM // 2
F_GROUPS = 4
F_DIM = 64
C_HEADS = 4
C_DIM = 64
CHUNK = 128
D_FF = -(-8 * D_MODEL // (3 * 256)) * 256

ATTN_W = ATTN_HEADS * HEAD_DIM
KV_W = KV_HEADS * HEAD_DIM
F_W = F_GROUPS * F_DIM
C_W = C_HEADS * C_DIM
D_MIX = ATTN_W + F_W + C_W
D_IN = ATTN_W + 2 * KV_W + F_W + 2 * C_W
SPLITS = (ATTN_W, ATTN_W + KV_W, ATTN_W + 2 * KV_W, ATTN_W + 2 * KV_W + F_W,
          ATTN_W + 2 * KV_W + F_W + C_W)
ALPHA = (2 * DEPTH) ** 0.25
BETA = (8 * DEPTH) ** -0.25
LN_EPS = 1e-6

kernel_name = 'hymba_style_fourier_sgu_window_attn_dit'


def _ln(x):
    xf = x.astype(jnp.float32)
    mu = jnp.mean(xf, axis=-1, keepdims=True)
    var = jnp.mean(jnp.square(xf - mu), axis=-1, keepdims=True)
    return ((xf - mu) * lax.rsqrt(var + LN_EPS)).astype(x.dtype)


def _ln_affine(x, g, b):
    return _ln(x) * g + b


def _ada(cvec, w, b):
    m = (jax.nn.silu(cvec) @ w + b)[:, None, :]
    return jnp.split(m, 6, axis=-1)


def _modulate(x, shift, scale):
    return _ln(x) * (1.0 + scale) + shift


def _post_norm(x, y, gate, g, b):
    return _ln_affine(ALPHA * x + gate * y, g, b)


def _axial_rope_tables(n):
    rows = n // GRID_W
    row = jnp.repeat(jnp.arange(rows), GRID_W).astype(jnp.float32)
    col = jnp.tile(jnp.arange(GRID_W), rows).astype(jnp.float32)
    freqs = ROPE_BASE ** (-jnp.arange(0, AXIS_DIM, 2, dtype=jnp.float32) / AXIS_DIM)
    ang_r = row[:, None] * freqs
    ang_c = col[:, None] * freqs
    return (jnp.cos(ang_r), jnp.sin(ang_r), jnp.cos(ang_c), jnp.sin(ang_c))


def _rotate(x, cos, sin):
    x1, x2 = jnp.split(x, 2, axis=-1)
    cos = cos[None, :, None, :].astype(x.dtype)
    sin = sin[None, :, None, :].astype(x.dtype)
    return jnp.concatenate([x1 * cos - x2 * sin, x1 * sin + x2 * cos], axis=-1)


def _apply_axial_rope(x, rope):
    cr, sr, cc, sc = rope
    return jnp.concatenate([_rotate(x[..., :AXIS_DIM], cr, sr),
                            _rotate(x[..., AXIS_DIM:], cc, sc)], axis=-1)


def _band(t):
    B, N = t.shape[0], t.shape[1]
    nb = N // BLOCK
    tp = jnp.pad(t, ((0, 0), (WINDOW, WINDOW), (0, 0), (0, 0)))
    views = [tp[:, i * BLOCK: i * BLOCK + N].reshape(B, nb, BLOCK, KV_HEADS, HEAD_DIM)
             for i in range(N_BAND)]
    return jnp.concatenate(views, axis=2)


def _window_attention(q, k, v, kc, vc, sink):
    B, N = q.shape[0], q.shape[1]
    nb = N // BLOCK
    nw = N_BAND * BLOCK
    qb = q.reshape(B, nb, BLOCK, KV_HEADS, GQA_GROUP, HEAD_DIM) * (HEAD_DIM ** -0.5)
    kb, vb = _band(k), _band(v)
    s_win = jnp.einsum('bnqkgd,bnjkd->bkgnqj', qb, kb).astype(jnp.float32)
    q_idx = jnp.arange(BLOCK)[:, None]
    j_idx = jnp.arange(nw)[None, :]
    key_pos = jnp.arange(nb)[:, None, None] * BLOCK + j_idx[None] - WINDOW
    mask = (jnp.abs(j_idx - WINDOW - q_idx) <= WINDOW)[None] & (key_pos >= 0) & (key_pos < N)
    s_win = jnp.where(mask, s_win, -jnp.inf)
    s_ctx = jnp.einsum('bnqkgd,blkd->bkgnql', qb, kc).astype(jnp.float32)
    s_sink = jnp.broadcast_to(sink.astype(jnp.float32).reshape(1, KV_HEADS, GQA_GROUP, 1, 1, 1),
                              s_win.shape[:-1] + (1,))
    p = jax.nn.softmax(jnp.concatenate([s_win, s_ctx, s_sink], axis=-1), axis=-1).astype(v.dtype)
    L = kc.shape[1]
    o = (jnp.einsum('bkgnqj,bnjkd->bnqkgd', p[..., :nw], vb)
         + jnp.einsum('bkgnql,blkd->bnqkgd', p[..., nw:nw + L], vc))
    return o.reshape(B, N, ATTN_W)


def _context_attention(qc, kc, vc, sink):
    B, L = qc.shape[0], qc.shape[1]
    qg = qc.reshape(B, L, KV_HEADS, GQA_GROUP, HEAD_DIM) * (HEAD_DIM ** -0.5)
    s = jnp.einsum('blkgd,bmkd->bkglm', qg, kc).astype(jnp.float32)
    s_sink = jnp.broadcast_to(sink.astype(jnp.float32).reshape(1, KV_HEADS, GQA_GROUP, 1, 1),
                              s.shape[:-1] + (1,))
    p = jax.nn.softmax(jnp.concatenate([s, s_sink], axis=-1), axis=-1).astype(vc.dtype)
    o = jnp.einsum('bkglm,bmkd->blkgd', p[..., :L], vc)
    return o.reshape(B, L, ATTN_W)


def _fourier_mix(f, w_f, b_f):
    B, N = f.shape[0], f.shape[1]
    g = f.reshape(B, N, F_GROUPS, F_DIM).astype(jnp.float32)
    z = jnp.fft.fft2(g, axes=(1, 3), norm='ortho').real.astype(f.dtype)
    y = jnp.einsum('bngc,gcd->bngd', z, w_f) + b_f
    return y.reshape(B, N, F_W)


def _spatial_gating(u, v, ln_g, ln_b, w_s, b_s):
    B, N = u.shape[0], u.shape[1]
    nc = N // CHUNK
    u = jax.nn.gelu(u).reshape(B, nc, CHUNK, C_HEADS, C_DIM)
    v = _ln_affine(jax.nn.gelu(v).reshape(B, nc, CHUNK, C_HEADS, C_DIM), ln_g, ln_b)
    vs = jnp.einsum('hpq,bcqhd->bcphd', w_s, v) + b_s.T[:, :, None]
    return (u * vs).reshape(B, N, C_W)


def _split_heads(q, k, v):
    B, N = q.shape[0], q.shape[1]
    return (q.reshape(B, N, ATTN_HEADS, HEAD_DIM), k.reshape(B, N, KV_HEADS, HEAD_DIM),
            v.reshape(B, N, KV_HEADS, HEAD_DIM))


def _swiglu(h, w1, w2):
    a, b = jnp.split(h @ w1, 2, axis=-1)
    return (jax.nn.silu(a) * b) @ w2


def _layer(x, xc, c, c_ctx, rope, w_ada, b_ada, w_in, w_out, attn_sink, w_fourier, b_fourier,
           sgu_ln_g, sgu_ln_b, w_spatial, b_spatial, ln1_g, ln1_b, w_ffn_in, w_ffn_out,
           ln2_g, ln2_b, last):
    sh_m, sc_m, g_m, sh_f, sc_f, g_f = _ada(c, w_ada, b_ada)
    csh_m, csc_m, cg_m, csh_f, csc_f, cg_f = _ada(c_ctx[None], w_ada, b_ada)
    B, L = xc.shape[0], xc.shape[1]

    hc = _modulate(xc, csh_m, csc_m)
    if last:
        kc, vc = jnp.split(hc @ w_in[:, SPLITS[0]:SPLITS[2]], 2, axis=-1)
        kc = kc.reshape(B, L, KV_HEADS, HEAD_DIM)
        vc = vc.reshape(B, L, KV_HEADS, HEAD_DIM)
        xc_new = xc
    else:
        qc, kc, vc, fc, uc, gc = jnp.split(hc @ w_in, SPLITS, axis=-1)
        qc, kc, vc = _split_heads(qc, kc, vc)
        oc = jnp.concatenate([_context_attention(qc, kc, vc, attn_sink),
                              _fourier_mix(fc, w_fourier, b_fourier),
                              _spatial_gating(uc, gc, sgu_ln_g, sgu_ln_b, w_spatial, b_spatial)],
                             axis=-1)
        xc_new = _post_norm(xc, oc @ w_out, cg_m, ln1_g, ln1_b)
        xc_new = _post_norm(xc_new, _swiglu(_modulate(xc_new, csh_f, csc_f), w_ffn_in, w_ffn_out),
                            cg_f, ln2_g, ln2_b)

    h = _modulate(x, sh_m, sc_m)
    q, k, v, f, u, g = jnp.split(h @ w_in, SPLITS, axis=-1)
    q, k, v = _split_heads(q, k, v)
    q = _apply_axial_rope(q, rope)
    k = _apply_axial_rope(k, rope)
    o = jnp.concatenate([_window_attention(q, k, v, kc, vc, attn_sink),
                         _fourier_mix(f, w_fourier, b_fourier),
                         _spatial_gating(u, g, sgu_ln_g, sgu_ln_b, w_spatial, b_spatial)], axis=-1)
    x = _post_norm(x, o @ w_out, g_m, ln1_g, ln1_b)
    x = _post_norm(x, _swiglu(_modulate(x, sh_f, sc_f), w_ffn_in, w_ffn_out), g_f, ln2_g, ln2_b)
    return x, xc_new


def setup_inputs(seed: int = 0) -> dict:
    key = jax.random.key(seed)
    ks = jax.random.split(key, 24)

    def nrm(k, shape, scale):
        return jax.random.normal(k, shape, jnp.float32) * scale

    return {
        'x': nrm(ks[0], (BATCH, SEQ, D_MODEL), 1.0),
        'c': nrm(ks[1], (BATCH, D_MODEL), 1.0),
        'ctx': nrm(ks[2], (BATCH, CTX_LEN, D_MODEL), 1.0),
        'c_ctx': nrm(ks[3], (D_MODEL,), 1.0),
        'w_ada': nrm(ks[4], (DEPTH, D_MODEL, 6 * D_MODEL), 0.5 * D_MODEL ** -0.5),
        'b_ada': nrm(ks[5], (DEPTH, 6 * D_MODEL), 0.01),
        'w_in': nrm(ks[6], (DEPTH, D_MODEL, D_IN), D_MODEL ** -0.5),
        'w_out': nrm(ks[7], (DEPTH, D_MIX, D_MODEL), BETA * D_MIX ** -0.5),
        'attn_sink': nrm(ks[8], (DEPTH, ATTN_HEADS), 1.0),
        'w_fourier': nrm(ks[9], (DEPTH, F_GROUPS, F_DIM, F_DIM), F_DIM ** -0.5),
        'b_fourier': nrm(ks[10], (DEPTH, F_GROUPS, F_DIM), 0.01),
        'sgu_ln_g': 1.0 + nrm(ks[11], (DEPTH, C_HEADS, C_DIM), 0.01),
        'sgu_ln_b': nrm(ks[12], (DEPTH, C_HEADS, C_DIM), 0.01),
        'w_spatial': nrm(ks[13], (DEPTH, C_HEADS, CHUNK, CHUNK), CHUNK ** -0.5),
        'b_spatial': 1.0 + nrm(ks[14], (DEPTH, C_HEADS, CHUNK), 0.01),
        'ln1_g': 1.0 + nrm(ks[15], (DEPTH, D_MODEL), 0.01),
        'ln1_b': nrm(ks[16], (DEPTH, D_MODEL), 0.01),
        'w_ffn_in': nrm(ks[17], (DEPTH, D_MODEL, 2 * D_FF), D_MODEL ** -0.5),
        'w_ffn_out': nrm(ks[18], (DEPTH, D_FF, D_MODEL), BETA * D_FF ** -0.5),
        'ln2_g': 1.0 + nrm(ks[19], (DEPTH, D_MODEL), 0.01),
        'ln2_b': nrm(ks[20], (DEPTH, D_MODEL), 0.01),
    }


def reference(x, c, ctx, c_ctx, w_ada, b_ada, w_in, w_out, attn_sink, w_fourier, b_fourier,
              sgu_ln_g, sgu_ln_b, w_spatial, b_spatial, ln1_g, ln1_b, w_ffn_in, w_ffn_out,
              ln2_g, ln2_b):
    rope = _axial_rope_tables(x.shape[1])
    xc = ctx
    for l in range(DEPTH):
        x, xc = _layer(x, xc, c, c_ctx, rope, w_ada[l], b_ada[l], w_in[l], w_out[l], attn_sink[l],
                       w_fourier[l], b_fourier[l], sgu_ln_g[l], sgu_ln_b[l], w_spatial[l],
                       b_spatial[l], ln1_g[l], ln1_b[l], w_ffn_in[l], w_ffn_out[l], ln2_g[l],
                       ln2_b[l], l == DEPTH - 1)
    return x
```

```python
import functools
import math

import numpy as np
import jax
import jax.numpy as jnp
from jax import lax
from jax.experimental import pallas as pl
from jax.experimental.pallas import tpu as pltpu

F32 = jnp.float32
BF16 = jnp.bfloat16

D_MODEL = 1024
DEPTH = 2
GRID_W = 64
HEAD_DIM = 64
ATTN_HEADS = 8
KV_HEADS = 2
WINDOW = 128
BLOCK = 128
ROPE_BASE = 10000.0
AXIS_DIM = HEAD_DIM // 2
F_GROUPS = 4
F_DIM = 64
C_HEADS = 4
C_DIM = 64
CHUNK = 128
D_FF = 2816
ATTN_W = ATTN_HEADS * HEAD_DIM
KV_W = KV_HEADS * HEAD_DIM
F_W = F_GROUPS * F_DIM
C_W = C_HEADS * C_DIM
D_IN = ATTN_W + 2 * KV_W + F_W + 2 * C_W
ALPHA = (2 * DEPTH) ** 0.25
LN_EPS = 1e-6
NEG_BIG = -1e30

LANES = 128
DFT_B = 128
VMEM_LIMIT = 56 * 1024 * 1024


def _dot(a, b):
    return jnp.dot(a, b, preferred_element_type=F32)


def _split(a):
    hi = a.astype(BF16)
    lo = (a - hi.astype(F32)).astype(BF16)
    return hi, lo


def _dot3(a_hi, a_lo, b_hi, b_lo):
    return _dot(a_hi, b_hi) + _dot(a_lo, b_hi) + _dot(a_hi, b_lo)


def _ln(x):
    mu = jnp.mean(x, axis=-1, keepdims=True)
    xc = x - mu
    var = jnp.mean(xc * xc, axis=-1, keepdims=True)
    return xc * lax.rsqrt(var + LN_EPS)


def _gelu(x):
    return 0.5 * x * (1.0 + jnp.tanh(math.sqrt(2.0 / math.pi) * (x + 0.044715 * (x * x * x))))


def _silu(x):
    return x / (1.0 + jnp.exp(-x))


def _mod_rows(mod, row):
    return [mod[row:row + 1, i * D_MODEL:(i + 1) * D_MODEL] for i in range(6)]


def _rope(t, cos, sin_a, sin_b):
    return (t * cos + pltpu.roll(t, LANES - AXIS_DIM // 2, 1) * sin_a
            + pltpu.roll(t, AXIS_DIM // 2, 1) * sin_b)


def _project(h, w_in_ref, lo, hi):
    return _dot(h, w_in_ref[:, lo:hi])


def _dup_halves(t):
    lane = lax.broadcasted_iota(jnp.int32, (1, LANES), 1)
    low = lane < HEAD_DIM
    sw = pltpu.roll(t, HEAD_DIM, 1)
    return jnp.where(low, t, sw), jnp.where(low, sw, t)


def _channel_dft(f, bd_hi_ref, bd_lo_ref):
    f_hi, f_lo = _split(f)
    w = _dot3(f_hi, f_lo, bd_hi_ref[...], bd_lo_ref[...])
    return w[:, :F_W], w[:, F_W:]


def _spatial_gate(u, g, avg_ref, lng_ref, lnb_ref, wscat_ref, bs_ref):
    rows = u.shape[0]
    ug = _gelu(u)
    vg = _gelu(g)
    avg = avg_ref[...]
    v_hi, v_lo = _split(vg)
    mu = _dot(v_hi, avg) + _dot(v_lo, avg)
    vc = vg - mu
    c_hi, c_lo = _split(vc * vc)
    var = _dot(c_hi, avg) + _dot(c_lo, avg)
    vn = vc * lax.rsqrt(var + LN_EPS) * lng_ref[...] + lnb_ref[...]
    lane = lax.broadcasted_iota(jnp.int32, (1, C_W), 1)
    vnb = vn.astype(BF16)
    zero = jnp.zeros_like(vnb)
    outs = []
    for c in range(rows // CHUNK):
        blk = vnb[c * CHUNK:(c + 1) * CHUNK]
        rhs = jnp.concatenate(
            [jnp.where((lane >= h * C_DIM) & (lane < (h + 1) * C_DIM), blk, zero[:CHUNK])
             for h in range(C_HEADS)], axis=0)
        vs = _dot(wscat_ref[...], rhs) + bs_ref[...]
        outs.append(ug[c * CHUNK:(c + 1) * CHUNK] * vs)
    return jnp.concatenate(outs, axis=0) if len(outs) > 1 else outs[0]


def _softmax_pv(s, sink_col, vv):
    m = jnp.maximum(jnp.max(s, axis=1, keepdims=True), sink_col)
    e = jnp.exp(s - m)
    denom = jnp.sum(e, axis=1, keepdims=True) + jnp.exp(sink_col - m)
    return _dot(e.astype(BF16), vv) / denom


def _stack_heads(q, kh):
    lane = lax.broadcasted_iota(jnp.int32, (1, LANES), 1)
    low = lane < HEAD_DIM
    p0 = q[:, kh * 256:kh * 256 + LANES]
    p1 = q[:, kh * 256 + LANES:kh * 256 + 2 * LANES]
    z = jnp.zeros_like(p0)
    return jnp.concatenate([jnp.where(low, p0, z), jnp.where(low, z, p0),
                            jnp.where(low, p1, z), jnp.where(low, z, p1)], axis=0), low


def _unstack_heads(o4, rows, low):
    return (jnp.where(low, o4[0:rows], o4[rows:2 * rows]),
            jnp.where(low, o4[2 * rows:3 * rows], o4[3 * rows:4 * rows]))


def _sink_col(sink_ref, kh, rows):
    return jnp.concatenate(
        [jnp.broadcast_to(sink_ref[4 * kh + g:4 * kh + g + 1, 0:1], (rows, 1)) for g in range(4)],
        axis=0)


def _out_ffn(x, o_cat, mod_row, w_out_ref, ln1g, ln1b, w1_ref, w2_ref, ln2g, ln2b, n_chunks):
    g_m, sh_f, sc_f, g_f = mod_row[2], mod_row[3], mod_row[4], mod_row[5]
    y = _dot(o_cat, w_out_ref[...])
    x1 = _ln(ALPHA * x + g_m * y) * ln1g + ln1b
    h2 = (_ln(x1) * (1.0 + sc_f) + sh_f).astype(BF16)
    cw = D_FF // n_chunks
    y2 = None
    for c in range(n_chunks):
        a = _dot(h2, w1_ref[:, c * cw:(c + 1) * cw])
        b = _dot(h2, w1_ref[:, D_FF + c * cw:D_FF + (c + 1) * cw])
        t = (_silu(a) * b).astype(BF16)
        part = _dot(t, w2_ref[c * cw:(c + 1) * cw, :])
        y2 = part if y2 is None else y2 + part
    return _ln(ALPHA * x1 + g_f * y2) * ln2g + ln2b


def _ada_kernel(cv_ref, w_ref, b_ref, o_ref):
    s = _silu(cv_ref[...])
    o_ref[...] = jnp.dot(s, w_ref[...], precision=lax.Precision.HIGHEST,
                         preferred_element_type=F32) + b_ref[...]


def _inproj_kernel(x_ref, mod_ref, w_in_ref, cos_ref, sina_ref, sinb_ref, bd_hi_ref, bd_lo_ref,
                   avg_ref, lng_ref, lnb_ref, wscat_ref, bs_ref,
                   q_ref, kd_ref, vd_ref, fr_ref, fi_ref, sg_ref):
    sh, sc = _mod_rows(mod_ref[...], 0)[:2]
    h = (_ln(x_ref[...]) * (1.0 + sc) + sh).astype(BF16)
    cos, sin_a, sin_b = cos_ref[...], sina_ref[...], sinb_ref[...]
    for p in range(ATTN_W // LANES):
        qp = _rope(_project(h, w_in_ref, p * LANES, (p + 1) * LANES), cos, sin_a, sin_b)
        q_ref[:, p * LANES:(p + 1) * LANES] = (qp * HEAD_DIM ** -0.5).astype(BF16)
    k = _rope(_project(h, w_in_ref, ATTN_W, ATTN_W + KV_W), cos, sin_a, sin_b)
    k0, k1 = _dup_halves(k)
    kd_ref[:, :LANES] = k0.astype(BF16)
    kd_ref[:, LANES:] = k1.astype(BF16)
    v0, v1 = _dup_halves(_project(h, w_in_ref, ATTN_W + KV_W, ATTN_W + 2 * KV_W))
    vd_ref[:, :LANES] = v0.astype(BF16)
    vd_ref[:, LANES:] = v1.astype(BF16)
    f0 = ATTN_W + 2 * KV_W
    fr, fi = _channel_dft(_project(h, w_in_ref, f0, f0 + F_W), bd_hi_ref, bd_lo_ref)
    fr_ref[...] = fr
    fi_ref[...] = fi
    u = _project(h, w_in_ref, f0 + F_W, f0 + F_W + C_W)
    g = _project(h, w_in_ref, f0 + F_W + C_W, D_IN)
    sg_ref[...] = _spatial_gate(u, g, avg_ref, lng_ref, lnb_ref, wscat_ref, bs_ref).astype(BF16)


def _attn_kernel(q_ref, kd_ref, vd_ref, kc_ref, vc_ref, sink_ref, o_ref, *, n_tokens):
    n = pl.program_id(0)
    span = 3 * BLOCK
    start = pl.multiple_of(jnp.clip((n - 1) * BLOCK, 0, n_tokens - span), BLOCK)
    kw = kd_ref[pl.ds(start, span), :]
    vw = vd_ref[pl.ds(start, span), :]
    n_ctx = kc_ref.shape[0]
    qpos = n * BLOCK + lax.broadcasted_iota(jnp.int32, (BLOCK, span + n_ctx), 0)
    col = lax.broadcasted_iota(jnp.int32, (BLOCK, span + n_ctx), 1)
    ok = (jnp.abs(start + col - qpos) <= WINDOW) | (col >= span)
    ok4 = jnp.concatenate([ok] * 4, axis=0)
    q = q_ref[...]
    for kh in range(KV_HEADS):
        kk = jnp.concatenate([kw[:, kh * LANES:(kh + 1) * LANES],
                              kc_ref[:, kh * LANES:(kh + 1) * LANES]], axis=0)
        vv = jnp.concatenate([vw[:, kh * LANES:(kh + 1) * LANES],
                              vc_ref[:, kh * LANES:(kh + 1) * LANES]], axis=0)
        q4, low = _stack_heads(q, kh)
        s = lax.dot_general(q4, kk, (((1,), (1,)), ((), ())), preferred_element_type=F32)
        s = jnp.where(ok4, s, NEG_BIG)
        o4 = _softmax_pv(s, _sink_col(sink_ref, kh, BLOCK), vv)
        pa, pb = _unstack_heads(o4, BLOCK, low)
        o_ref[:, kh * 256:kh * 256 + LANES] = pa.astype(BF16)
        o_ref[:, kh * 256 + LANES:kh * 256 + 2 * LANES] = pb.astype(BF16)


def _dft_stage1_kernel(xr_ref, xi_ref, m_hi_ref, m_lo_ref, yr_ref, yi_ref):
    a = xr_ref.shape[0]
    x_hi, x_lo = _split(jnp.concatenate([xr_ref[...], xi_ref[...]], axis=0))
    y = _dot3(m_hi_ref[...], m_lo_ref[...], x_hi, x_lo)
    yr_ref[...] = y[:a]
    yi_ref[...] = y[a:]


def _dft_stage2_kernel(yr_ref, yi_ref, tc_ref, ts_ref, c_ref, s_ref, wf_ref, bf_ref, o_ref):
    c, s = c_ref[...], s_ref[...]
    for j in range(yr_ref.shape[0]):
        tc, ts = tc_ref[j:j + 1, :], ts_ref[j:j + 1, :]
        g = jnp.concatenate([c * tc - s * ts, c * ts + s * tc], axis=1)
        y = jnp.concatenate([yr_ref[j], yi_ref[j]], axis=0)
        g_hi, g_lo = _split(g)
        y_hi, y_lo = _split(y)
        z = _dot3(g_hi, g_lo, y_hi, y_lo)
        o_ref[j] = (_dot(z.astype(BF16), wf_ref[...]) + bf_ref[...]).astype(BF16)


def _outffn_kernel(x_ref, oa_ref, of_ref, sg_ref, mod_ref, w_out_ref, ln1g_ref, ln1b_ref,
                   w1_ref, w2_ref, ln2g_ref, ln2b_ref, o_ref, *, n_chunks):
    o_cat = jnp.concatenate([oa_ref[...], of_ref[...], sg_ref[...]], axis=1)
    o_ref[...] = _out_ffn(x_ref[...], o_cat, _mod_rows(mod_ref[...], 0), w_out_ref,
                          ln1g_ref[...], ln1b_ref[...], w1_ref, w2_ref,
                          ln2g_ref[...], ln2b_ref[...], n_chunks)


def _ctx_kernel(xc_ref, mod0_ref, mod1_ref, w_in0_ref, w_in1_ref, bd_hi_ref, bd_lo_ref, avg_ref,
                lng_ref, lnb_ref, wscat_ref, bs_ref, sink_ref, cs_hi_ref, cs_lo_ref, wf_ref, bf_ref,
                w_out_ref, ln1g_ref, ln1b_ref, w1_ref, w2_ref, ln2g_ref, ln2b_ref,
                kc0_ref, vc0_ref, kc1_ref, vc1_ref, *, n_chunks):
    xc = xc_ref[...]
    n_ctx = xc.shape[0]
    m0 = _mod_rows(mod0_ref[...], 1)
    h = (_ln(xc) * (1.0 + m0[1]) + m0[0]).astype(BF16)
    q = (_project(h, w_in0_ref, 0, ATTN_W) * HEAD_DIM ** -0.5).astype(BF16)
    k0, k1 = _dup_halves(_project(h, w_in0_ref, ATTN_W, ATTN_W + KV_W))
    v0, v1 = _dup_halves(_project(h, w_in0_ref, ATTN_W + KV_W, ATTN_W + 2 * KV_W))
    kd = [k0.astype(BF16), k1.astype(BF16)]
    vd = [v0.astype(BF16), v1.astype(BF16)]
    kc0_ref[:, :LANES], kc0_ref[:, LANES:] = kd[0], kd[1]
    vc0_ref[:, :LANES], vc0_ref[:, LANES:] = vd[0], vd[1]
    pairs = []
    for kh in range(KV_HEADS):
        q4, low = _stack_heads(q, kh)
        s = lax.dot_general(q4, kd[kh], (((1,), (1,)), ((), ())), preferred_element_type=F32)
        o4 = _softmax_pv(s, _sink_col(sink_ref, kh, n_ctx), vd[kh])
        pairs.extend(_unstack_heads(o4, n_ctx, low))
    f0 = ATTN_W + 2 * KV_W
    fr, fi = _channel_dft(_project(h, w_in0_ref, f0, f0 + F_W), bd_hi_ref, bd_lo_ref)
    y_hi, y_lo = _split(jnp.concatenate([fr, fi], axis=0))
    z = _dot3(cs_hi_ref[...], cs_lo_ref[...], y_hi, y_lo)
    o_f = _dot(z.astype(BF16), wf_ref[...]) + bf_ref[...]
    u = _project(h, w_in0_ref, f0 + F_W, f0 + F_W + C_W)
    g = _project(h, w_in0_ref, f0 + F_W + C_W, D_IN)
    sg = _spatial_gate(u, g, avg_ref, lng_ref, lnb_ref, wscat_ref, bs_ref)
    o_cat = jnp.concatenate([p.astype(BF16) for p in pairs] + [o_f.astype(BF16), sg.astype(BF16)],
                            axis=1)
    xc1 = _out_ffn(xc, o_cat, m0, w_out_ref, ln1g_ref[...], ln1b_ref[...], w1_ref, w2_ref,
                   ln2g_ref[...], ln2b_ref[...], n_chunks)
    m1 = _mod_rows(mod1_ref[...], 1)
    h1 = (_ln(xc1) * (1.0 + m1[1]) + m1[0]).astype(BF16)
    k0, k1 = _dup_halves(_project(h1, w_in1_ref, ATTN_W, ATTN_W + KV_W))
    v0, v1 = _dup_halves(_project(h1, w_in1_ref, ATTN_W + KV_W, ATTN_W + 2 * KV_W))
    kc1_ref[:, :LANES], kc1_ref[:, LANES:] = k0.astype(BF16), k1.astype(BF16)
    vc1_ref[:, :LANES], vc1_ref[:, LANES:] = v0.astype(BF16), v1.astype(BF16)


def _hi_lo(a):
    a = np.asarray(a, np.float64)
    hi = jnp.asarray(a, F32).astype(BF16)
    lo = (jnp.asarray(a, F32) - hi.astype(F32)).astype(BF16)
    return hi, lo


def _dft_cos_sin(n):
    idx = np.arange(n, dtype=np.int64)
    ang = 2.0 * np.pi * ((idx[:, None] * idx[None, :]) % n) / n
    return np.cos(ang), np.sin(ang)


def _channel_dft_tables():
    c, s = _dft_cos_sin(F_DIM)
    bd = np.zeros((F_W, 2 * F_W))
    for g in range(F_GROUPS):
        sl = slice(g * F_DIM, (g + 1) * F_DIM)
        bd[sl, sl] = c / math.sqrt(F_DIM)
        bd[sl, F_W + g * F_DIM:F_W + (g + 1) * F_DIM] = -s / math.sqrt(F_DIM)
    return _hi_lo(bd)


def _group_avg_table():
    a = np.zeros((C_W, C_W))
    for h in range(C_HEADS):
        a[h * C_DIM:(h + 1) * C_DIM, h * C_DIM:(h + 1) * C_DIM] = 1.0 / C_DIM
    return jnp.asarray(a, BF16)


def _rope_tables(n):
    row = (np.arange(n) // GRID_W).astype(np.float32)
    col = (np.arange(n) % GRID_W).astype(np.float32)
    freqs = jnp.asarray(ROPE_BASE, F32) ** (-jnp.arange(0, AXIS_DIM, 2, dtype=F32) / AXIS_DIM)
    ang_r = jnp.asarray(row)[:, None] * freqs
    ang_c = jnp.asarray(col)[:, None] * freqs
    cr, sr, cc, sc = jnp.cos(ang_r), jnp.sin(ang_r), jnp.cos(ang_c), jnp.sin(ang_c)
    z = jnp.zeros_like(sr)
    reps = LANES // HEAD_DIM
    cos = jnp.tile(jnp.concatenate([cr, cr, cc, cc], axis=1), (1, reps))
    sin_a = jnp.tile(jnp.concatenate([-sr, z, -sc, z], axis=1), (1, reps))
    sin_b = jnp.tile(jnp.concatenate([z, sr, z, sc], axis=1), (1, reps))
    return cos, sin_a, sin_b


def _stage1_tables(a):
    c, s = _dft_cos_sin(a)
    m = np.block([[c, s], [-s, c]]) / math.sqrt(a)
    return _hi_lo(m)


def _stage2_tables(a):
    n = a * DFT_B
    k1 = np.arange(a, dtype=np.int64)[:, None]
    b = np.arange(DFT_B, dtype=np.int64)[None, :]
    ang = 2.0 * np.pi * ((k1 * b) % n) / n
    scale = 1.0 / math.sqrt(DFT_B)
    c, s = _dft_cos_sin(DFT_B)
    return (jnp.asarray(np.cos(ang) * scale, F32), jnp.asarray(np.sin(ang) * scale, F32),
            jnp.asarray(c, F32), jnp.asarray(s, F32))


def _ctx_dft_tables(n_ctx):
    c, s = _dft_cos_sin(n_ctx)
    return _hi_lo(np.concatenate([c, s], axis=1) / math.sqrt(n_ctx))


def _const_spec(shape):
    nd = len(shape)
    return pl.BlockSpec(shape, lambda *_: (0,) * nd, pipeline_mode=pl.Buffered(1))


def _params(*sem):
    return pltpu.CompilerParams(dimension_semantics=sem, vmem_limit_bytes=VMEM_LIMIT)


def _ada(cv, w_ada, b_ada):
    tn = 1536
    return pl.pallas_call(
        _ada_kernel,
        grid=(DEPTH, 6 * D_MODEL // tn),
        in_specs=[pl.BlockSpec((8, D_MODEL), lambda l, j: (0, 0)),
                  pl.BlockSpec((None, D_MODEL, tn), lambda l, j: (l, 0, j)),
                  pl.BlockSpec((None, 1, tn), lambda l, j: (l, 0, j))],
        out_specs=pl.BlockSpec((None, 8, tn), lambda l, j: (l, 0, j)),
        out_shape=jax.ShapeDtypeStruct((DEPTH, 8, 6 * D_MODEL), F32),
        compiler_params=_params("arbitrary", "arbitrary"),
        name="ada",
    )(cv, w_ada, b_ada.reshape(DEPTH, 1, 6 * D_MODEL))


def _inproj(x, mod, w_in, rope, consts, tm):
    n = x.shape[0]
    row = lambda w: pl.BlockSpec((tm, w), lambda i: (i, 0))
    cspecs = [_const_spec(c.shape) for c in consts]
    return pl.pallas_call(
        _inproj_kernel,
        grid=(n // tm,),
        in_specs=[row(D_MODEL), _const_spec(mod.shape), _const_spec(w_in.shape),
                  row(LANES), row(LANES), row(LANES)] + cspecs,
        out_specs=[row(ATTN_W), row(2 * LANES), row(2 * LANES), row(F_W), row(F_W), row(C_W)],
        out_shape=[jax.ShapeDtypeStruct((n, ATTN_W), BF16), jax.ShapeDtypeStruct((n, 2 * LANES), BF16),
                   jax.ShapeDtypeStruct((n, 2 * LANES), BF16), jax.ShapeDtypeStruct((n, F_W), F32),
                   jax.ShapeDtypeStruct((n, F_W), F32), jax.ShapeDtypeStruct((n, C_W), BF16)],
        compiler_params=_params("arbitrary"),
        name="inproj",
    )(x, mod, w_in, *rope, *consts)


def _attention(q, kd, vd, kc, vc, sink_tab):
    n = q.shape[0]
    return pl.pallas_call(
        functools.partial(_attn_kernel, n_tokens=n),
        grid=(n // BLOCK,),
        in_specs=[pl.BlockSpec((BLOCK, ATTN_W), lambda i: (i, 0)), _const_spec(kd.shape),
                  _const_spec(vd.shape), _const_spec(kc.shape), _const_spec(vc.shape),
                  _const_spec(sink_tab.shape)],
        out_specs=pl.BlockSpec((BLOCK, ATTN_W), lambda i: (i, 0)),
        out_shape=jax.ShapeDtypeStruct((n, ATTN_W), BF16),
        compiler_params=_params("arbitrary"),
        name="attention",
    )(q, kd, vd, kc, vc, sink_tab)


def _fourier(fr, fi, tabs1, tabs2, wf_bd, bf_row):
    n = fr.shape[0]
    a = n // DFT_B
    cols = DFT_B * F_W
    tn = min(cols, 2048)
    m_hi, m_lo = tabs1
    colspec = pl.BlockSpec((a, tn), lambda j: (0, j))
    yr, yi = pl.pallas_call(
        _dft_stage1_kernel,
        grid=(cols // tn,),
        in_specs=[colspec, colspec, _const_spec(m_hi.shape), _const_spec(m_lo.shape)],
        out_specs=[colspec, colspec],
        out_shape=[jax.ShapeDtypeStruct((a, cols), F32)] * 2,
        compiler_params=_params("arbitrary"),
        name="dft_stage1",
    )(fr.reshape(a, cols), fi.reshape(a, cols), m_hi, m_lo)
    tc, ts, c, s = tabs2
    slab = pl.BlockSpec((8, DFT_B, F_W), lambda i: (i, 0, 0))
    tw = pl.BlockSpec((8, DFT_B), lambda i: (i, 0))
    zt = pl.pallas_call(
        _dft_stage2_kernel,
        grid=(a // 8,),
        in_specs=[slab, slab, tw, tw, _const_spec(c.shape), _const_spec(s.shape),
                  _const_spec(wf_bd.shape), _const_spec(bf_row.shape)],
        out_specs=slab,
        out_shape=jax.ShapeDtypeStruct((a, DFT_B, F_W), BF16),
        compiler_params=_params("arbitrary"),
        name="dft_stage2",
    )(yr.reshape(a, DFT_B, F_W), yi.reshape(a, DFT_B, F_W), tc, ts, c, s, wf_bd, bf_row)
    return jnp.transpose(zt, (1, 0, 2)).reshape(n, F_W)


def _outffn(x, oa, of, sg, mod, w_out, ln1g, ln1b, w1, w2, ln2g, ln2b, tm, n_chunks):
    n = x.shape[0]
    row = lambda w: pl.BlockSpec((tm, w), lambda i: (i, 0))
    consts = [mod, w_out, ln1g, ln1b, w1, w2, ln2g, ln2b]
    return pl.pallas_call(
        functools.partial(_outffn_kernel, n_chunks=n_chunks),
        grid=(n // tm,),
        in_specs=[row(D_MODEL), row(ATTN_W), row(F_W), row(C_W)] + [_const_spec(c.shape) for c in consts],
        out_specs=row(D_MODEL),
        out_shape=jax.ShapeDtypeStruct((n, D_MODEL), F32),
        compiler_params=_params("arbitrary"),
        name="outffn",
    )(x, oa, of, sg, *consts)


def _context(xc, mods, w_in, shared, sgu0, sink_tab0, ctx_tabs, wf0, bf0, w_out0, ln1, w1_0, w2_0, ln2,
             n_chunks):
    n_ctx = xc.shape[0]
    bd_hi, bd_lo, avg = shared
    cs_hi, cs_lo = ctx_tabs
    args = [xc, mods[0], mods[1], w_in[0], w_in[1], bd_hi, bd_lo, avg, *sgu0, sink_tab0, cs_hi, cs_lo,
            wf0, bf0, w_out0, *ln1, w1_0, w2_0, *ln2]
    kv = jax.ShapeDtypeStruct((n_ctx, 2 * LANES), BF16)
    return pl.pallas_call(
        functools.partial(_ctx_kernel, n_chunks=n_chunks),
        out_shape=[kv, kv, kv, kv],
        compiler_params=pltpu.CompilerParams(vmem_limit_bytes=VMEM_LIMIT),
        name="context",
    )(*args)


def _block_diag_wf(w_f):
    bd = jnp.zeros((F_W, F_W), F32)
    for g in range(F_GROUPS):
        bd = bd.at[g * F_DIM:(g + 1) * F_DIM, g * F_DIM:(g + 1) * F_DIM].set(w_f[g])
    return bd.astype(BF16)


def _sgu_consts(ln_g, ln_b, w_s, b_s):
    wscat = jnp.concatenate([w_s[h] for h in range(C_HEADS)], axis=1).astype(BF16)
    bs_tab = jnp.repeat(b_s.T, C_DIM, axis=1)
    return (ln_g.reshape(1, C_W), ln_b.reshape(1, C_W), wscat, bs_tab)


def _forward(x, c, ctx, c_ctx, w_ada, b_ada, w_in, w_out, attn_sink, w_fourier, b_fourier,
             sgu_ln_g, sgu_ln_b, w_spatial, b_spatial, ln1_g, ln1_b, w_ffn_in, w_ffn_out,
             ln2_g, ln2_b, tm_in=256, tm_out=256, n_chunks=2):
    n = x.shape[1]
    n_ctx = ctx.shape[1]
    xs = x[0]
    cv = jnp.zeros((8, D_MODEL), F32).at[0].set(c[0]).at[1].set(c_ctx)
    mods = _ada(cv, w_ada, b_ada)

    w_in_b = w_in.astype(BF16)
    w_out_b = w_out.astype(BF16)
    w1_b = w_ffn_in.astype(BF16)
    w2_b = w_ffn_out.astype(BF16)
    bd_hi, bd_lo = _channel_dft_tables()
    avg = _group_avg_table()
    rope = _rope_tables(n)
    tabs1 = _stage1_tables(n // DFT_B)
    tabs2 = _stage2_tables(n // DFT_B)
    ctx_tabs = _ctx_dft_tables(n_ctx)
    row = lambda v: v.reshape(1, -1)
    sgu = [_sgu_consts(sgu_ln_g[l], sgu_ln_b[l], w_spatial[l], b_spatial[l]) for l in range(DEPTH)]
    sink_tab = [jnp.broadcast_to(attn_sink[l][:, None], (ATTN_HEADS, LANES)) for l in range(DEPTH)]
    wf_bd = [_block_diag_wf(w_fourier[l]) for l in range(DEPTH)]
    bf_row = [b_fourier[l].reshape(1, F_W) for l in range(DEPTH)]

    kc0, vc0, kc1, vc1 = _context(
        ctx[0], mods, w_in_b, (bd_hi, bd_lo, avg), sgu[0], sink_tab[0], ctx_tabs, wf_bd[0], bf_row[0],
        w_out_b[0], (row(ln1_g[0]), row(ln1_b[0])), w1_b[0], w2_b[0], (row(ln2_g[0]), row(ln2_b[0])),
        n_chunks)
    ctx_kv = [(kc0, vc0), (kc1, vc1)]

    for l in range(DEPTH):
        consts = (bd_hi, bd_lo, avg) + sgu[l]
        q, kd, vd, fr, fi, sg = _inproj(xs, mods[l], w_in_b[l], rope, consts, tm_in)
        oa = _attention(q, kd, vd, ctx_kv[l][0], ctx_kv[l][1], sink_tab[l])
        of = _fourier(fr, fi, tabs1, tabs2, wf_bd[l], bf_row[l])
        xs = _outffn(xs, oa, of, sg, mods[l], w_out_b[l], row(ln1_g[l]), row(ln1_b[l]), w1_b[l], w2_b[l],
                     row(ln2_g[l]), row(ln2_b[l]), tm_out, n_chunks)
    return xs[None]


def kernel(x, c, ctx, c_ctx, w_ada, b_ada, w_in, w_out, attn_sink, w_fourier, b_fourier, sgu_ln_g,
           sgu_ln_b, w_spatial, b_spatial, ln1_g, ln1_b, w_ffn_in, w_ffn_out, ln2_g, ln2_b):
    return _forward(x, c, ctx, c_ctx, w_ada, b_ada, w_in, w_out, attn_sink, w_fourier, b_fourier,
                    sgu_ln_g, sgu_ln_b, w_spatial, b_spatial, ln1_g, ln1_b, w_ffn_in, w_ffn_out,
                    ln2_g, ln2_b)
```

```python
import functools
import math

import numpy as np
import jax
import jax.numpy as jnp
from jax import lax
from jax.experimental import pallas as pl
from jax.experimental.pallas import tpu as pltpu

F32 = jnp.float32
BF16 = jnp.bfloat16

D_MODEL = 1024
DEPTH = 2
GRID_W = 64
HEAD_DIM = 64
ATTN_HEADS = 8
KV_HEADS = 2
WINDOW = 128
BLOCK = 128
ROPE_BASE = 10000.0
AXIS_DIM = HEAD_DIM // 2
F_GROUPS = 4
F_DIM = 64
C_HEADS = 4
C_DIM = 64
CHUNK = 128
D_FF = 2816
ATTN_W = ATTN_HEADS * HEAD_DIM
KV_W = KV_HEADS * HEAD_DIM
F_W = F_GROUPS * F_DIM
C_W = C_HEADS * C_DIM
D_IN = ATTN_W + 2 * KV_W + F_W + 2 * C_W
ALPHA = (2 * DEPTH) ** 0.25
LN_EPS = 1e-6
NEG_BIG = -1e30

LANES = 128
SUBLANES = 8
DFT_B = 128
B_BLOCKS = DFT_B // SUBLANES
VMEM_LIMIT = 56 * 1024 * 1024


def _dot(a, b):
    return jnp.dot(a, b, preferred_element_type=F32)


def _split(a):
    hi = a.astype(BF16)
    lo = (a - hi.astype(F32)).astype(BF16)
    return hi, lo


def _dot3(a_hi, a_lo, b_hi, b_lo):
    return _dot(a_hi, b_hi) + _dot(a_lo, b_hi) + _dot(a_hi, b_lo)


def _ln(x):
    mu = jnp.mean(x, axis=-1, keepdims=True)
    xc = x - mu
    var = jnp.mean(xc * xc, axis=-1, keepdims=True)
    return xc * lax.rsqrt(var + LN_EPS)


def _gelu(x):
    return 0.5 * x * (1.0 + jnp.tanh(math.sqrt(2.0 / math.pi) * (x + 0.044715 * (x * x * x))))


def _silu(x):
    return x / (1.0 + jnp.exp(-x))


def _mod_rows(mod, row):
    return [mod[row:row + 1, i * D_MODEL:(i + 1) * D_MODEL] for i in range(6)]


def _rope(t, tabs):
    cos, sin_a, sin_b = tabs
    return (t * cos + pltpu.roll(t, LANES - AXIS_DIM // 2, 1) * sin_a
            + pltpu.roll(t, AXIS_DIM // 2, 1) * sin_b)


def _rope_tile(rt_ref, ct_ref, first_grid_row, rows):
    tabs = []
    for t in range(3):
        groups = [rt_ref[t, pl.ds(first_grid_row + r, 1), :] + ct_ref[t]
                  for r in range(rows // GRID_W)]
        tabs.append(jnp.concatenate(groups, axis=0) if len(groups) > 1 else groups[0])
    return tabs


def _project(h, w_in_ref, lo, hi):
    return _dot(h, w_in_ref[:, lo:hi])


def _dup_halves(t):
    lane = lax.broadcasted_iota(jnp.int32, (1, LANES), 1)
    low = lane < HEAD_DIM
    sw = pltpu.roll(t, HEAD_DIM, 1)
    return jnp.where(low, t, sw), jnp.where(low, sw, t)


def _channel_dft(f, bd_hi_ref, bd_lo_ref):
    f_hi, f_lo = _split(f)
    w = _dot3(f_hi, f_lo, bd_hi_ref[...], bd_lo_ref[...])
    return w[:, :F_W], w[:, F_W:]


def _spatial_gate(u, g, avg_ref, lng_ref, lnb_ref, wscat_ref, bs_ref):
    rows = u.shape[0]
    ug = _gelu(u)
    vg = _gelu(g)
    avg = avg_ref[...]
    v_hi, v_lo = _split(vg)
    mu = _dot(v_hi, avg) + _dot(v_lo, avg)
    vc = vg - mu
    c_hi, c_lo = _split(vc * vc)
    var = _dot(c_hi, avg) + _dot(c_lo, avg)
    vn = vc * lax.rsqrt(var + LN_EPS) * lng_ref[...] + lnb_ref[...]
    lane = lax.broadcasted_iota(jnp.int32, (1, C_W), 1)
    vnb = vn.astype(BF16)
    zero = jnp.zeros_like(vnb)
    outs = []
    for c in range(rows // CHUNK):
        blk = vnb[c * CHUNK:(c + 1) * CHUNK]
        rhs = jnp.concatenate(
            [jnp.where((lane >= h * C_DIM) & (lane < (h + 1) * C_DIM), blk, zero[:CHUNK])
             for h in range(C_HEADS)], axis=0)
        vs = _dot(wscat_ref[...], rhs) + bs_ref[...]
        outs.append(ug[c * CHUNK:(c + 1) * CHUNK] * vs)
    return jnp.concatenate(outs, axis=0) if len(outs) > 1 else outs[0]


def _softmax_pv(s, sink_col, vv):
    m = jnp.maximum(jnp.max(s, axis=1, keepdims=True), sink_col)
    e = jnp.exp(s - m)
    denom = jnp.sum(e, axis=1, keepdims=True) + jnp.exp(sink_col - m)
    return _dot(e.astype(BF16), vv) / denom


def _stack_heads(q, kh):
    lane = lax.broadcasted_iota(jnp.int32, (1, LANES), 1)
    low = lane < HEAD_DIM
    p0 = q[:, kh * 256:kh * 256 + LANES]
    p1 = q[:, kh * 256 + LANES:kh * 256 + 2 * LANES]
    z = jnp.zeros_like(p0)
    return jnp.concatenate([jnp.where(low, p0, z), jnp.where(low, z, p0),
                            jnp.where(low, p1, z), jnp.where(low, z, p1)], axis=0), low


def _unstack_heads(o4, rows, low):
    return (jnp.where(low, o4[0:rows], o4[rows:2 * rows]),
            jnp.where(low, o4[2 * rows:3 * rows], o4[3 * rows:4 * rows]))


def _sink_col(sink_ref, kh, rows):
    return jnp.concatenate(
        [jnp.broadcast_to(sink_ref[4 * kh + g:4 * kh + g + 1, 0:1], (rows, 1)) for g in range(4)],
        axis=0)


def _out_ffn(x, o_cat, mod_row, w_out_ref, ln1g, ln1b, w1_ref, w2_ref, ln2g, ln2b, n_chunks):
    g_m, sh_f, sc_f, g_f = mod_row[2], mod_row[3], mod_row[4], mod_row[5]
    y = _dot(o_cat, w_out_ref[...])
    x1 = _ln(ALPHA * x + g_m * y) * ln1g + ln1b
    h2 = (_ln(x1) * (1.0 + sc_f) + sh_f).astype(BF16)
    cw = D_FF // n_chunks
    y2 = None
    for c in range(n_chunks):
        a = _dot(h2, w1_ref[:, c * cw:(c + 1) * cw])
        b = _dot(h2, w1_ref[:, D_FF + c * cw:D_FF + (c + 1) * cw])
        t = (_silu(a) * b).astype(BF16)
        part = _dot(t, w2_ref[c * cw:(c + 1) * cw, :])
        y2 = part if y2 is None else y2 + part
    return _ln(ALPHA * x1 + g_f * y2) * ln2g + ln2b


def _strided_rows(ref, part, first, count):
    return ref[part, pl.ds(first, count, stride=SUBLANES), :]


def _ada_kernel(cv_ref, w_ref, b_ref, o_ref):
    s = _silu(cv_ref[...])
    o_ref[...] = jnp.dot(s, w_ref[...], precision=lax.Precision.HIGHEST,
                         preferred_element_type=F32) + b_ref[...]


def _inproj_kernel(x_ref, mod_ref, w_in_ref, rt_ref, ct_ref, bd_hi_ref, bd_lo_ref,
                   avg_ref, lng_ref, lnb_ref, wscat_ref, bs_ref,
                   q_ref, kd_ref, vd_ref, fx_ref, sg_ref):
    tm = x_ref.shape[0]
    sh, sc = _mod_rows(mod_ref[...], 0)[:2]
    h = (_ln(x_ref[...]) * (1.0 + sc) + sh).astype(BF16)
    tabs = _rope_tile(rt_ref, ct_ref, pl.program_id(0) * (tm // GRID_W), tm)
    for p in range(ATTN_W // LANES):
        qp = _rope(_project(h, w_in_ref, p * LANES, (p + 1) * LANES), tabs)
        q_ref[:, p * LANES:(p + 1) * LANES] = (qp * HEAD_DIM ** -0.5).astype(BF16)
    k0, k1 = _dup_halves(_rope(_project(h, w_in_ref, ATTN_W, ATTN_W + KV_W), tabs))
    kd_ref[:, :LANES] = k0.astype(BF16)
    kd_ref[:, LANES:] = k1.astype(BF16)
    v0, v1 = _dup_halves(_project(h, w_in_ref, ATTN_W + KV_W, ATTN_W + 2 * KV_W))
    vd_ref[:, :LANES] = v0.astype(BF16)
    vd_ref[:, LANES:] = v1.astype(BF16)
    f0 = ATTN_W + 2 * KV_W
    fr, fi = _channel_dft(_project(h, w_in_ref, f0, f0 + F_W), bd_hi_ref, bd_lo_ref)
    parts = (fr[:, :LANES], fr[:, LANES:], fi[:, :LANES], fi[:, LANES:])
    for al in range(tm // DFT_B):
        for bb in range(B_BLOCKS):
            r0 = al * DFT_B + bb * SUBLANES
            for p in range(4):
                fx_ref[p, bb, al * SUBLANES:(al + 1) * SUBLANES, :] = parts[p][r0:r0 + SUBLANES]
    u = _project(h, w_in_ref, f0 + F_W, f0 + F_W + C_W)
    g = _project(h, w_in_ref, f0 + F_W + C_W, D_IN)
    sg_ref[...] = _spatial_gate(u, g, avg_ref, lng_ref, lnb_ref, wscat_ref, bs_ref).astype(BF16)


def _attn_kernel(q_ref, kd_ref, vd_ref, kc_ref, vc_ref, sink_ref, o_ref, *, n_tokens):
    n = pl.program_id(0)
    span = 3 * BLOCK
    start = pl.multiple_of(jnp.clip((n - 1) * BLOCK, 0, n_tokens - span), BLOCK)
    kw = kd_ref[pl.ds(start, span), :]
    vw = vd_ref[pl.ds(start, span), :]
    n_ctx = kc_ref.shape[0]
    qpos = n * BLOCK + lax.broadcasted_iota(jnp.int32, (BLOCK, span + n_ctx), 0)
    col = lax.broadcasted_iota(jnp.int32, (BLOCK, span + n_ctx), 1)
    ok = (jnp.abs(start + col - qpos) <= WINDOW) | (col >= span)
    ok4 = jnp.concatenate([ok] * 4, axis=0)
    q = q_ref[...]
    for kh in range(KV_HEADS):
        kk = jnp.concatenate([kw[:, kh * LANES:(kh + 1) * LANES],
                              kc_ref[:, kh * LANES:(kh + 1) * LANES]], axis=0)
        vv = jnp.concatenate([vw[:, kh * LANES:(kh + 1) * LANES],
                              vc_ref[:, kh * LANES:(kh + 1) * LANES]], axis=0)
        q4, low = _stack_heads(q, kh)
        s = lax.dot_general(q4, kk, (((1,), (1,)), ((), ())), preferred_element_type=F32)
        s = jnp.where(ok4, s, NEG_BIG)
        o4 = _softmax_pv(s, _sink_col(sink_ref, kh, BLOCK), vv)
        pa, pb = _unstack_heads(o4, BLOCK, low)
        o_ref[:, kh * 256:kh * 256 + LANES] = pa.astype(BF16)
        o_ref[:, kh * 256 + LANES:kh * 256 + 2 * LANES] = pb.astype(BF16)


def _dft_stage1_kernel(x_ref, m_hi_ref, m_lo_ref, tc_ref, ts_ref, y_ref):
    a = m_hi_ref.shape[0] // 2
    for bl in range(SUBLANES):
        p = [_strided_rows(x_ref, i, bl, a) for i in range(4)]
        x = jnp.concatenate([jnp.concatenate(p[0:2], axis=1), jnp.concatenate(p[2:4], axis=1)], axis=0)
        x_hi, x_lo = _split(x)
        y = _dot3(m_hi_ref[...], m_lo_ref[...], x_hi, x_lo)
        yr, yi = y[:a], y[a:]
        tc, ts = tc_ref[:, bl:bl + 1], ts_ref[:, bl:bl + 1]
        wr = yr * tc + yi * ts
        wi = yi * tc - yr * ts
        parts = (wr[:, :LANES], wr[:, LANES:], wi[:, :LANES], wi[:, LANES:])
        for kb in range(a // SUBLANES):
            for i in range(4):
                y_ref[i, kb, bl * SUBLANES:(bl + 1) * SUBLANES, :] = (
                    parts[i][kb * SUBLANES:(kb + 1) * SUBLANES])


def _dft_stage2_kernel(y_ref, cs_hi_ref, cs_lo_ref, wf_ref, bf_ref, o_ref):
    zs = []
    for kl in range(SUBLANES):
        p = [_strided_rows(y_ref, i, kl, DFT_B) for i in range(4)]
        r = jnp.concatenate([jnp.concatenate(p[0:2], axis=1), jnp.concatenate(p[2:4], axis=1)], axis=0)
        r_hi, r_lo = _split(r)
        zs.append(_dot3(cs_hi_ref[...], cs_lo_ref[...], r_hi, r_lo))
    z = jnp.concatenate(zs, axis=0)
    out = _dot(z.astype(BF16), wf_ref[...]) + bf_ref[...]
    for kl in range(SUBLANES):
        for hf in range(F_W // LANES):
            o_ref[hf, pl.ds(kl, DFT_B, stride=SUBLANES), :] = (
                out[kl * DFT_B:(kl + 1) * DFT_B, hf * LANES:(hf + 1) * LANES])


def _outffn_kernel(x_ref, oa_ref, of_ref, sg_ref, mod_ref, w_out_ref, ln1g_ref, ln1b_ref,
                   w1_ref, w2_ref, ln2g_ref, ln2b_ref, o_ref, *, n_chunks):
    n_kb, k2_rows = of_ref.shape[1], of_ref.shape[2]
    halves = []
    for hf in range(F_W // LANES):
        pieces = [of_ref[hf, kb, k2 * SUBLANES:(k2 + 1) * SUBLANES, :]
                  for k2 in range(k2_rows // SUBLANES) for kb in range(n_kb)]
        halves.append(jnp.concatenate(pieces, axis=0).astype(BF16))
    o_cat = jnp.concatenate([oa_ref[...]] + halves + [sg_ref[...]], axis=1)
    o_ref[...] = _out_ffn(x_ref[...], o_cat, _mod_rows(mod_ref[...], 0), w_out_ref,
                          ln1g_ref[...], ln1b_ref[...], w1_ref, w2_ref,
                          ln2g_ref[...], ln2b_ref[...], n_chunks)


def _ctx_kernel(xc_ref, mod0_ref, mod1_ref, w_in0_ref, w_in1_ref, bd_hi_ref, bd_lo_ref, avg_ref,
                lng_ref, lnb_ref, wscat_ref, bs_ref, sink_ref, cs_hi_ref, cs_lo_ref, wf_ref, bf_ref,
                w_out_ref, ln1g_ref, ln1b_ref, w1_ref, w2_ref, ln2g_ref, ln2b_ref,
                kc0_ref, vc0_ref, kc1_ref, vc1_ref, *, n_chunks):
    xc = xc_ref[...]
    n_ctx = xc.shape[0]
    m0 = _mod_rows(mod0_ref[...], 1)
    h = (_ln(xc) * (1.0 + m0[1]) + m0[0]).astype(BF16)
    q = (_project(h, w_in0_ref, 0, ATTN_W) * HEAD_DIM ** -0.5).astype(BF16)
    k0, k1 = _dup_halves(_project(h, w_in0_ref, ATTN_W, ATTN_W + KV_W))
    v0, v1 = _dup_halves(_project(h, w_in0_ref, ATTN_W + KV_W, ATTN_W + 2 * KV_W))
    kd = [k0.astype(BF16), k1.astype(BF16)]
    vd = [v0.astype(BF16), v1.astype(BF16)]
    kc0_ref[:, :LANES], kc0_ref[:, LANES:] = kd[0], kd[1]
    vc0_ref[:, :LANES], vc0_ref[:, LANES:] = vd[0], vd[1]
    pairs = []
    for kh in range(KV_HEADS):
        q4, low = _stack_heads(q, kh)
        s = lax.dot_general(q4, kd[kh], (((1,), (1,)), ((), ())), preferred_element_type=F32)
        o4 = _softmax_pv(s, _sink_col(sink_ref, kh, n_ctx), vd[kh])
        pairs.extend(_unstack_heads(o4, n_ctx, low))
    f0 = ATTN_W + 2 * KV_W
    fr, fi = _channel_dft(_project(h, w_in0_ref, f0, f0 + F_W), bd_hi_ref, bd_lo_ref)
    y_hi, y_lo = _split(jnp.concatenate([fr, fi], axis=0))
    z = _dot3(cs_hi_ref[...], cs_lo_ref[...], y_hi, y_lo)
    o_f = _dot(z.astype(BF16), wf_ref[...]) + bf_ref[...]
    u = _project(h, w_in0_ref, f0 + F_W, f0 + F_W + C_W)
    g = _project(h, w_in0_ref, f0 + F_W + C_W, D_IN)
    sg = _spatial_gate(u, g, avg_ref, lng_ref, lnb_ref, wscat_ref, bs_ref)
    o_cat = jnp.concatenate([p.astype(BF16) for p in pairs] + [o_f.astype(BF16), sg.astype(BF16)],
                            axis=1)
    xc1 = _out_ffn(xc, o_cat, m0, w_out_ref, ln1g_ref[...], ln1b_ref[...], w1_ref, w2_ref,
                   ln2g_ref[...], ln2b_ref[...], n_chunks)
    m1 = _mod_rows(mod1_ref[...], 1)
    h1 = (_ln(xc1) * (1.0 + m1[1]) + m1[0]).astype(BF16)
    k0, k1 = _dup_halves(_project(h1, w_in1_ref, ATTN_W, ATTN_W + KV_W))
    v0, v1 = _dup_halves(_project(h1, w_in1_ref, ATTN_W + KV_W, ATTN_W + 2 * KV_W))
    kc1_ref[:, :LANES], kc1_ref[:, LANES:] = k0.astype(BF16), k1.astype(BF16)
    vc1_ref[:, :LANES], vc1_ref[:, LANES:] = v0.astype(BF16), v1.astype(BF16)


def _hi_lo(a):
    a32 = jnp.asarray(np.asarray(a, np.float64), F32)
    hi = a32.astype(BF16)
    return hi, (a32 - hi.astype(F32)).astype(BF16)


def _dft_cos_sin(n):
    idx = np.arange(n, dtype=np.int64)
    ang = 2.0 * np.pi * ((idx[:, None] * idx[None, :]) % n) / n
    return np.cos(ang), np.sin(ang)


def _channel_dft_tables():
    c, s = _dft_cos_sin(F_DIM)
    bd = np.zeros((F_W, 2 * F_W))
    for g in range(F_GROUPS):
        sl = slice(g * F_DIM, (g + 1) * F_DIM)
        bd[sl, sl] = c / math.sqrt(F_DIM)
        bd[sl, F_W + g * F_DIM:F_W + (g + 1) * F_DIM] = -s / math.sqrt(F_DIM)
    return _hi_lo(bd)


def _group_avg_table():
    a = np.zeros((C_W, C_W))
    for h in range(C_HEADS):
        a[h * C_DIM:(h + 1) * C_DIM, h * C_DIM:(h + 1) * C_DIM] = 1.0 / C_DIM
    return jnp.asarray(a, BF16)


def _rope_tables(n):
    freqs = jnp.asarray(ROPE_BASE, F32) ** (-jnp.arange(0, AXIS_DIM, 2, dtype=F32) / AXIS_DIM)
    reps = LANES // HEAD_DIM

    def tables(pos, row_axis):
        ang = pos[:, None] * freqs
        c, s = jnp.cos(ang), jnp.sin(ang)
        z = jnp.zeros_like(s)
        pad = [z, z]
        pick = (lambda t: t + pad) if row_axis else (lambda t: pad + t)
        return jnp.stack([jnp.tile(jnp.concatenate(pick(t), axis=1), (1, reps))
                          for t in ([c, c], [-s, z], [z, s])])

    rt = tables(jnp.arange(n // GRID_W, dtype=F32), True)
    ct = tables(jnp.arange(GRID_W, dtype=F32), False)
    return rt, ct


def _stage1_tables(a):
    n = a * DFT_B
    c, s = _dft_cos_sin(a)
    m_hi, m_lo = _hi_lo(np.block([[c, s], [-s, c]]) / math.sqrt(a))
    k1 = np.arange(a, dtype=np.int64)[None, :, None]
    b = (np.arange(B_BLOCKS, dtype=np.int64)[:, None, None] * SUBLANES
         + np.arange(SUBLANES, dtype=np.int64)[None, None, :])
    ang = 2.0 * np.pi * ((k1 * b) % n) / n
    return m_hi, m_lo, jnp.asarray(np.cos(ang), F32), jnp.asarray(np.sin(ang), F32)


def _stage2_tables():
    c, s = _dft_cos_sin(DFT_B)
    return _hi_lo(np.concatenate([c, s], axis=1) / math.sqrt(DFT_B))


def _ctx_dft_tables(n_ctx):
    c, s = _dft_cos_sin(n_ctx)
    return _hi_lo(np.concatenate([c, s], axis=1) / math.sqrt(n_ctx))


def _const_spec(shape):
    nd = len(shape)
    return pl.BlockSpec(shape, lambda *_: (0,) * nd, pipeline_mode=pl.Buffered(1))


def _layer_spec(shape, layer):
    nd = len(shape) - 1
    return pl.BlockSpec((None,) + tuple(shape[1:]), lambda *_: (layer,) + (0,) * nd,
                        pipeline_mode=pl.Buffered(1))


def _params(*sem):
    return pltpu.CompilerParams(dimension_semantics=sem, vmem_limit_bytes=VMEM_LIMIT)


def _ada(cv, w_ada, b_ada):
    tn = 1536
    return pl.pallas_call(
        _ada_kernel,
        grid=(DEPTH, 6 * D_MODEL // tn),
        in_specs=[pl.BlockSpec((8, D_MODEL), lambda l, j: (0, 0)),
                  pl.BlockSpec((None, D_MODEL, tn), lambda l, j: (l, 0, j)),
                  pl.BlockSpec((None, 1, tn), lambda l, j: (l, 0, j))],
        out_specs=pl.BlockSpec((None, 8, tn), lambda l, j: (l, 0, j)),
        out_shape=jax.ShapeDtypeStruct((DEPTH, 8, 6 * D_MODEL), F32),
        compiler_params=_params("arbitrary", "arbitrary"),
        name="ada",
    )(cv, w_ada, b_ada.reshape(DEPTH, 1, 6 * D_MODEL))


def _inproj(x, mods, w_in, layer, rope, shared, sgu, tm):
    n = x.shape[0]
    a = n // DFT_B
    row = lambda w: pl.BlockSpec((tm, w), lambda i: (i, 0))
    consts = list(rope) + list(shared)
    fx_rows = tm // DFT_B * SUBLANES
    return pl.pallas_call(
        _inproj_kernel,
        grid=(n // tm,),
        in_specs=[row(D_MODEL), _layer_spec(mods.shape, layer), _layer_spec(w_in.shape, layer)]
        + [_const_spec(c.shape) for c in consts] + [_layer_spec(c.shape, layer) for c in sgu],
        out_specs=[row(ATTN_W), row(2 * LANES), row(2 * LANES),
                   pl.BlockSpec((4, B_BLOCKS, fx_rows, LANES), lambda i: (0, 0, i, 0)), row(C_W)],
        out_shape=[jax.ShapeDtypeStruct((n, ATTN_W), BF16), jax.ShapeDtypeStruct((n, 2 * LANES), BF16),
                   jax.ShapeDtypeStruct((n, 2 * LANES), BF16),
                   jax.ShapeDtypeStruct((4, B_BLOCKS, a * SUBLANES, LANES), F32),
                   jax.ShapeDtypeStruct((n, C_W), BF16)],
        compiler_params=_params("arbitrary"),
        name="inproj",
    )(x, mods, w_in, *consts, *sgu)


def _attention(q, kd, vd, kc, vc, sink_tab, layer):
    n = q.shape[0]
    return pl.pallas_call(
        functools.partial(_attn_kernel, n_tokens=n),
        grid=(n // BLOCK,),
        in_specs=[pl.BlockSpec((BLOCK, ATTN_W), lambda i: (i, 0)), _const_spec(kd.shape),
                  _const_spec(vd.shape), _const_spec(kc.shape), _const_spec(vc.shape),
                  _layer_spec(sink_tab.shape, layer)],
        out_specs=pl.BlockSpec((BLOCK, ATTN_W), lambda i: (i, 0)),
        out_shape=jax.ShapeDtypeStruct((n, ATTN_W), BF16),
        compiler_params=_params("arbitrary"),
        name="attention",
    )(q, kd, vd, kc, vc, sink_tab)


def _fourier(fx, tabs1, tabs2, wf_bd, bf_row, layer):
    a = fx.shape[2] // SUBLANES
    kb = a // SUBLANES
    m_hi, m_lo, tc, ts = tabs1
    tw = pl.BlockSpec((None, a, SUBLANES), lambda j: (j, 0, 0))
    yarr = pl.pallas_call(
        _dft_stage1_kernel,
        grid=(B_BLOCKS,),
        in_specs=[pl.BlockSpec((4, None, a * SUBLANES, LANES), lambda j: (0, j, 0, 0)),
                  _const_spec(m_hi.shape), _const_spec(m_lo.shape), tw, tw],
        out_specs=pl.BlockSpec((4, kb, SUBLANES * SUBLANES, LANES), lambda j: (0, 0, j, 0)),
        out_shape=jax.ShapeDtypeStruct((4, kb, DFT_B * SUBLANES, LANES), F32),
        compiler_params=_params("arbitrary"),
        name="dft_stage1",
    )(fx, m_hi, m_lo, tc, ts)
    cs_hi, cs_lo = tabs2
    return pl.pallas_call(
        _dft_stage2_kernel,
        grid=(kb,),
        in_specs=[pl.BlockSpec((4, None, DFT_B * SUBLANES, LANES), lambda i: (0, i, 0, 0)),
                  _const_spec(cs_hi.shape), _const_spec(cs_lo.shape),
                  _layer_spec(wf_bd.shape, layer), _layer_spec(bf_row.shape, layer)],
        out_specs=pl.BlockSpec((F_W // LANES, None, DFT_B * SUBLANES, LANES), lambda i: (0, i, 0, 0)),
        out_shape=jax.ShapeDtypeStruct((F_W // LANES, kb, DFT_B * SUBLANES, LANES), F32),
        compiler_params=_params("arbitrary"),
        name="dft_stage2",
    )(yarr, cs_hi, cs_lo, wf_bd, bf_row)


def _outffn(x, oa, of, sg, mods, weights, layer, tm, n_chunks):
    n = x.shape[0]
    a = n // DFT_B
    row = lambda w: pl.BlockSpec((tm, w), lambda i: (i, 0))
    of_spec = pl.BlockSpec((F_W // LANES, a // SUBLANES, tm // a * SUBLANES, LANES),
                           lambda i: (0, 0, i, 0))
    consts = [mods] + list(weights)
    return pl.pallas_call(
        functools.partial(_outffn_kernel, n_chunks=n_chunks),
        grid=(n // tm,),
        in_specs=[row(D_MODEL), row(ATTN_W), of_spec, row(C_W)]
        + [_layer_spec(c.shape, layer) for c in consts],
        out_specs=row(D_MODEL),
        out_shape=jax.ShapeDtypeStruct((n, D_MODEL), F32),
        compiler_params=_params("arbitrary"),
        name="outffn",
    )(x, oa, of, sg, *consts)


def _context(xc, mods, w_in, shared, sgu, sink_tab, ctx_tabs, wf_bd, bf_row, weights, n_chunks):
    n_ctx = xc.shape[0]
    args = ([(xc, None), (mods, 0), (mods, 1), (w_in, 0), (w_in, 1)] + [(t, None) for t in shared]
            + [(t, 0) for t in sgu] + [(sink_tab, 0)] + [(t, None) for t in ctx_tabs]
            + [(wf_bd, 0), (bf_row, 0)] + [(t, 0) for t in weights])
    kv = jax.ShapeDtypeStruct((n_ctx, 2 * LANES), BF16)
    return pl.pallas_call(
        functools.partial(_ctx_kernel, n_chunks=n_chunks),
        grid=(1,),
        in_specs=[_const_spec(t.shape) if l is None else _layer_spec(t.shape, l) for t, l in args],
        out_specs=[pl.BlockSpec(kv.shape, lambda i: (0, 0))] * 4,
        out_shape=[kv, kv, kv, kv],
        compiler_params=_params("arbitrary"),
        name="context",
    )(*[t for t, _ in args])


def _block_diag_wf(w_f):
    bd = jnp.zeros((w_f.shape[0], F_W, F_W), F32)
    for g in range(F_GROUPS):
        bd = bd.at[:, g * F_DIM:(g + 1) * F_DIM, g * F_DIM:(g + 1) * F_DIM].set(w_f[:, g])
    return bd.astype(BF16)


def _forward(x, c, ctx, c_ctx, w_ada, b_ada, w_in, w_out, attn_sink, w_fourier, b_fourier,
             sgu_ln_g, sgu_ln_b, w_spatial, b_spatial, ln1_g, ln1_b, w_ffn_in, w_ffn_out,
             ln2_g, ln2_b, tm_in=256, tm_out=256, n_chunks=2):
    n = x.shape[1]
    n_ctx = ctx.shape[1]
    depth = w_in.shape[0]
    xs = x[0]
    cv = jnp.zeros((8, D_MODEL), F32).at[0].set(c[0]).at[1].set(c_ctx)
    mods = _ada(cv, w_ada, b_ada)

    w_in_b = w_in.astype(BF16)
    rows = lambda v: v.reshape(depth, 1, -1)
    weights = (w_out.astype(BF16), rows(ln1_g), rows(ln1_b), w_ffn_in.astype(BF16),
               w_ffn_out.astype(BF16), rows(ln2_g), rows(ln2_b))
    shared = _channel_dft_tables() + (_group_avg_table(),)
    rope = _rope_tables(n)
    tabs1 = _stage1_tables(n // DFT_B)
    tabs2 = _stage2_tables()
    ctx_tabs = _ctx_dft_tables(n_ctx)
    sgu = (sgu_ln_g.reshape(depth, 1, C_W), sgu_ln_b.reshape(depth, 1, C_W),
           jnp.concatenate([w_spatial[:, h] for h in range(C_HEADS)], axis=2).astype(BF16),
           jnp.repeat(jnp.swapaxes(b_spatial, 1, 2), C_DIM, axis=2))
    sink_tab = jnp.broadcast_to(attn_sink[:, :, None], (depth, ATTN_HEADS, LANES))
    wf_bd = _block_diag_wf(w_fourier)
    bf_row = b_fourier.reshape(depth, 1, F_W)

    ctx_kv = _context(ctx[0], mods, w_in_b, shared, sgu, sink_tab, ctx_tabs, wf_bd, bf_row, weights,
                      n_chunks)

    for l in range(depth):
        q, kd, vd, fx, sg = _inproj(xs, mods, w_in_b, l, rope, shared, sgu, tm_in)
        oa = _attention(q, kd, vd, ctx_kv[2 * l], ctx_kv[2 * l + 1], sink_tab, l)
        of = _fourier(fx, tabs1, tabs2, wf_bd, bf_row, l)
        xs = _outffn(xs, oa, of, sg, mods, weights, l, tm_out, n_chunks)
    return xs[None]


def kernel(x, c, ctx, c_ctx, w_ada, b_ada, w_in, w_out, attn_sink, w_fourier, b_fourier, sgu_ln_g,
           sgu_ln_b, w_spatial, b_spatial, ln1_g, ln1_b, w_ffn_in, w_ffn_out, ln2_g, ln2_b):
    return _forward(x, c, ctx, c_ctx, w_ada, b_ada, w_in, w_out, attn_sink, w_fourier, b_fourier,
                    sgu_ln_g, sgu_ln_b, w_spatial, b_spatial, ln1_g, ln1_b, w_ffn_in, w_ffn_out,
                    ln2_g, ln2_b)
```

```python
import functools
import math

import numpy as np
import jax
import jax.numpy as jnp
from jax import lax
from jax.experimental import pallas as pl
from jax.experimental.pallas import tpu as pltpu

F32 = jnp.float32
BF16 = jnp.bfloat16

D_MODEL = 1024
DEPTH = 2
GRID_W = 64
HEAD_DIM = 64
ATTN_HEADS = 8
KV_HEADS = 2
WINDOW = 128
BLOCK = 128
ROPE_BASE = 10000.0
AXIS_DIM = HEAD_DIM // 2
F_GROUPS = 4
F_DIM = 64
C_HEADS = 4
C_DIM = 64
CHUNK = 128
D_FF = 2816
ATTN_W = ATTN_HEADS * HEAD_DIM
KV_W = KV_HEADS * HEAD_DIM
F_W = F_GROUPS * F_DIM
C_W = C_HEADS * C_DIM
D_IN = ATTN_W + 2 * KV_W + F_W + 2 * C_W
COL_K = ATTN_W
COL_V = COL_K + KV_W
COL_F = COL_V + KV_W
COL_U = COL_F + 2 * F_W
COL_G = COL_U + C_W
D_PROJ = COL_G + C_W
ALPHA = (2 * DEPTH) ** 0.25
LN_EPS = 1e-6
NEG_BIG = -1e30

LANES = 128
SUBLANES = 8
MXU_TILE = 256
FFN_CHUNKS = (0, 4 * MXU_TILE, 8 * MXU_TILE, D_FF)
DFT_B = 128
B_BLOCKS = DFT_B // SUBLANES
VMEM_LIMIT = 56 * 1024 * 1024


def _dot(a, b):
    return jnp.dot(a, b, preferred_element_type=F32)


def _split(a):
    hi = a.astype(BF16)
    lo = (a - hi.astype(F32)).astype(BF16)
    return hi, lo


def _dot3(a_hi, a_lo, b_hi, b_lo):
    return _dot(a_hi, b_hi) + _dot(a_lo, b_hi) + _dot(a_hi, b_lo)


def _ln(x):
    mu = jnp.mean(x, axis=-1, keepdims=True)
    xc = x - mu
    var = jnp.mean(xc * xc, axis=-1, keepdims=True)
    return xc * lax.rsqrt(var + LN_EPS)


def _gelu(x):
    return 0.5 * x * (1.0 + jnp.tanh(math.sqrt(2.0 / math.pi) * (x + 0.044715 * (x * x * x))))


def _silu(x):
    return x / (1.0 + jnp.exp(-x))


def _mod_rows(mod, row):
    return [mod[row:row + 1, i * D_MODEL:(i + 1) * D_MODEL] for i in range(6)]


def _rope(t, tabs):
    cos, sin_a, sin_b = tabs
    return (t * cos + pltpu.roll(t, LANES - AXIS_DIM // 2, 1) * sin_a
            + pltpu.roll(t, AXIS_DIM // 2, 1) * sin_b)


def _rope_tile(rt_ref, ct_ref, first_grid_row, rows):
    tabs = []
    for t in range(3):
        groups = [rt_ref[t, pl.ds(first_grid_row + r, 1), :] + ct_ref[t]
                  for r in range(rows // GRID_W)]
        tabs.append(jnp.concatenate(groups, axis=0) if len(groups) > 1 else groups[0])
    return tabs


def _project(h, w_in_ref, lo, hi):
    return _dot(h, w_in_ref[:, lo:hi])


def _dup_halves(t):
    lane = lax.broadcasted_iota(jnp.int32, (1, LANES), 1)
    low = lane < HEAD_DIM
    sw = pltpu.roll(t, HEAD_DIM, 1)
    return jnp.where(low, t, sw), jnp.where(low, sw, t)


def _spatial_gate_stages(u, g, avg_ref, lng_ref, lnb_ref, wscat_ref, bs_ref):
    rows = u.shape[0]
    ug = _gelu(u)
    vg = _gelu(g)
    avg = avg_ref[...]
    v_hi, v_lo = _split(vg)
    mu = _dot(v_hi, avg) + _dot(v_lo, avg)
    yield None
    vc = vg - mu
    c_hi, c_lo = _split(vc * vc)
    var = _dot(c_hi, avg) + _dot(c_lo, avg)
    yield None
    vn = vc * lax.rsqrt(var + LN_EPS) * lng_ref[...] + lnb_ref[...]
    lane = lax.broadcasted_iota(jnp.int32, (1, C_W), 1)
    vnb = vn.astype(BF16)
    zero = jnp.zeros_like(vnb)
    outs = []
    for c in range(rows // CHUNK):
        blk = vnb[c * CHUNK:(c + 1) * CHUNK]
        rhs = jnp.concatenate(
            [jnp.where((lane >= h * C_DIM) & (lane < (h + 1) * C_DIM), blk, zero[:CHUNK])
             for h in range(C_HEADS)], axis=0)
        vs = _dot(wscat_ref[...], rhs) + bs_ref[...]
        outs.append(ug[c * CHUNK:(c + 1) * CHUNK] * vs)
    yield jnp.concatenate(outs, axis=0) if len(outs) > 1 else outs[0]


def _spatial_gate(*args):
    return list(_spatial_gate_stages(*args))[-1]


def _softmax_pv(s, sink_col, vv):
    m = jnp.maximum(jnp.max(s, axis=1, keepdims=True), sink_col)
    e = jnp.exp(s - m)
    denom = jnp.sum(e, axis=1, keepdims=True) + jnp.exp(sink_col - m)
    return _dot(e.astype(BF16), vv) / denom


def _stack_heads(q, kh):
    lane = lax.broadcasted_iota(jnp.int32, (1, LANES), 1)
    low = lane < HEAD_DIM
    p0 = q[:, kh * 256:kh * 256 + LANES]
    p1 = q[:, kh * 256 + LANES:kh * 256 + 2 * LANES]
    z = jnp.zeros_like(p0)
    return jnp.concatenate([jnp.where(low, p0, z), jnp.where(low, z, p0),
                            jnp.where(low, p1, z), jnp.where(low, z, p1)], axis=0), low


def _unstack_heads(o4, rows, low):
    return (jnp.where(low, o4[0:rows], o4[rows:2 * rows]),
            jnp.where(low, o4[2 * rows:3 * rows], o4[3 * rows:4 * rows]))


def _sink_col(sink_ref, kh, rows):
    return jnp.concatenate(
        [jnp.broadcast_to(sink_ref[4 * kh + g:4 * kh + g + 1, 0:1], (rows, 1)) for g in range(4)],
        axis=0)


def _out_ffn_stages(x, o_cat, mod_row, w_out_ref, ln1g, ln1b, w1_ref, w2_ref, ln2g, ln2b):
    g_m, sh_f, sc_f, g_f = mod_row[2], mod_row[3], mod_row[4], mod_row[5]
    y = _dot(o_cat, w_out_ref[...])
    yield None
    x1 = _ln(ALPHA * x + g_m * y) * ln1g + ln1b
    h2 = (_ln(x1) * (1.0 + sc_f) + sh_f).astype(BF16)
    yield None
    y2 = None
    for lo, hi in zip(FFN_CHUNKS[:-1], FFN_CHUNKS[1:]):
        a = _dot(h2, w1_ref[:, lo:hi])
        b = _dot(h2, w1_ref[:, D_FF + lo:D_FF + hi])
        t = (_silu(a) * b).astype(BF16)
        part = _dot(t, w2_ref[lo:hi, :])
        y2 = part if y2 is None else y2 + part
        yield None
    yield _ln(ALPHA * x1 + g_f * y2) * ln2g + ln2b


OUT_FFN_STAGES = len(FFN_CHUNKS) + 2


def _run_skewed(gens, n_stages, skew):
    results = [None] * len(gens)
    for t in range(n_stages + skew * (len(gens) - 1)):
        for r, gen in enumerate(gens):
            if 0 <= t - r * skew < n_stages:
                results[r] = next(gen)
    return results


def _out_ffn(*args):
    return _run_skewed([_out_ffn_stages(*args)], OUT_FFN_STAGES, 0)[0]


def _strided_rows(ref, part, first, count):
    return ref[part, pl.ds(first, count, stride=SUBLANES), :]


def _ada_kernel(cv_ref, w_ref, b_ref, o_ref):
    s = _silu(cv_ref[...])
    o_ref[...] = jnp.dot(s, w_ref[...], precision=lax.Precision.HIGHEST,
                         preferred_element_type=F32) + b_ref[...]


def _wprep_kernel(w_in_ref, bdc_ref, wf_ref, o_ref):
    hp = functools.partial(jnp.dot, precision=lax.Precision.HIGHEST, preferred_element_type=F32)
    wf = wf_ref[...]
    fold = jnp.concatenate([hp(bdc_ref[:, :F_W], wf), hp(bdc_ref[:, F_W:], wf)], axis=1)
    o_ref[:, :COL_F] = w_in_ref[:, :COL_F].astype(BF16)
    o_ref[:, COL_F:COL_U] = hp(w_in_ref[:, COL_F:COL_F + F_W], fold).astype(BF16)
    o_ref[:, COL_U:] = w_in_ref[:, COL_F + F_W:].astype(BF16)


def _inproj_stages(x, sh, sc, w_ref, tabs, avg_ref, lng_ref, lnb_ref, wscat_ref, bs_ref,
                   q_ref, kd_ref, vd_ref, fx_ref, sg_ref, row0):
    rows = x.shape[0]
    rs = slice(row0, row0 + rows)
    h = (_ln(x) * (1.0 + sc) + sh).astype(BF16)
    yield
    for p in range(ATTN_W // LANES):
        qp = _rope(_project(h, w_ref, p * LANES, (p + 1) * LANES), tabs)
        q_ref[rs, p * LANES:(p + 1) * LANES] = (qp * HEAD_DIM ** -0.5).astype(BF16)
    k0, k1 = _dup_halves(_rope(_project(h, w_ref, COL_K, COL_V), tabs))
    kd_ref[rs, :LANES] = k0.astype(BF16)
    kd_ref[rs, LANES:] = k1.astype(BF16)
    v0, v1 = _dup_halves(_project(h, w_ref, COL_V, COL_F))
    vd_ref[rs, :LANES] = v0.astype(BF16)
    vd_ref[rs, LANES:] = v1.astype(BF16)
    yield
    f = _project(h, w_ref, COL_F, COL_U)
    for al in range(rows // DFT_B):
        for bb in range(B_BLOCKS):
            r0 = al * DFT_B + bb * SUBLANES
            a0 = (row0 // DFT_B + al) * SUBLANES
            for p in range(4):
                fx_ref[p, bb, a0:a0 + SUBLANES, :] = f[r0:r0 + SUBLANES, p * LANES:(p + 1) * LANES]
    u = _project(h, w_ref, COL_U, COL_G)
    g = _project(h, w_ref, COL_G, D_PROJ)
    gate = _spatial_gate_stages(u, g, avg_ref, lng_ref, lnb_ref, wscat_ref, bs_ref)
    next(gate)
    yield
    next(gate)
    yield
    sg_ref[rs, :] = next(gate).astype(BF16)
    yield


INPROJ_STAGES = 5


def _inproj_kernel(x_ref, mod_ref, w_ref, rt_ref, ct_ref, avg_ref, lng_ref, lnb_ref, wscat_ref,
                   bs_ref, q_ref, kd_ref, vd_ref, fx_ref, sg_ref, *, n_sub):
    tm = x_ref.shape[0]
    sub = tm // n_sub
    sh, sc = _mod_rows(mod_ref[...], 0)[:2]
    gens = []
    for r in range(n_sub):
        grid_row = pl.program_id(0) * (tm // GRID_W) + r * (sub // GRID_W)
        gens.append(_inproj_stages(
            x_ref[r * sub:(r + 1) * sub, :], sh, sc, w_ref, _rope_tile(rt_ref, ct_ref, grid_row, sub),
            avg_ref, lng_ref, lnb_ref, wscat_ref, bs_ref, q_ref, kd_ref, vd_ref, fx_ref, sg_ref,
            r * sub))
    _run_skewed(gens, INPROJ_STAGES, 1)


def _attn_kernel(q_ref, kd_ref, vd_ref, kc_ref, vc_ref, sink_ref, o_ref, *, n_tokens):
    n = pl.program_id(0)
    span = 3 * BLOCK
    start = pl.multiple_of(jnp.clip((n - 1) * BLOCK, 0, n_tokens - span), BLOCK)
    kw = kd_ref[pl.ds(start, span), :]
    vw = vd_ref[pl.ds(start, span), :]
    n_ctx = kc_ref.shape[0]
    qpos = n * BLOCK + lax.broadcasted_iota(jnp.int32, (BLOCK, span + n_ctx), 0)
    col = lax.broadcasted_iota(jnp.int32, (BLOCK, span + n_ctx), 1)
    ok = (jnp.abs(start + col - qpos) <= WINDOW) | (col >= span)
    ok4 = jnp.concatenate([ok] * 4, axis=0)
    q = q_ref[...]
    for kh in range(KV_HEADS):
        kk = jnp.concatenate([kw[:, kh * LANES:(kh + 1) * LANES],
                              kc_ref[:, kh * LANES:(kh + 1) * LANES]], axis=0)
        vv = jnp.concatenate([vw[:, kh * LANES:(kh + 1) * LANES],
                              vc_ref[:, kh * LANES:(kh + 1) * LANES]], axis=0)
        q4, low = _stack_heads(q, kh)
        s = lax.dot_general(q4, kk, (((1,), (1,)), ((), ())), preferred_element_type=F32)
        s = jnp.where(ok4, s, NEG_BIG)
        o4 = _softmax_pv(s, _sink_col(sink_ref, kh, BLOCK), vv)
        pa, pb = _unstack_heads(o4, BLOCK, low)
        o_ref[:, kh * 256:kh * 256 + LANES] = pa.astype(BF16)
        o_ref[:, kh * 256 + LANES:kh * 256 + 2 * LANES] = pb.astype(BF16)


def _dft_stage1_kernel(x_ref, m_hi_ref, m_lo_ref, tc_ref, ts_ref, y_ref):
    a = m_hi_ref.shape[0] // 2
    for bl in range(SUBLANES):
        p = [_strided_rows(x_ref, i, bl, a) for i in range(4)]
        x = jnp.concatenate([jnp.concatenate(p[0:2], axis=1), jnp.concatenate(p[2:4], axis=1)], axis=0)
        x_hi, x_lo = _split(x)
        y = _dot3(m_hi_ref[...], m_lo_ref[...], x_hi, x_lo)
        yr, yi = y[:a], y[a:]
        tc, ts = tc_ref[:, bl:bl + 1], ts_ref[:, bl:bl + 1]
        wr = yr * tc + yi * ts
        wi = yi * tc - yr * ts
        parts = (wr[:, :LANES], wr[:, LANES:], wi[:, :LANES], wi[:, LANES:])
        for kb in range(a // SUBLANES):
            for i in range(4):
                y_ref[i, kb, bl * SUBLANES:(bl + 1) * SUBLANES, :] = (
                    parts[i][kb * SUBLANES:(kb + 1) * SUBLANES])


def _dft_stage2_kernel(y_ref, cs_hi_ref, cs_lo_ref, bf_ref, o_ref):
    for kl in range(SUBLANES):
        p = [_strided_rows(y_ref, i, kl, DFT_B) for i in range(4)]
        r = jnp.concatenate([jnp.concatenate(p[0:2], axis=1), jnp.concatenate(p[2:4], axis=1)], axis=0)
        r_hi, r_lo = _split(r)
        out = _dot3(cs_hi_ref[...], cs_lo_ref[...], r_hi, r_lo) + bf_ref[...]
        for hf in range(F_W // LANES):
            o_ref[hf, pl.ds(kl, DFT_B, stride=SUBLANES), :] = out[:, hf * LANES:(hf + 1) * LANES]


def _outffn_kernel(x_ref, oa_ref, of_ref, sg_ref, mod_ref, w_out_ref, ln1g_ref, ln1b_ref,
                   w1_ref, w2_ref, ln2g_ref, ln2b_ref, o_ref, *, n_sub):
    n_kb, k2_rows = of_ref.shape[1], of_ref.shape[2]
    halves = []
    for hf in range(F_W // LANES):
        pieces = [of_ref[hf, kb, k2 * SUBLANES:(k2 + 1) * SUBLANES, :]
                  for k2 in range(k2_rows // SUBLANES) for kb in range(n_kb)]
        halves.append(jnp.concatenate(pieces, axis=0).astype(BF16))
    o_cat = jnp.concatenate([oa_ref[...]] + halves + [sg_ref[...]], axis=1)
    tm = x_ref.shape[0]
    sub = tm // n_sub
    mod_row = _mod_rows(mod_ref[...], 0)
    gens = [_out_ffn_stages(x_ref[r * sub:(r + 1) * sub, :], o_cat[r * sub:(r + 1) * sub], mod_row,
                            w_out_ref, ln1g_ref[...], ln1b_ref[...], w1_ref, w2_ref,
                            ln2g_ref[...], ln2b_ref[...]) for r in range(n_sub)]
    for r, res in enumerate(_run_skewed(gens, OUT_FFN_STAGES, 1)):
        o_ref[r * sub:(r + 1) * sub, :] = res


def _ctx_kernel(xc_ref, mod0_ref, mod1_ref, w_in0_ref, w_in1_ref, avg_ref,
                lng_ref, lnb_ref, wscat_ref, bs_ref, sink_ref, cs_hi_ref, cs_lo_ref, bf_ref,
                w_out_ref, ln1g_ref, ln1b_ref, w1_ref, w2_ref, ln2g_ref, ln2b_ref,
                kc0_ref, vc0_ref, kc1_ref, vc1_ref):
    xc = xc_ref[...]
    n_ctx = xc.shape[0]
    m0 = _mod_rows(mod0_ref[...], 1)
    h = (_ln(xc) * (1.0 + m0[1]) + m0[0]).astype(BF16)
    q = (_project(h, w_in0_ref, 0, COL_K) * HEAD_DIM ** -0.5).astype(BF16)
    k0, k1 = _dup_halves(_project(h, w_in0_ref, COL_K, COL_V))
    v0, v1 = _dup_halves(_project(h, w_in0_ref, COL_V, COL_F))
    kd = [k0.astype(BF16), k1.astype(BF16)]
    vd = [v0.astype(BF16), v1.astype(BF16)]
    kc0_ref[:, :LANES], kc0_ref[:, LANES:] = kd[0], kd[1]
    vc0_ref[:, :LANES], vc0_ref[:, LANES:] = vd[0], vd[1]
    pairs = []
    for kh in range(KV_HEADS):
        q4, low = _stack_heads(q, kh)
        s = lax.dot_general(q4, kd[kh], (((1,), (1,)), ((), ())), preferred_element_type=F32)
        o4 = _softmax_pv(s, _sink_col(sink_ref, kh, n_ctx), vd[kh])
        pairs.extend(_unstack_heads(o4, n_ctx, low))
    f = _project(h, w_in0_ref, COL_F, COL_U)
    y_hi, y_lo = _split(jnp.concatenate([f[:, :F_W], f[:, F_W:]], axis=0))
    o_f = _dot3(cs_hi_ref[...], cs_lo_ref[...], y_hi, y_lo) + bf_ref[...]
    u = _project(h, w_in0_ref, COL_U, COL_G)
    g = _project(h, w_in0_ref, COL_G, D_PROJ)
    sg = _spatial_gate(u, g, avg_ref, lng_ref, lnb_ref, wscat_ref, bs_ref)
    o_cat = jnp.concatenate([p.astype(BF16) for p in pairs] + [o_f.astype(BF16), sg.astype(BF16)],
                            axis=1)
    xc1 = _out_ffn(xc, o_cat, m0, w_out_ref, ln1g_ref[...], ln1b_ref[...], w1_ref, w2_ref,
                   ln2g_ref[...], ln2b_ref[...])
    m1 = _mod_rows(mod1_ref[...], 1)
    h1 = (_ln(xc1) * (1.0 + m1[1]) + m1[0]).astype(BF16)
    k0, k1 = _dup_halves(_project(h1, w_in1_ref, COL_K, COL_V))
    v0, v1 = _dup_halves(_project(h1, w_in1_ref, COL_V, COL_F))
    kc1_ref[:, :LANES], kc1_ref[:, LANES:] = k0.astype(BF16), k1.astype(BF16)
    vc1_ref[:, :LANES], vc1_ref[:, LANES:] = v0.astype(BF16), v1.astype(BF16)


def _hi_lo(a):
    a32 = jnp.asarray(np.asarray(a, np.float64), F32)
    hi = a32.astype(BF16)
    return hi, (a32 - hi.astype(F32)).astype(BF16)


def _dft_cos_sin(n):
    idx = np.arange(n, dtype=np.int64)
    ang = 2.0 * np.pi * ((idx[:, None] * idx[None, :]) % n) / n
    return np.cos(ang), np.sin(ang)


def _channel_dft_tables():
    c, s = _dft_cos_sin(F_DIM)
    bd = np.zeros((F_W, 2 * F_W))
    for g in range(F_GROUPS):
        sl = slice(g * F_DIM, (g + 1) * F_DIM)
        bd[sl, sl] = c / math.sqrt(F_DIM)
        bd[sl, F_W + g * F_DIM:F_W + (g + 1) * F_DIM] = -s / math.sqrt(F_DIM)
    return jnp.asarray(bd, F32)


def _group_avg_table():
    a = np.zeros((C_W, C_W))
    for h in range(C_HEADS):
        a[h * C_DIM:(h + 1) * C_DIM, h * C_DIM:(h + 1) * C_DIM] = 1.0 / C_DIM
    return jnp.asarray(a, BF16)


def _rope_tables(n):
    freqs = jnp.asarray(ROPE_BASE, F32) ** (-jnp.arange(0, AXIS_DIM, 2, dtype=F32) / AXIS_DIM)
    reps = LANES // HEAD_DIM

    def tables(pos, row_axis):
        ang = pos[:, None] * freqs
        c, s = jnp.cos(ang), jnp.sin(ang)
        z = jnp.zeros_like(s)
        pad = [z, z]
        pick = (lambda t: t + pad) if row_axis else (lambda t: pad + t)
        return jnp.stack([jnp.tile(jnp.concatenate(pick(t), axis=1), (1, reps))
                          for t in ([c, c], [-s, z], [z, s])])

    rt = tables(jnp.arange(n // GRID_W, dtype=F32), True)
    ct = tables(jnp.arange(GRID_W, dtype=F32), False)
    return rt, ct


def _stage1_tables(a):
    n = a * DFT_B
    c, s = _dft_cos_sin(a)
    m_hi, m_lo = _hi_lo(np.block([[c, s], [-s, c]]) / math.sqrt(a))
    k1 = np.arange(a, dtype=np.int64)[None, :, None]
    b = (np.arange(B_BLOCKS, dtype=np.int64)[:, None, None] * SUBLANES
         + np.arange(SUBLANES, dtype=np.int64)[None, None, :])
    ang = 2.0 * np.pi * ((k1 * b) % n) / n
    return m_hi, m_lo, jnp.asarray(np.cos(ang), F32), jnp.asarray(np.sin(ang), F32)


def _stage2_tables():
    c, s = _dft_cos_sin(DFT_B)
    return _hi_lo(np.concatenate([c, s], axis=1) / math.sqrt(DFT_B))


def _ctx_dft_tables(n_ctx):
    c, s = _dft_cos_sin(n_ctx)
    return _hi_lo(np.concatenate([c, s], axis=1) / math.sqrt(n_ctx))


def _const_spec(shape):
    nd = len(shape)
    return pl.BlockSpec(shape, lambda *_: (0,) * nd, pipeline_mode=pl.Buffered(1))


def _layer_spec(shape, layer):
    nd = len(shape) - 1
    return pl.BlockSpec((None,) + tuple(shape[1:]), lambda *_: (layer,) + (0,) * nd,
                        pipeline_mode=pl.Buffered(1))


def _params(*sem):
    return pltpu.CompilerParams(dimension_semantics=sem, vmem_limit_bytes=VMEM_LIMIT)


def _ada(cv, w_ada, b_ada):
    tn = 1536
    return pl.pallas_call(
        _ada_kernel,
        grid=(DEPTH, 6 * D_MODEL // tn),
        in_specs=[pl.BlockSpec((8, D_MODEL), lambda l, j: (0, 0)),
                  pl.BlockSpec((None, D_MODEL, tn), lambda l, j: (l, 0, j)),
                  pl.BlockSpec((None, 1, tn), lambda l, j: (l, 0, j))],
        out_specs=pl.BlockSpec((None, 8, tn), lambda l, j: (l, 0, j)),
        out_shape=jax.ShapeDtypeStruct((DEPTH, 8, 6 * D_MODEL), F32),
        compiler_params=_params("arbitrary", "arbitrary"),
        name="ada",
    )(cv, w_ada, b_ada.reshape(DEPTH, 1, 6 * D_MODEL))


def _wprep(w_in, bdc, wf_bd):
    depth = w_in.shape[0]
    layer = lambda shape: pl.BlockSpec((None,) + tuple(shape[1:]), lambda l: (l, 0, 0))
    return pl.pallas_call(
        _wprep_kernel,
        grid=(depth,),
        in_specs=[layer(w_in.shape), pl.BlockSpec(bdc.shape, lambda l: (0, 0)), layer(wf_bd.shape)],
        out_specs=layer((depth, D_MODEL, D_PROJ)),
        out_shape=jax.ShapeDtypeStruct((depth, D_MODEL, D_PROJ), BF16),
        compiler_params=_params("arbitrary"),
        name="wprep",
    )(w_in, bdc, wf_bd)


def _inproj(x, mods, w_in, layer, rope, shared, sgu, tm, n_sub):
    n = x.shape[0]
    a = n // DFT_B
    row = lambda w: pl.BlockSpec((tm, w), lambda i: (i, 0))
    consts = list(rope) + list(shared)
    fx_rows = tm // DFT_B * SUBLANES
    return pl.pallas_call(
        functools.partial(_inproj_kernel, n_sub=n_sub),
        grid=(n // tm,),
        in_specs=[row(D_MODEL), _layer_spec(mods.shape, layer), _layer_spec(w_in.shape, layer)]
        + [_const_spec(c.shape) for c in consts] + [_layer_spec(c.shape, layer) for c in sgu],
        out_specs=[row(ATTN_W), row(2 * LANES), row(2 * LANES),
                   pl.BlockSpec((4, B_BLOCKS, fx_rows, LANES), lambda i: (0, 0, i, 0)), row(C_W)],
        out_shape=[jax.ShapeDtypeStruct((n, ATTN_W), BF16), jax.ShapeDtypeStruct((n, 2 * LANES), BF16),
                   jax.ShapeDtypeStruct((n, 2 * LANES), BF16),
                   jax.ShapeDtypeStruct((4, B_BLOCKS, a * SUBLANES, LANES), F32),
                   jax.ShapeDtypeStruct((n, C_W), BF16)],
        compiler_params=_params("arbitrary"),
        name="inproj",
    )(x, mods, w_in, *consts, *sgu)


def _attention(q, kd, vd, kc, vc, sink_tab, layer):
    n = q.shape[0]
    return pl.pallas_call(
        functools.partial(_attn_kernel, n_tokens=n),
        grid=(n // BLOCK,),
        in_specs=[pl.BlockSpec((BLOCK, ATTN_W), lambda i: (i, 0)), _const_spec(kd.shape),
                  _const_spec(vd.shape), _const_spec(kc.shape), _const_spec(vc.shape),
                  _layer_spec(sink_tab.shape, layer)],
        out_specs=pl.BlockSpec((BLOCK, ATTN_W), lambda i: (i, 0)),
        out_shape=jax.ShapeDtypeStruct((n, ATTN_W), BF16),
        compiler_params=_params("arbitrary"),
        name="attention",
    )(q, kd, vd, kc, vc, sink_tab)


def _fourier(fx, tabs1, tabs2, bf_row, layer):
    a = fx.shape[2] // SUBLANES
    kb = a // SUBLANES
    m_hi, m_lo, tc, ts = tabs1
    tw = pl.BlockSpec((None, a, SUBLANES), lambda j: (j, 0, 0))
    yarr = pl.pallas_call(
        _dft_stage1_kernel,
        grid=(B_BLOCKS,),
        in_specs=[pl.BlockSpec((4, None, a * SUBLANES, LANES), lambda j: (0, j, 0, 0)),
                  _const_spec(m_hi.shape), _const_spec(m_lo.shape), tw, tw],
        out_specs=pl.BlockSpec((4, kb, SUBLANES * SUBLANES, LANES), lambda j: (0, 0, j, 0)),
        out_shape=jax.ShapeDtypeStruct((4, kb, DFT_B * SUBLANES, LANES), F32),
        compiler_params=_params("arbitrary"),
        name="dft_stage1",
    )(fx, m_hi, m_lo, tc, ts)
    cs_hi, cs_lo = tabs2
    return pl.pallas_call(
        _dft_stage2_kernel,
        grid=(kb,),
        in_specs=[pl.BlockSpec((4, None, DFT_B * SUBLANES, LANES), lambda i: (0, i, 0, 0)),
                  _const_spec(cs_hi.shape), _const_spec(cs_lo.shape),
                  _layer_spec(bf_row.shape, layer)],
        out_specs=pl.BlockSpec((F_W // LANES, None, DFT_B * SUBLANES, LANES), lambda i: (0, i, 0, 0)),
        out_shape=jax.ShapeDtypeStruct((F_W // LANES, kb, DFT_B * SUBLANES, LANES), F32),
        compiler_params=_params("arbitrary"),
        name="dft_stage2",
    )(yarr, cs_hi, cs_lo, bf_row)


def _outffn(x, oa, of, sg, mods, weights, layer, tm, n_sub):
    n = x.shape[0]
    a = n // DFT_B
    row = lambda w: pl.BlockSpec((tm, w), lambda i: (i, 0))
    of_spec = pl.BlockSpec((F_W // LANES, a // SUBLANES, tm // a * SUBLANES, LANES),
                           lambda i: (0, 0, i, 0))
    consts = [mods] + list(weights)
    return pl.pallas_call(
        functools.partial(_outffn_kernel, n_sub=n_sub),
        grid=(n // tm,),
        in_specs=[row(D_MODEL), row(ATTN_W), of_spec, row(C_W)]
        + [_layer_spec(c.shape, layer) for c in consts],
        out_specs=row(D_MODEL),
        out_shape=jax.ShapeDtypeStruct((n, D_MODEL), F32),
        compiler_params=_params("arbitrary"),
        name="outffn",
    )(x, oa, of, sg, *consts)


def _context(xc, mods, w_in, shared, sgu, sink_tab, ctx_tabs, bf_row, weights):
    n_ctx = xc.shape[0]
    args = ([(xc, None), (mods, 0), (mods, 1), (w_in, 0), (w_in, 1)] + [(t, None) for t in shared]
            + [(t, 0) for t in sgu] + [(sink_tab, 0)] + [(t, None) for t in ctx_tabs]
            + [(bf_row, 0)] + [(t, 0) for t in weights])
    kv = jax.ShapeDtypeStruct((n_ctx, 2 * LANES), BF16)
    return pl.pallas_call(
        _ctx_kernel,
        grid=(1,),
        in_specs=[_const_spec(t.shape) if l is None else _layer_spec(t.shape, l) for t, l in args],
        out_specs=[pl.BlockSpec(kv.shape, lambda i: (0, 0))] * 4,
        out_shape=[kv, kv, kv, kv],
        compiler_params=_params("arbitrary"),
        name="context",
    )(*[t for t, _ in args])


def _block_diag_wf(w_f):
    bd = jnp.zeros((w_f.shape[0], F_W, F_W), F32)
    for g in range(F_GROUPS):
        bd = bd.at[:, g * F_DIM:(g + 1) * F_DIM, g * F_DIM:(g + 1) * F_DIM].set(w_f[:, g])
    return bd


def _forward(x, c, ctx, c_ctx, w_ada, b_ada, w_in, w_out, attn_sink, w_fourier, b_fourier,
             sgu_ln_g, sgu_ln_b, w_spatial, b_spatial, ln1_g, ln1_b, w_ffn_in, w_ffn_out,
             ln2_g, ln2_b, tm_in=1024, n_sub_in=4, tm_out=1024, n_sub=4):
    n = x.shape[1]
    n_ctx = ctx.shape[1]
    depth = w_in.shape[0]
    xs = x[0]
    cv = jnp.zeros((8, D_MODEL), F32).at[0].set(c[0]).at[1].set(c_ctx)
    mods = _ada(cv, w_ada, b_ada)

    w_in_b = _wprep(w_in, _channel_dft_tables(), _block_diag_wf(w_fourier))
    rows = lambda v: v.reshape(depth, 1, -1)
    weights = (w_out.astype(BF16), rows(ln1_g), rows(ln1_b), w_ffn_in.astype(BF16),
               w_ffn_out.astype(BF16), rows(ln2_g), rows(ln2_b))
    shared = (_group_avg_table(),)
    rope = _rope_tables(n)
    tabs1 = _stage1_tables(n // DFT_B)
    tabs2 = _stage2_tables()
    ctx_tabs = _ctx_dft_tables(n_ctx)
    sgu = (sgu_ln_g.reshape(depth, 1, C_W), sgu_ln_b.reshape(depth, 1, C_W),
           jnp.concatenate([w_spatial[:, h] for h in range(C_HEADS)], axis=2).astype(BF16),
           jnp.repeat(jnp.swapaxes(b_spatial, 1, 2), C_DIM, axis=2))
    sink_tab = jnp.broadcast_to(attn_sink[:, :, None], (depth, ATTN_HEADS, LANES))
    bf_row = b_fourier.reshape(depth, 1, F_W)

    ctx_kv = _context(ctx[0], mods, w_in_b, shared, sgu, sink_tab, ctx_tabs, bf_row, weights)

    for l in range(depth):
        q, kd, vd, fx, sg = _inproj(xs, mods, w_in_b, l, rope, shared, sgu, tm_in, n_sub_in)
        oa = _attention(q, kd, vd, ctx_kv[2 * l], ctx_kv[2 * l + 1], sink_tab, l)
        of = _fourier(fx, tabs1, tabs2, bf_row, l)
        xs = _outffn(xs, oa, of, sg, mods, weights, l, tm_out, n_sub)
    return xs[None]


def kernel(x, c, ctx, c_ctx, w_ada, b_ada, w_in, w_out, attn_sink, w_fourier, b_fourier, sgu_ln_g,
           sgu_ln_b, w_spatial, b_spatial, ln1_g, ln1_b, w_ffn_in, w_ffn_out, ln2_g, ln2_b):
    return _forward(x, c, ctx, c_ctx, w_ada, b_ada, w_in, w_out, attn_sink, w_fourier, b_fourier,
                    sgu_ln_g, sgu_ln_b, w_spatial, b_spatial, ln1_g, ln1_b, w_ffn_in, w_ffn_out,
                    ln2_g, ln2_b)
```

```python
import functools
import math

import numpy as np
import jax
import jax.numpy as jnp
from jax import lax
from jax.experimental import pallas as pl
from jax.experimental.pallas import tpu as pltpu

F32 = jnp.float32
BF16 = jnp.bfloat16

D_MODEL = 1024
DEPTH = 2
GRID_W = 64
HEAD_DIM = 64
ATTN_HEADS = 8
KV_HEADS = 2
WINDOW = 128
BLOCK = 128
ROPE_BASE = 10000.0
AXIS_DIM = HEAD_DIM // 2
F_GROUPS = 4
F_DIM = 64
C_HEADS = 4
C_DIM = 64
CHUNK = 128
D_FF = 2816
ATTN_W = ATTN_HEADS * HEAD_DIM
KV_W = KV_HEADS * HEAD_DIM
F_W = F_GROUPS * F_DIM
C_W = C_HEADS * C_DIM
D_IN = ATTN_W + 2 * KV_W + F_W + 2 * C_W
COL_K = ATTN_W
COL_V = COL_K + KV_W
COL_F = COL_V + KV_W
COL_U = COL_F + 2 * F_W
COL_G = COL_U + C_W
D_PROJ = COL_G + C_W
GQA_GROUP = ATTN_HEADS // KV_HEADS
HEAD_ORDER = tuple(kv * GQA_GROUP + g for g in range(GQA_GROUP) for kv in range(KV_HEADS))
Q_SCALE = HEAD_DIM ** -0.5 * math.log2(math.e)
ALPHA = (2 * DEPTH) ** 0.25
LN_EPS = 1e-6
NEG_BIG = -1e30

LANES = 128
SUBLANES = 8
MXU_TILE = 256
FFN_CHUNKS = (0, 4 * MXU_TILE, 8 * MXU_TILE, D_FF)
DFT_B = 128
B_BLOCKS = DFT_B // SUBLANES
VMEM_LIMIT = 56 * 1024 * 1024


def _dot(a, b):
    return jnp.dot(a, b, preferred_element_type=F32)


def _split(a):
    hi = a.astype(BF16)
    lo = (a - hi.astype(F32)).astype(BF16)
    return hi, lo


def _dot3(a_hi, a_lo, b_hi, b_lo):
    return _dot(a_hi, b_hi) + _dot(a_lo, b_hi) + _dot(a_hi, b_lo)


def _ln(x):
    mu = jnp.mean(x, axis=-1, keepdims=True)
    xc = x - mu
    var = jnp.mean(xc * xc, axis=-1, keepdims=True)
    return xc * lax.rsqrt(var + LN_EPS)


def _gelu(x):
    return 0.5 * x * (1.0 + jnp.tanh(math.sqrt(2.0 / math.pi) * (x + 0.044715 * (x * x * x))))


def _silu(x):
    return x / (1.0 + jnp.exp(-x))


def _mod_rows(mod, row):
    return [mod[row:row + 1, i * D_MODEL:(i + 1) * D_MODEL] for i in range(6)]


def _rope(t, tabs):
    cos, sin_a, sin_b = tabs
    return (t * cos + pltpu.roll(t, LANES - AXIS_DIM // 2, 1) * sin_a
            + pltpu.roll(t, AXIS_DIM // 2, 1) * sin_b)


def _rope_tile(rt_ref, ct_ref, first_grid_row, rows):
    tabs = []
    for t in range(3):
        groups = [rt_ref[t, pl.ds(first_grid_row + r, 1), :] + ct_ref[t]
                  for r in range(rows // GRID_W)]
        tabs.append(jnp.concatenate(groups, axis=0) if len(groups) > 1 else groups[0])
    return tabs


def _project(h, w_in_ref, lo, hi):
    return _dot(h, w_in_ref[:, lo:hi])


def _dup_halves(t):
    lane = lax.broadcasted_iota(jnp.int32, (1, LANES), 1)
    low = lane < HEAD_DIM
    sw = pltpu.roll(t, HEAD_DIM, 1)
    return jnp.where(low, t, sw), jnp.where(low, sw, t)


def _spatial_gate_stages(u, g, avg_ref, lng_ref, lnb_ref, wscat_ref, bs_ref):
    rows = u.shape[0]
    ug = _gelu(u)
    vg = _gelu(g)
    avg = avg_ref[...]
    v_hi, v_lo = _split(vg)
    mu = _dot(v_hi, avg) + _dot(v_lo, avg)
    yield None
    vc = vg - mu
    c_hi, c_lo = _split(vc * vc)
    var = _dot(c_hi, avg) + _dot(c_lo, avg)
    yield None
    vn = vc * lax.rsqrt(var + LN_EPS) * lng_ref[...] + lnb_ref[...]
    lane = lax.broadcasted_iota(jnp.int32, (1, C_W), 1)
    vnb = vn.astype(BF16)
    zero = jnp.zeros_like(vnb)
    outs = []
    for c in range(rows // CHUNK):
        blk = vnb[c * CHUNK:(c + 1) * CHUNK]
        rhs = jnp.concatenate(
            [jnp.where((lane >= h * C_DIM) & (lane < (h + 1) * C_DIM), blk, zero[:CHUNK])
             for h in range(C_HEADS)], axis=0)
        vs = _dot(wscat_ref[...], rhs) + bs_ref[...]
        outs.append(ug[c * CHUNK:(c + 1) * CHUNK] * vs)
    yield jnp.concatenate(outs, axis=0) if len(outs) > 1 else outs[0]


def _spatial_gate(*args):
    return list(_spatial_gate_stages(*args))[-1]


def _low_lanes():
    return lax.broadcasted_iota(jnp.int32, (1, LANES), 1) < HEAD_DIM


def _split_keys(k):
    low = _low_lanes()
    z = jnp.zeros_like(k)
    return jnp.where(low, k, z), jnp.where(low, z, k)


def _split_values(v):
    top, bot = _split_keys(v)
    low_ones = jnp.where(_low_lanes(), 1.0, 0.0)
    return (jnp.concatenate([top, jnp.broadcast_to(low_ones, v.shape).astype(BF16)], axis=1),
            jnp.concatenate([bot, jnp.broadcast_to(1.0 - low_ones, v.shape).astype(BF16)], axis=1))


def _stack_groups(q):
    return jnp.concatenate([q[:, g * LANES:(g + 1) * LANES] for g in range(ATTN_W // LANES)], axis=0)


def _sink_cols(sink_ref, rows):
    return [jnp.concatenate([jnp.broadcast_to(sink_ref[4 * kh + g:4 * kh + g + 1, 0:1], (rows, 1))
                             for g in range(4)], axis=0) for kh in range(KV_HEADS)]


def _pair_softmax(s, sinks):
    half = s.shape[1] // 2
    es, ts = [], []
    for kh in range(KV_HEADS):
        sh = s[:, kh * half:(kh + 1) * half]
        m = jnp.max(sh, axis=1, keepdims=True)
        es.append(jnp.exp2(sh - m).astype(BF16))
        ts.append(jnp.exp2(sinks[kh] - m))
    return jnp.concatenate(es, axis=1), ts


def _pair_normalize(o, ts):
    return o[:, :LANES] / (o[:, LANES:] + jnp.where(_low_lanes(), ts[0], ts[1]))


def _out_ffn_stages(x, o_cat, mod_row, w_out_ref, ln1g, ln1b, w1_ref, w2_ref, ln2g, ln2b):
    g_m, sh_f, sc_f, g_f = mod_row[2], mod_row[3], mod_row[4], mod_row[5]
    y = _dot(o_cat, w_out_ref[...])
    yield None
    x1 = _ln(ALPHA * x + g_m * y) * ln1g + ln1b
    h2 = (_ln(x1) * (1.0 + sc_f) + sh_f).astype(BF16)
    yield None
    y2 = None
    for lo, hi in zip(FFN_CHUNKS[:-1], FFN_CHUNKS[1:]):
        a = _dot(h2, w1_ref[:, lo:hi])
        b = _dot(h2, w1_ref[:, D_FF + lo:D_FF + hi])
        t = (_silu(a) * b).astype(BF16)
        part = _dot(t, w2_ref[lo:hi, :])
        y2 = part if y2 is None else y2 + part
        yield None
    yield _ln(ALPHA * x1 + g_f * y2) * ln2g + ln2b


OUT_FFN_STAGES = len(FFN_CHUNKS) + 2


def _run_skewed(gens, n_stages, skew):
    results = [None] * len(gens)
    for t in range(n_stages + skew * (len(gens) - 1)):
        for r, gen in enumerate(gens):
            if 0 <= t - r * skew < n_stages:
                results[r] = next(gen)
    return results


def _out_ffn(*args):
    return _run_skewed([_out_ffn_stages(*args)], OUT_FFN_STAGES, 0)[0]


def _strided_rows(ref, part, first, count):
    return ref[part, pl.ds(first, count, stride=SUBLANES), :]


def _ada_kernel(cv_ref, w_ref, b_ref, o_ref):
    s = _silu(cv_ref[...])
    o_ref[...] = jnp.dot(s, w_ref[...], precision=lax.Precision.HIGHEST,
                         preferred_element_type=F32) + b_ref[...]


def _wprep_kernel(w_in_ref, bdc_ref, wf_ref, o_ref):
    hp = functools.partial(jnp.dot, precision=lax.Precision.HIGHEST, preferred_element_type=F32)
    wf = wf_ref[...]
    fold = jnp.concatenate([hp(bdc_ref[:, :F_W], wf), hp(bdc_ref[:, F_W:], wf)], axis=1)
    for j, head in enumerate(HEAD_ORDER):
        o_ref[:, j * HEAD_DIM:(j + 1) * HEAD_DIM] = (
            w_in_ref[:, head * HEAD_DIM:(head + 1) * HEAD_DIM] * Q_SCALE).astype(BF16)
    o_ref[:, COL_K:COL_F] = w_in_ref[:, COL_K:COL_F].astype(BF16)
    o_ref[:, COL_F:COL_U] = hp(w_in_ref[:, COL_F:COL_F + F_W], fold).astype(BF16)
    o_ref[:, COL_U:] = w_in_ref[:, COL_F + F_W:].astype(BF16)


def _inproj_stages(x, sh, sc, w_ref, tabs, avg_ref, lng_ref, lnb_ref, wscat_ref, bs_ref,
                   q_ref, kd_ref, vd_ref, fx_ref, sg_ref, row0):
    rows = x.shape[0]
    rs = slice(row0, row0 + rows)
    h = (_ln(x) * (1.0 + sc) + sh).astype(BF16)
    yield
    for p in range(ATTN_W // LANES):
        qp = _rope(_project(h, w_ref, p * LANES, (p + 1) * LANES), tabs)
        q_ref[rs, p * LANES:(p + 1) * LANES] = qp.astype(BF16)
    kd_ref[rs, :] = _rope(_project(h, w_ref, COL_K, COL_V), tabs).astype(BF16)
    vd_ref[rs, :] = _project(h, w_ref, COL_V, COL_F).astype(BF16)
    yield
    f = _project(h, w_ref, COL_F, COL_U)
    for al in range(rows // DFT_B):
        for bb in range(B_BLOCKS):
            r0 = al * DFT_B + bb * SUBLANES
            a0 = (row0 // DFT_B + al) * SUBLANES
            for p in range(4):
                fx_ref[p, bb, a0:a0 + SUBLANES, :] = f[r0:r0 + SUBLANES, p * LANES:(p + 1) * LANES]
    u = _project(h, w_ref, COL_U, COL_G)
    g = _project(h, w_ref, COL_G, D_PROJ)
    gate = _spatial_gate_stages(u, g, avg_ref, lng_ref, lnb_ref, wscat_ref, bs_ref)
    next(gate)
    yield
    next(gate)
    yield
    sg_ref[rs, :] = next(gate).astype(BF16)
    yield


INPROJ_STAGES = 5


def _inproj_kernel(x_ref, mod_ref, w_ref, rt_ref, ct_ref, avg_ref, lng_ref, lnb_ref, wscat_ref,
                   bs_ref, q_ref, kd_ref, vd_ref, fx_ref, sg_ref, *, n_sub):
    tm = x_ref.shape[0]
    sub = tm // n_sub
    sh, sc = _mod_rows(mod_ref[...], 0)[:2]
    gens = []
    for r in range(n_sub):
        grid_row = pl.program_id(0) * (tm // GRID_W) + r * (sub // GRID_W)
        gens.append(_inproj_stages(
            x_ref[r * sub:(r + 1) * sub, :], sh, sc, w_ref, _rope_tile(rt_ref, ct_ref, grid_row, sub),
            avg_ref, lng_ref, lnb_ref, wscat_ref, bs_ref, q_ref, kd_ref, vd_ref, fx_ref, sg_ref,
            r * sub))
    _run_skewed(gens, INPROJ_STAGES, 1)


def _attn_block_stages(q_ref, k_ref, v_ref, kc_ref, vc_ref, sinks, o_ref, blk, row0, n_tokens):
    span = 3 * BLOCK
    n_ctx = kc_ref.shape[1]
    half = span + n_ctx
    start = pl.multiple_of(jnp.clip((blk - 1) * BLOCK, 0, n_tokens - span), BLOCK)
    k_top, k_bot = _split_keys(k_ref[pl.ds(start, span), :])
    v_top, v_bot = _split_values(v_ref[pl.ds(start, span), :])
    kk = jnp.concatenate([k_top, kc_ref[0], k_bot, kc_ref[1]], axis=0)
    vv = jnp.concatenate([v_top, vc_ref[0], v_bot, vc_ref[1]], axis=0)
    q4 = _stack_groups(q_ref[row0:row0 + BLOCK, :])
    yield
    s = lax.dot_general(q4, kk, (((1,), (1,)), ((), ())), preferred_element_type=F32)
    yield
    qpos = blk * BLOCK + lax.broadcasted_iota(jnp.int32, (BLOCK, span), 0)
    kpos = start + lax.broadcasted_iota(jnp.int32, (BLOCK, span), 1)
    ok = jnp.concatenate([jnp.abs(kpos - qpos) <= WINDOW] * 4, axis=0)
    s = jnp.concatenate([jnp.where(ok, s[:, :span], NEG_BIG), s[:, span:half],
                         jnp.where(ok, s[:, half:half + span], NEG_BIG), s[:, half + span:]], axis=1)
    e, ts = _pair_softmax(s, sinks)
    yield
    o = _dot(e, vv)
    yield
    res = _pair_normalize(o, ts).astype(BF16)
    for g in range(ATTN_W // LANES):
        o_ref[row0:row0 + BLOCK, g * LANES:(g + 1) * LANES] = res[g * BLOCK:(g + 1) * BLOCK]
    yield


ATTN_STAGES = 5


def _attn_kernel(q_ref, k_ref, v_ref, kc_ref, vc_ref, sink_ref, o_ref, *, n_tokens):
    n_blocks = q_ref.shape[0] // BLOCK
    sinks = _sink_cols(sink_ref, BLOCK)
    gens = [_attn_block_stages(q_ref, k_ref, v_ref, kc_ref, vc_ref, sinks, o_ref,
                               pl.program_id(0) * n_blocks + r, r * BLOCK, n_tokens)
            for r in range(n_blocks)]
    _run_skewed(gens, ATTN_STAGES, 1)


def _dft_stage1_kernel(x_ref, m_hi_ref, m_lo_ref, tc_ref, ts_ref, y_ref):
    a = m_hi_ref.shape[0] // 2
    for bl in range(SUBLANES):
        p = [_strided_rows(x_ref, i, bl, a) for i in range(4)]
        x = jnp.concatenate([jnp.concatenate(p[0:2], axis=1), jnp.concatenate(p[2:4], axis=1)], axis=0)
        x_hi, x_lo = _split(x)
        y = _dot3(m_hi_ref[...], m_lo_ref[...], x_hi, x_lo)
        yr, yi = y[:a], y[a:]
        tc, ts = tc_ref[:, bl:bl + 1], ts_ref[:, bl:bl + 1]
        wr = yr * tc + yi * ts
        wi = yi * tc - yr * ts
        parts = (wr[:, :LANES], wr[:, LANES:], wi[:, :LANES], wi[:, LANES:])
        for kb in range(a // SUBLANES):
            for i in range(4):
                y_ref[i, kb, bl * SUBLANES:(bl + 1) * SUBLANES, :] = (
                    parts[i][kb * SUBLANES:(kb + 1) * SUBLANES])


def _dft_stage2_kernel(y_ref, cs_hi_ref, cs_lo_ref, bf_ref, o_ref):
    for kl in range(SUBLANES):
        p = [_strided_rows(y_ref, i, kl, DFT_B) for i in range(4)]
        r = jnp.concatenate([jnp.concatenate(p[0:2], axis=1), jnp.concatenate(p[2:4], axis=1)], axis=0)
        r_hi, r_lo = _split(r)
        out = _dot3(cs_hi_ref[...], cs_lo_ref[...], r_hi, r_lo) + bf_ref[...]
        for hf in range(F_W // LANES):
            o_ref[hf, pl.ds(kl, DFT_B, stride=SUBLANES), :] = out[:, hf * LANES:(hf + 1) * LANES]


def _outffn_kernel(x_ref, oa_ref, of_ref, sg_ref, mod_ref, w_out_ref, ln1g_ref, ln1b_ref,
                   w1_ref, w2_ref, ln2g_ref, ln2b_ref, o_ref, *, n_sub):
    n_kb, k2_rows = of_ref.shape[1], of_ref.shape[2]
    halves = []
    for hf in range(F_W // LANES):
        pieces = [of_ref[hf, kb, k2 * SUBLANES:(k2 + 1) * SUBLANES, :]
                  for k2 in range(k2_rows // SUBLANES) for kb in range(n_kb)]
        halves.append(jnp.concatenate(pieces, axis=0).astype(BF16))
    o_cat = jnp.concatenate([oa_ref[...]] + halves + [sg_ref[...]], axis=1)
    tm = x_ref.shape[0]
    sub = tm // n_sub
    mod_row = _mod_rows(mod_ref[...], 0)
    gens = [_out_ffn_stages(x_ref[r * sub:(r + 1) * sub, :], o_cat[r * sub:(r + 1) * sub], mod_row,
                            w_out_ref, ln1g_ref[...], ln1b_ref[...], w1_ref, w2_ref,
                            ln2g_ref[...], ln2b_ref[...]) for r in range(n_sub)]
    for r, res in enumerate(_run_skewed(gens, OUT_FFN_STAGES, 1)):
        o_ref[r * sub:(r + 1) * sub, :] = res


def _ctx_kernel(xc_ref, mod0_ref, mod1_ref, w_in0_ref, w_in1_ref, avg_ref,
                lng_ref, lnb_ref, wscat_ref, bs_ref, sink_ref, cs_hi_ref, cs_lo_ref, bf_ref,
                w_out_ref, ln1g_ref, ln1b_ref, w1_ref, w2_ref, ln2g_ref, ln2b_ref,
                kc0_ref, vc0_ref, kc1_ref, vc1_ref):
    xc = xc_ref[...]
    n_ctx = xc.shape[0]
    m0 = _mod_rows(mod0_ref[...], 1)
    h = (_ln(xc) * (1.0 + m0[1]) + m0[0]).astype(BF16)
    q4 = _stack_groups(_project(h, w_in0_ref, 0, COL_K).astype(BF16))
    k_parts = _split_keys(_project(h, w_in0_ref, COL_K, COL_V).astype(BF16))
    v_parts = _split_values(_project(h, w_in0_ref, COL_V, COL_F).astype(BF16))
    kc0_ref[0], kc0_ref[1] = k_parts
    vc0_ref[0], vc0_ref[1] = v_parts
    kk = jnp.concatenate(k_parts, axis=0)
    vv = jnp.concatenate(v_parts, axis=0)
    s = lax.dot_general(q4, kk, (((1,), (1,)), ((), ())), preferred_element_type=F32)
    e, ts = _pair_softmax(s, _sink_cols(sink_ref, n_ctx))
    o4 = _pair_normalize(_dot(e, vv), ts)
    pairs = [o4[g * n_ctx:(g + 1) * n_ctx] for g in range(ATTN_W // LANES)]
    f = _project(h, w_in0_ref, COL_F, COL_U)
    y_hi, y_lo = _split(jnp.concatenate([f[:, :F_W], f[:, F_W:]], axis=0))
    o_f = _dot3(cs_hi_ref[...], cs_lo_ref[...], y_hi, y_lo) + bf_ref[...]
    u = _project(h, w_in0_ref, COL_U, COL_G)
    g = _project(h, w_in0_ref, COL_G, D_PROJ)
    sg = _spatial_gate(u, g, avg_ref, lng_ref, lnb_ref, wscat_ref, bs_ref)
    o_cat = jnp.concatenate([p.astype(BF16) for p in pairs] + [o_f.astype(BF16), sg.astype(BF16)],
                            axis=1)
    xc1 = _out_ffn(xc, o_cat, m0, w_out_ref, ln1g_ref[...], ln1b_ref[...], w1_ref, w2_ref,
                   ln2g_ref[...], ln2b_ref[...])
    m1 = _mod_rows(mod1_ref[...], 1)
    h1 = (_ln(xc1) * (1.0 + m1[1]) + m1[0]).astype(BF16)
    kc1_ref[0], kc1_ref[1] = _split_keys(_project(h1, w_in1_ref, COL_K, COL_V).astype(BF16))
    vc1_ref[0], vc1_ref[1] = _split_values(_project(h1, w_in1_ref, COL_V, COL_F).astype(BF16))


def _hi_lo(a):
    a32 = jnp.asarray(np.asarray(a, np.float64), F32)
    hi = a32.astype(BF16)
    return hi, (a32 - hi.astype(F32)).astype(BF16)


def _dft_cos_sin(n):
    idx = np.arange(n, dtype=np.int64)
    ang = 2.0 * np.pi * ((idx[:, None] * idx[None, :]) % n) / n
    return np.cos(ang), np.sin(ang)


def _channel_dft_tables():
    c, s = _dft_cos_sin(F_DIM)
    bd = np.zeros((F_W, 2 * F_W))
    for g in range(F_GROUPS):
        sl = slice(g * F_DIM, (g + 1) * F_DIM)
        bd[sl, sl] = c / math.sqrt(F_DIM)
        bd[sl, F_W + g * F_DIM:F_W + (g + 1) * F_DIM] = -s / math.sqrt(F_DIM)
    return jnp.asarray(bd, F32)


def _group_avg_table():
    a = np.zeros((C_W, C_W))
    for h in range(C_HEADS):
        a[h * C_DIM:(h + 1) * C_DIM, h * C_DIM:(h + 1) * C_DIM] = 1.0 / C_DIM
    return jnp.asarray(a, BF16)


def _rope_tables(n):
    freqs = jnp.asarray(ROPE_BASE, F32) ** (-jnp.arange(0, AXIS_DIM, 2, dtype=F32) / AXIS_DIM)
    reps = LANES // HEAD_DIM

    def tables(pos, row_axis):
        ang = pos[:, None] * freqs
        c, s = jnp.cos(ang), jnp.sin(ang)
        z = jnp.zeros_like(s)
        pad = [z, z]
        pick = (lambda t: t + pad) if row_axis else (lambda t: pad + t)
        return jnp.stack([jnp.tile(jnp.concatenate(pick(t), axis=1), (1, reps))
                          for t in ([c, c], [-s, z], [z, s])])

    rt = tables(jnp.arange(n // GRID_W, dtype=F32), True)
    ct = tables(jnp.arange(GRID_W, dtype=F32), False)
    return rt, ct


def _stage1_tables(a):
    n = a * DFT_B
    c, s = _dft_cos_sin(a)
    m_hi, m_lo = _hi_lo(np.block([[c, s], [-s, c]]) / math.sqrt(a))
    k1 = np.arange(a, dtype=np.int64)[None, :, None]
    b = (np.arange(B_BLOCKS, dtype=np.int64)[:, None, None] * SUBLANES
         + np.arange(SUBLANES, dtype=np.int64)[None, None, :])
    ang = 2.0 * np.pi * ((k1 * b) % n) / n
    return m_hi, m_lo, jnp.asarray(np.cos(ang), F32), jnp.asarray(np.sin(ang), F32)


def _stage2_tables():
    c, s = _dft_cos_sin(DFT_B)
    return _hi_lo(np.concatenate([c, s], axis=1) / math.sqrt(DFT_B))


def _ctx_dft_tables(n_ctx):
    c, s = _dft_cos_sin(n_ctx)
    return _hi_lo(np.concatenate([c, s], axis=1) / math.sqrt(n_ctx))


def _const_spec(shape):
    nd = len(shape)
    return pl.BlockSpec(shape, lambda *_: (0,) * nd, pipeline_mode=pl.Buffered(1))


def _layer_spec(shape, layer):
    nd = len(shape) - 1
    return pl.BlockSpec((None,) + tuple(shape[1:]), lambda *_: (layer,) + (0,) * nd,
                        pipeline_mode=pl.Buffered(1))


def _params(*sem):
    return pltpu.CompilerParams(dimension_semantics=sem, vmem_limit_bytes=VMEM_LIMIT)


def _ada(cv, w_ada, b_ada):
    tn = 1536
    return pl.pallas_call(
        _ada_kernel,
        grid=(DEPTH, 6 * D_MODEL // tn),
        in_specs=[pl.BlockSpec((8, D_MODEL), lambda l, j: (0, 0)),
                  pl.BlockSpec((None, D_MODEL, tn), lambda l, j: (l, 0, j)),
                  pl.BlockSpec((None, 1, tn), lambda l, j: (l, 0, j))],
        out_specs=pl.BlockSpec((None, 8, tn), lambda l, j: (l, 0, j)),
        out_shape=jax.ShapeDtypeStruct((DEPTH, 8, 6 * D_MODEL), F32),
        compiler_params=_params("arbitrary", "arbitrary"),
        name="ada",
    )(cv, w_ada, b_ada.reshape(DEPTH, 1, 6 * D_MODEL))


def _wprep(w_in, bdc, wf_bd):
    depth = w_in.shape[0]
    layer = lambda shape: pl.BlockSpec((None,) + tuple(shape[1:]), lambda l: (l, 0, 0))
    return pl.pallas_call(
        _wprep_kernel,
        grid=(depth,),
        in_specs=[layer(w_in.shape), pl.BlockSpec(bdc.shape, lambda l: (0, 0)), layer(wf_bd.shape)],
        out_specs=layer((depth, D_MODEL, D_PROJ)),
        out_shape=jax.ShapeDtypeStruct((depth, D_MODEL, D_PROJ), BF16),
        compiler_params=_params("arbitrary"),
        name="wprep",
    )(w_in, bdc, wf_bd)


def _inproj(x, mods, w_in, layer, rope, shared, sgu, tm, n_sub):
    n = x.shape[0]
    a = n // DFT_B
    row = lambda w: pl.BlockSpec((tm, w), lambda i: (i, 0))
    consts = list(rope) + list(shared)
    fx_rows = tm // DFT_B * SUBLANES
    return pl.pallas_call(
        functools.partial(_inproj_kernel, n_sub=n_sub),
        grid=(n // tm,),
        in_specs=[row(D_MODEL), _layer_spec(mods.shape, layer), _layer_spec(w_in.shape, layer)]
        + [_const_spec(c.shape) for c in consts] + [_layer_spec(c.shape, layer) for c in sgu],
        out_specs=[row(ATTN_W), row(KV_W), row(KV_W),
                   pl.BlockSpec((4, B_BLOCKS, fx_rows, LANES), lambda i: (0, 0, i, 0)), row(C_W)],
        out_shape=[jax.ShapeDtypeStruct((n, ATTN_W), BF16), jax.ShapeDtypeStruct((n, KV_W), BF16),
                   jax.ShapeDtypeStruct((n, KV_W), BF16),
                   jax.ShapeDtypeStruct((4, B_BLOCKS, a * SUBLANES, LANES), F32),
                   jax.ShapeDtypeStruct((n, C_W), BF16)],
        compiler_params=_params("arbitrary"),
        name="inproj",
    )(x, mods, w_in, *consts, *sgu)


def _attention(q, kd, vd, kc, vc, sink_tab, layer, tq):
    n = q.shape[0]
    return pl.pallas_call(
        functools.partial(_attn_kernel, n_tokens=n),
        grid=(n // tq,),
        in_specs=[pl.BlockSpec((tq, ATTN_W), lambda i: (i, 0)), _const_spec(kd.shape),
                  _const_spec(vd.shape), _const_spec(kc.shape), _const_spec(vc.shape),
                  _layer_spec(sink_tab.shape, layer)],
        out_specs=pl.BlockSpec((tq, ATTN_W), lambda i: (i, 0)),
        out_shape=jax.ShapeDtypeStruct((n, ATTN_W), BF16),
        compiler_params=_params("arbitrary"),
        name="attention",
    )(q, kd, vd, kc, vc, sink_tab)


def _fourier(fx, tabs1, tabs2, bf_row, layer):
    a = fx.shape[2] // SUBLANES
    kb = a // SUBLANES
    m_hi, m_lo, tc, ts = tabs1
    tw = pl.BlockSpec((None, a, SUBLANES), lambda j: (j, 0, 0))
    yarr = pl.pallas_call(
        _dft_stage1_kernel,
        grid=(B_BLOCKS,),
        in_specs=[pl.BlockSpec((4, None, a * SUBLANES, LANES), lambda j: (0, j, 0, 0)),
                  _const_spec(m_hi.shape), _const_spec(m_lo.shape), tw, tw],
        out_specs=pl.BlockSpec((4, kb, SUBLANES * SUBLANES, LANES), lambda j: (0, 0, j, 0)),
        out_shape=jax.ShapeDtypeStruct((4, kb, DFT_B * SUBLANES, LANES), F32),
        compiler_params=_params("arbitrary"),
        name="dft_stage1",
    )(fx, m_hi, m_lo, tc, ts)
    cs_hi, cs_lo = tabs2
    return pl.pallas_call(
        _dft_stage2_kernel,
        grid=(kb,),
        in_specs=[pl.BlockSpec((4, None, DFT_B * SUBLANES, LANES), lambda i: (0, i, 0, 0)),
                  _const_spec(cs_hi.shape), _const_spec(cs_lo.shape),
                  _layer_spec(bf_row.shape, layer)],
        out_specs=pl.BlockSpec((F_W // LANES, None, DFT_B * SUBLANES, LANES), lambda i: (0, i, 0, 0)),
        out_shape=jax.ShapeDtypeStruct((F_W // LANES, kb, DFT_B * SUBLANES, LANES), F32),
        compiler_params=_params("arbitrary"),
        name="dft_stage2",
    )(yarr, cs_hi, cs_lo, bf_row)


def _outffn(x, oa, of, sg, mods, weights, layer, tm, n_sub):
    n = x.shape[0]
    a = n // DFT_B
    row = lambda w: pl.BlockSpec((tm, w), lambda i: (i, 0))
    of_spec = pl.BlockSpec((F_W // LANES, a // SUBLANES, tm // a * SUBLANES, LANES),
                           lambda i: (0, 0, i, 0))
    consts = [mods] + list(weights)
    return pl.pallas_call(
        functools.partial(_outffn_kernel, n_sub=n_sub),
        grid=(n // tm,),
        in_specs=[row(D_MODEL), row(ATTN_W), of_spec, row(C_W)]
        + [_layer_spec(c.shape, layer) for c in consts],
        out_specs=row(D_MODEL),
        out_shape=jax.ShapeDtypeStruct((n, D_MODEL), F32),
        compiler_params=_params("arbitrary"),
        name="outffn",
    )(x, oa, of, sg, *consts)


def _context(xc, mods, w_in, shared, sgu, sink_tab, ctx_tabs, bf_row, weights):
    n_ctx = xc.shape[0]
    args = ([(xc, None), (mods, 0), (mods, 1), (w_in, 0), (w_in, 1)] + [(t, None) for t in shared]
            + [(t, 0) for t in sgu] + [(sink_tab, 0)] + [(t, None) for t in ctx_tabs]
            + [(bf_row, 0)] + [(t, 0) for t in weights])
    outs = [jax.ShapeDtypeStruct((KV_HEADS, n_ctx, w), BF16) for w in (LANES, 2 * LANES)] * 2
    return pl.pallas_call(
        _ctx_kernel,
        grid=(1,),
        in_specs=[_const_spec(t.shape) if l is None else _layer_spec(t.shape, l) for t, l in args],
        out_specs=[pl.BlockSpec(o.shape, lambda i: (0, 0, 0)) for o in outs],
        out_shape=outs,
        compiler_params=_params("arbitrary"),
        name="context",
    )(*[t for t, _ in args])


def _block_diag_wf(w_f):
    bd = jnp.zeros((w_f.shape[0], F_W, F_W), F32)
    for g in range(F_GROUPS):
        bd = bd.at[:, g * F_DIM:(g + 1) * F_DIM, g * F_DIM:(g + 1) * F_DIM].set(w_f[:, g])
    return bd


def _forward(x, c, ctx, c_ctx, w_ada, b_ada, w_in, w_out, attn_sink, w_fourier, b_fourier,
             sgu_ln_g, sgu_ln_b, w_spatial, b_spatial, ln1_g, ln1_b, w_ffn_in, w_ffn_out,
             ln2_g, ln2_b, tm_in=1024, n_sub_in=4, tq=512, tm_out=1024, n_sub=4):
    n = x.shape[1]
    n_ctx = ctx.shape[1]
    depth = w_in.shape[0]
    xs = x[0]
    cv = jnp.zeros((8, D_MODEL), F32).at[0].set(c[0]).at[1].set(c_ctx)
    mods = _ada(cv, w_ada, b_ada)

    w_in_b = _wprep(w_in, _channel_dft_tables(), _block_diag_wf(w_fourier))
    rows = lambda v: v.reshape(depth, 1, -1)
    w_out_p = jnp.concatenate([w_out[:, h * HEAD_DIM:(h + 1) * HEAD_DIM, :] for h in HEAD_ORDER]
                              + [w_out[:, ATTN_W:, :]], axis=1)
    weights = (w_out_p.astype(BF16), rows(ln1_g), rows(ln1_b), w_ffn_in.astype(BF16),
               w_ffn_out.astype(BF16), rows(ln2_g), rows(ln2_b))
    shared = (_group_avg_table(),)
    rope = _rope_tables(n)
    tabs1 = _stage1_tables(n // DFT_B)
    tabs2 = _stage2_tables()
    ctx_tabs = _ctx_dft_tables(n_ctx)
    sgu = (sgu_ln_g.reshape(depth, 1, C_W), sgu_ln_b.reshape(depth, 1, C_W),
           jnp.concatenate([w_spatial[:, h] for h in range(C_HEADS)], axis=2).astype(BF16),
           jnp.repeat(jnp.swapaxes(b_spatial, 1, 2), C_DIM, axis=2))
    sink_tab = jnp.broadcast_to((attn_sink * math.log2(math.e))[:, :, None], (depth, ATTN_HEADS, LANES))
    bf_row = b_fourier.reshape(depth, 1, F_W)

    ctx_kv = _context(ctx[0], mods, w_in_b, shared, sgu, sink_tab, ctx_tabs, bf_row, weights)

    for l in range(depth):
        q, kd, vd, fx, sg = _inproj(xs, mods, w_in_b, l, rope, shared, sgu, tm_in, n_sub_in)
        oa = _attention(q, kd, vd, ctx_kv[2 * l], ctx_kv[2 * l + 1], sink_tab, l, tq)
        of = _fourier(fx, tabs1, tabs2, bf_row, l)
        xs = _outffn(xs, oa, of, sg, mods, weights, l, tm_out, n_sub)
    return xs[None]


def kernel(x, c, ctx, c_ctx, w_ada, b_ada, w_in, w_out, attn_sink, w_fourier, b_fourier, sgu_ln_g,
           sgu_ln_b, w_spatial, b_spatial, ln1_g, ln1_b, w_ffn_in, w_ffn_out, ln2_g, ln2_b):
    return _forward(x, c, ctx, c_ctx, w_ada, b_ada, w_in, w_out, attn_sink, w_fourier, b_fourier,
                    sgu_ln_g, sgu_ln_b, w_spatial, b_spatial, ln1_g, ln1_b, w_ffn_in, w_ffn_out,
                    ln2_g, ln2_b)
```

```python
import functools
import math

import numpy as np
import jax
import jax.numpy as jnp
from jax import lax
from jax.experimental import pallas as pl
from jax.experimental.pallas import tpu as pltpu

F32 = jnp.float32
BF16 = jnp.bfloat16

D_MODEL = 1024
DEPTH = 2
GRID_W = 64
HEAD_DIM = 64
ATTN_HEADS = 8
KV_HEADS = 2
WINDOW = 128
BLOCK = 128
ROPE_BASE = 10000.0
AXIS_DIM = HEAD_DIM // 2
F_GROUPS = 4
F_DIM = 64
C_HEADS = 4
C_DIM = 64
CHUNK = 128
D_FF = 2816
ATTN_W = ATTN_HEADS * HEAD_DIM
KV_W = KV_HEADS * HEAD_DIM
F_W = F_GROUPS * F_DIM
C_W = C_HEADS * C_DIM
D_IN = ATTN_W + 2 * KV_W + F_W + 2 * C_W
COL_K = ATTN_W
COL_V = COL_K + KV_W
COL_F = COL_V + KV_W
COL_U = COL_F + 2 * F_W
COL_G = COL_U + C_W
D_PROJ = COL_G + C_W
N_COND = 2
GQA_GROUP = ATTN_HEADS // KV_HEADS
HEAD_ORDER = tuple(kv * GQA_GROUP + g for g in range(GQA_GROUP) for kv in range(KV_HEADS))
Q_SCALE = HEAD_DIM ** -0.5 * math.log2(math.e)
ALPHA = (2 * DEPTH) ** 0.25
LN_EPS = 1e-6
NEG_BIG = -1e30

LANES = 128
SUBLANES = 8
MXU_TILE = 256
FFN_CHUNKS = (0, 4 * MXU_TILE, 8 * MXU_TILE, D_FF)
DFT_B = 128
B_BLOCKS = DFT_B // SUBLANES
VMEM_LIMIT = 56 * 1024 * 1024


def _dot(a, b):
    return jnp.dot(a, b, preferred_element_type=F32)


def _split(a):
    hi = a.astype(BF16)
    lo = (a - hi.astype(F32)).astype(BF16)
    return hi, lo


def _dot3(a_hi, a_lo, b_hi, b_lo):
    return _dot(a_hi, b_hi) + _dot(a_lo, b_hi) + _dot(a_hi, b_lo)


def _ln(x):
    mu = jnp.mean(x, axis=-1, keepdims=True)
    xc = x - mu
    var = jnp.mean(xc * xc, axis=-1, keepdims=True)
    return xc * lax.rsqrt(var + LN_EPS)


def _gelu(x):
    return 0.5 * x * (1.0 + jnp.tanh(math.sqrt(2.0 / math.pi) * (x + 0.044715 * (x * x * x))))


def _silu(x):
    return x / (1.0 + jnp.exp(-x))


def _mod_rows(mod, row):
    return [mod[row:row + 1, i * D_MODEL:(i + 1) * D_MODEL] for i in range(6)]


def _rope(t, tabs):
    cos, sin_a, sin_b = tabs
    return (t * cos + pltpu.roll(t, LANES - AXIS_DIM // 2, 1) * sin_a
            + pltpu.roll(t, AXIS_DIM // 2, 1) * sin_b)


def _rope_tile(rt_ref, ct_ref, first_grid_row, rows):
    tabs = []
    for t in range(3):
        groups = [rt_ref[t, pl.ds(first_grid_row + r, 1), :] + ct_ref[t]
                  for r in range(rows // GRID_W)]
        tabs.append(jnp.concatenate(groups, axis=0) if len(groups) > 1 else groups[0])
    return tabs


def _project(h, w_in_ref, lo, hi):
    return _dot(h, w_in_ref[:, lo:hi])


def _spatial_gate_stages(u, g, avg_ref, lng_ref, lnb_ref, wscat_ref, bs_ref):
    rows = u.shape[0]
    ug = _gelu(u)
    vg = _gelu(g)
    avg = avg_ref[...]
    v_hi, v_lo = _split(vg)
    mu = _dot(v_hi, avg) + _dot(v_lo, avg)
    yield None
    vc = vg - mu
    c_hi, c_lo = _split(vc * vc)
    var = _dot(c_hi, avg) + _dot(c_lo, avg)
    yield None
    vn = vc * lax.rsqrt(var + LN_EPS) * lng_ref[...] + lnb_ref[...]
    lane = lax.broadcasted_iota(jnp.int32, (1, C_W), 1)
    vnb = vn.astype(BF16)
    zero = jnp.zeros_like(vnb)
    outs = []
    for c in range(rows // CHUNK):
        blk = vnb[c * CHUNK:(c + 1) * CHUNK]
        rhs = jnp.concatenate(
            [jnp.where((lane >= h * C_DIM) & (lane < (h + 1) * C_DIM), blk, zero[:CHUNK])
             for h in range(C_HEADS)], axis=0)
        vs = _dot(wscat_ref[...], rhs) + bs_ref[...]
        outs.append(ug[c * CHUNK:(c + 1) * CHUNK] * vs)
    yield jnp.concatenate(outs, axis=0) if len(outs) > 1 else outs[0]


def _spatial_gate(*args):
    return list(_spatial_gate_stages(*args))[-1]


def _low_lanes():
    return lax.broadcasted_iota(jnp.int32, (1, LANES), 1) < HEAD_DIM


def _split_keys(k):
    low = _low_lanes()
    z = jnp.zeros_like(k)
    return jnp.where(low, k, z), jnp.where(low, z, k)


def _split_values(v):
    top, bot = _split_keys(v)
    low_ones = jnp.where(_low_lanes(), 1.0, 0.0)
    return (jnp.concatenate([top, jnp.broadcast_to(low_ones, v.shape).astype(BF16)], axis=1),
            jnp.concatenate([bot, jnp.broadcast_to(1.0 - low_ones, v.shape).astype(BF16)], axis=1))


def _stack_groups(q):
    return jnp.concatenate([q[:, g * LANES:(g + 1) * LANES] for g in range(ATTN_W // LANES)], axis=0)


def _sink_cols(sink_ref, rows):
    return [jnp.concatenate([jnp.broadcast_to(sink_ref[4 * kh + g:4 * kh + g + 1, 0:1], (rows, 1))
                             for g in range(4)], axis=0) for kh in range(KV_HEADS)]


def _pair_softmax(s, sinks):
    half = s.shape[1] // 2
    es, ts = [], []
    for kh in range(KV_HEADS):
        sh = s[:, kh * half:(kh + 1) * half]
        m = jnp.max(sh, axis=1, keepdims=True)
        es.append(jnp.exp2(sh - m).astype(BF16))
        ts.append(jnp.exp2(sinks[kh] - m))
    return jnp.concatenate(es, axis=1), ts


def _pair_normalize(o, ts):
    return o[:, :LANES] / (o[:, LANES:] + jnp.where(_low_lanes(), ts[0], ts[1]))


def _out_ffn_stages(x, o_cat, mod_row, w_out_ref, ln1g, ln1b, w1_ref, w2_ref, ln2g, ln2b):
    g_m, sh_f, sc_f, g_f = mod_row[2], mod_row[3], mod_row[4], mod_row[5]
    y = _dot(o_cat, w_out_ref[...])
    yield None
    x1 = _ln(ALPHA * x + g_m * y) * ln1g + ln1b
    h2 = (_ln(x1) * (1.0 + sc_f) + sh_f).astype(BF16)
    yield None
    y2 = None
    for lo, hi in zip(FFN_CHUNKS[:-1], FFN_CHUNKS[1:]):
        a = _dot(h2, w1_ref[:, lo:hi])
        b = _dot(h2, w1_ref[:, D_FF + lo:D_FF + hi])
        t = (_silu(a) * b).astype(BF16)
        part = _dot(t, w2_ref[lo:hi, :])
        y2 = part if y2 is None else y2 + part
        yield None
    yield _ln(ALPHA * x1 + g_f * y2) * ln2g + ln2b


OUT_FFN_STAGES = len(FFN_CHUNKS) + 2


def _run_skewed(gens, n_stages, skew):
    results = [None] * len(gens)
    for t in range(n_stages + skew * (len(gens) - 1)):
        for r, gen in enumerate(gens):
            if 0 <= t - r * skew < n_stages:
                results[r] = next(gen)
    return results


def _out_ffn(*args):
    return _run_skewed([_out_ffn_stages(*args)], OUT_FFN_STAGES, 0)[0]


def _strided_rows(ref, part, first, count):
    return ref[part, pl.ds(first, count, stride=SUBLANES), :]


def _ada_kernel(cvt_ref, w_ref, b_ref, o_ref):
    s = _silu(cvt_ref[...])
    w = w_ref[...]
    o_ref[...] = jnp.zeros_like(o_ref)
    for r in range(N_COND):
        o_ref[r:r + 1, :] = jnp.sum(w * s[:, r:r + 1], axis=0, keepdims=True) + b_ref[...]


def _wprep_kernel(w_in_ref, bdc_ref, wf_ref, o_ref):
    hp = functools.partial(jnp.dot, precision=lax.Precision.HIGHEST, preferred_element_type=F32)
    wf = wf_ref[...]
    fold = jnp.concatenate([hp(bdc_ref[:, :F_W], wf), hp(bdc_ref[:, F_W:], wf)], axis=1)
    for j, head in enumerate(HEAD_ORDER):
        o_ref[:, j * HEAD_DIM:(j + 1) * HEAD_DIM] = (
            w_in_ref[:, head * HEAD_DIM:(head + 1) * HEAD_DIM] * Q_SCALE).astype(BF16)
    o_ref[:, COL_K:COL_F] = w_in_ref[:, COL_K:COL_F].astype(BF16)
    o_ref[:, COL_F:COL_U] = hp(w_in_ref[:, COL_F:COL_F + F_W], fold).astype(BF16)
    o_ref[:, COL_U:] = w_in_ref[:, COL_F + F_W:].astype(BF16)


def _inproj_stages(x, sh, sc, w_ref, tabs, avg_ref, lng_ref, lnb_ref, wscat_ref, bs_ref,
                   q_ref, kd_ref, vd_ref, fx_ref, sg_ref, row0):
    rows = x.shape[0]
    rs = slice(row0, row0 + rows)
    h = (_ln(x) * (1.0 + sc) + sh).astype(BF16)
    yield
    for p in range(ATTN_W // MXU_TILE):
        qq = _project(h, w_ref, p * MXU_TILE, (p + 1) * MXU_TILE)
        for j in range(MXU_TILE // LANES):
            c0 = p * MXU_TILE + j * LANES
            q_ref[rs, c0:c0 + LANES] = _rope(qq[:, j * LANES:(j + 1) * LANES], tabs).astype(BF16)
    kv = _project(h, w_ref, COL_K, COL_F)
    kd_ref[rs, :] = _rope(kv[:, :KV_W], tabs).astype(BF16)
    vd_ref[rs, :] = kv[:, KV_W:].astype(BF16)
    yield
    f = _project(h, w_ref, COL_F, COL_U)
    for al in range(rows // DFT_B):
        for bb in range(B_BLOCKS):
            r0 = al * DFT_B + bb * SUBLANES
            a0 = (row0 // DFT_B + al) * SUBLANES
            for p in range(4):
                fx_ref[p, bb, a0:a0 + SUBLANES, :] = f[r0:r0 + SUBLANES, p * LANES:(p + 1) * LANES]
    u = _project(h, w_ref, COL_U, COL_G)
    g = _project(h, w_ref, COL_G, D_PROJ)
    gate = _spatial_gate_stages(u, g, avg_ref, lng_ref, lnb_ref, wscat_ref, bs_ref)
    next(gate)
    yield
    next(gate)
    yield
    sg_ref[rs, :] = next(gate).astype(BF16)
    yield


INPROJ_STAGES = 5


def _inproj_kernel(x_ref, mod_ref, w_ref, rt_ref, ct_ref, avg_ref, lng_ref, lnb_ref, wscat_ref,
                   bs_ref, q_ref, kd_ref, vd_ref, fx_ref, sg_ref, *, n_sub):
    tm = x_ref.shape[0]
    sub = tm // n_sub
    sh, sc = _mod_rows(mod_ref[...], 0)[:2]
    gens = []
    for r in range(n_sub):
        grid_row = pl.program_id(0) * (tm // GRID_W) + r * (sub // GRID_W)
        gens.append(_inproj_stages(
            x_ref[r * sub:(r + 1) * sub, :], sh, sc, w_ref, _rope_tile(rt_ref, ct_ref, grid_row, sub),
            avg_ref, lng_ref, lnb_ref, wscat_ref, bs_ref, q_ref, kd_ref, vd_ref, fx_ref, sg_ref,
            r * sub))
    _run_skewed(gens, INPROJ_STAGES, 1)


def _attn_block_stages(q_ref, k_ref, v_ref, kc_ref, vc_ref, sinks, o_ref, blk, row0, n_tokens):
    span = 3 * BLOCK
    n_ctx = kc_ref.shape[1]
    half = span + n_ctx
    start = pl.multiple_of(jnp.clip((blk - 1) * BLOCK, 0, n_tokens - span), BLOCK)
    k_top, k_bot = _split_keys(k_ref[pl.ds(start, span), :])
    v_top, v_bot = _split_values(v_ref[pl.ds(start, span), :])
    kk = jnp.concatenate([k_top, kc_ref[0], k_bot, kc_ref[1]], axis=0)
    vv = jnp.concatenate([v_top, vc_ref[0], v_bot, vc_ref[1]], axis=0)
    q4 = _stack_groups(q_ref[row0:row0 + BLOCK, :])
    yield
    s = lax.dot_general(q4, kk, (((1,), (1,)), ((), ())), preferred_element_type=F32)
    yield
    qpos = blk * BLOCK + lax.broadcasted_iota(jnp.int32, (BLOCK, span), 0)
    kpos = start + lax.broadcasted_iota(jnp.int32, (BLOCK, span), 1)
    ok = jnp.concatenate([jnp.abs(kpos - qpos) <= WINDOW] * 4, axis=0)
    s = jnp.concatenate([jnp.where(ok, s[:, :span], NEG_BIG), s[:, span:half],
                         jnp.where(ok, s[:, half:half + span], NEG_BIG), s[:, half + span:]], axis=1)
    e, ts = _pair_softmax(s, sinks)
    yield
    o = _dot(e, vv)
    yield
    res = _pair_normalize(o, ts).astype(BF16)
    for g in range(ATTN_W // LANES):
        o_ref[row0:row0 + BLOCK, g * LANES:(g + 1) * LANES] = res[g * BLOCK:(g + 1) * BLOCK]
    yield


ATTN_STAGES = 5


def _attn_kernel(q_ref, k_ref, v_ref, kc_ref, vc_ref, sink_ref, o_ref, *, n_tokens):
    n_blocks = q_ref.shape[0] // BLOCK
    sinks = _sink_cols(sink_ref, BLOCK)
    gens = [_attn_block_stages(q_ref, k_ref, v_ref, kc_ref, vc_ref, sinks, o_ref,
                               pl.program_id(0) * n_blocks + r, r * BLOCK, n_tokens)
            for r in range(n_blocks)]
    _run_skewed(gens, ATTN_STAGES, 1)


def _dft_stage1_kernel(x_ref, m_hi_ref, m_lo_ref, tc_ref, ts_ref, y_ref):
    a = m_hi_ref.shape[0] // 2
    for bl in range(SUBLANES):
        p = [_strided_rows(x_ref, i, bl, a) for i in range(4)]
        x = jnp.concatenate([jnp.concatenate(p[0:2], axis=1), jnp.concatenate(p[2:4], axis=1)], axis=0)
        x_hi, x_lo = _split(x)
        y = _dot3(m_hi_ref[...], m_lo_ref[...], x_hi, x_lo)
        yr, yi = y[:a], y[a:]
        tc, ts = tc_ref[:, bl:bl + 1], ts_ref[:, bl:bl + 1]
        wr = yr * tc + yi * ts
        wi = yi * tc - yr * ts
        parts = (wr[:, :LANES], wr[:, LANES:], wi[:, :LANES], wi[:, LANES:])
        for kb in range(a // SUBLANES):
            for i in range(4):
                y_ref[i, kb, bl * SUBLANES:(bl + 1) * SUBLANES, :] = (
                    parts[i][kb * SUBLANES:(kb + 1) * SUBLANES])


def _dft_stage2_kernel(y_ref, cs_hi_ref, cs_lo_ref, bf_ref, o_ref):
    for kl in range(SUBLANES):
        p = [_strided_rows(y_ref, i, kl, DFT_B) for i in range(4)]
        r = jnp.concatenate([jnp.concatenate(p[0:2], axis=1), jnp.concatenate(p[2:4], axis=1)], axis=0)
        r_hi, r_lo = _split(r)
        out = _dot3(cs_hi_ref[...], cs_lo_ref[...], r_hi, r_lo) + bf_ref[...]
        for hf in range(F_W // LANES):
            o_ref[hf, pl.ds(kl, DFT_B, stride=SUBLANES), :] = out[:, hf * LANES:(hf + 1) * LANES]


def _outffn_kernel(x_ref, oa_ref, of_ref, sg_ref, mod_ref, w_out_ref, ln1g_ref, ln1b_ref,
                   w1_ref, w2_ref, ln2g_ref, ln2b_ref, o_ref, *, n_sub):
    n_kb, k2_rows = of_ref.shape[1], of_ref.shape[2]
    halves = []
    for hf in range(F_W // LANES):
        pieces = [of_ref[hf, kb, k2 * SUBLANES:(k2 + 1) * SUBLANES, :]
                  for k2 in range(k2_rows // SUBLANES) for kb in range(n_kb)]
        halves.append(jnp.concatenate(pieces, axis=0).astype(BF16))
    o_cat = jnp.concatenate([oa_ref[...]] + halves + [sg_ref[...]], axis=1)
    tm = x_ref.shape[0]
    sub = tm // n_sub
    mod_row = _mod_rows(mod_ref[...], 0)
    gens = [_out_ffn_stages(x_ref[r * sub:(r + 1) * sub, :], o_cat[r * sub:(r + 1) * sub], mod_row,
                            w_out_ref, ln1g_ref[...], ln1b_ref[...], w1_ref, w2_ref,
                            ln2g_ref[...], ln2b_ref[...]) for r in range(n_sub)]
    for r, res in enumerate(_run_skewed(gens, OUT_FFN_STAGES, 1)):
        o_ref[r * sub:(r + 1) * sub, :] = res


def _ctx_kernel(xc_ref, mod0_ref, mod1_ref, w_in0_ref, w_in1_ref, avg_ref,
                lng_ref, lnb_ref, wscat_ref, bs_ref, sink_ref, cs_hi_ref, cs_lo_ref, bf_ref,
                w_out_ref, ln1g_ref, ln1b_ref, w1_ref, w2_ref, ln2g_ref, ln2b_ref,
                kc0_ref, vc0_ref, kc1_ref, vc1_ref):
    xc = xc_ref[...]
    n_ctx = xc.shape[0]
    m0 = _mod_rows(mod0_ref[...], 1)
    h = (_ln(xc) * (1.0 + m0[1]) + m0[0]).astype(BF16)
    q4 = _stack_groups(_project(h, w_in0_ref, 0, COL_K).astype(BF16))
    k_parts = _split_keys(_project(h, w_in0_ref, COL_K, COL_V).astype(BF16))
    v_parts = _split_values(_project(h, w_in0_ref, COL_V, COL_F).astype(BF16))
    kc0_ref[0], kc0_ref[1] = k_parts
    vc0_ref[0], vc0_ref[1] = v_parts
    kk = jnp.concatenate(k_parts, axis=0)
    vv = jnp.concatenate(v_parts, axis=0)
    s = lax.dot_general(q4, kk, (((1,), (1,)), ((), ())), preferred_element_type=F32)
    e, ts = _pair_softmax(s, _sink_cols(sink_ref, n_ctx))
    o4 = _pair_normalize(_dot(e, vv), ts)
    pairs = [o4[g * n_ctx:(g + 1) * n_ctx] for g in range(ATTN_W // LANES)]
    f = _project(h, w_in0_ref, COL_F, COL_U)
    y_hi, y_lo = _split(jnp.concatenate([f[:, :F_W], f[:, F_W:]], axis=0))
    o_f = _dot3(cs_hi_ref[...], cs_lo_ref[...], y_hi, y_lo) + bf_ref[...]
    u = _project(h, w_in0_ref, COL_U, COL_G)
    g = _project(h, w_in0_ref, COL_G, D_PROJ)
    sg = _spatial_gate(u, g, avg_ref, lng_ref, lnb_ref, wscat_ref, bs_ref)
    o_cat = jnp.concatenate([p.astype(BF16) for p in pairs] + [o_f.astype(BF16), sg.astype(BF16)],
                            axis=1)
    xc1 = _out_ffn(xc, o_cat, m0, w_out_ref, ln1g_ref[...], ln1b_ref[...], w1_ref, w2_ref,
                   ln2g_ref[...], ln2b_ref[...])
    m1 = _mod_rows(mod1_ref[...], 1)
    h1 = (_ln(xc1) * (1.0 + m1[1]) + m1[0]).astype(BF16)
    kc1_ref[0], kc1_ref[1] = _split_keys(_project(h1, w_in1_ref, COL_K, COL_V).astype(BF16))
    vc1_ref[0], vc1_ref[1] = _split_values(_project(h1, w_in1_ref, COL_V, COL_F).astype(BF16))


def _hi_lo(a):
    a32 = jnp.asarray(np.asarray(a, np.float64), F32)
    hi = a32.astype(BF16)
    return hi, (a32 - hi.astype(F32)).astype(BF16)


def _dft_cos_sin(n):
    idx = np.arange(n, dtype=np.int64)
    ang = 2.0 * np.pi * ((idx[:, None] * idx[None, :]) % n) / n
    return np.cos(ang), np.sin(ang)


def _channel_dft_tables():
    c, s = _dft_cos_sin(F_DIM)
    bd = np.zeros((F_W, 2 * F_W))
    for g in range(F_GROUPS):
        sl = slice(g * F_DIM, (g + 1) * F_DIM)
        bd[sl, sl] = c / math.sqrt(F_DIM)
        bd[sl, F_W + g * F_DIM:F_W + (g + 1) * F_DIM] = -s / math.sqrt(F_DIM)
    return jnp.asarray(bd, F32)


def _group_avg_table():
    a = np.zeros((C_W, C_W))
    for h in range(C_HEADS):
        a[h * C_DIM:(h + 1) * C_DIM, h * C_DIM:(h + 1) * C_DIM] = 1.0 / C_DIM
    return jnp.asarray(a, BF16)


def _rope_tables(n):
    freqs = jnp.asarray(ROPE_BASE, F32) ** (-jnp.arange(0, AXIS_DIM, 2, dtype=F32) / AXIS_DIM)
    reps = LANES // HEAD_DIM

    def tables(pos, row_axis):
        ang = pos[:, None] * freqs
        c, s = jnp.cos(ang), jnp.sin(ang)
        z = jnp.zeros_like(s)
        pad = [z, z]
        pick = (lambda t: t + pad) if row_axis else (lambda t: pad + t)
        return jnp.stack([jnp.tile(jnp.concatenate(pick(t), axis=1), (1, reps))
                          for t in ([c, c], [-s, z], [z, s])])

    rt = tables(jnp.arange(n // GRID_W, dtype=F32), True)
    ct = tables(jnp.arange(GRID_W, dtype=F32), False)
    return rt, ct


def _stage1_tables(a):
    n = a * DFT_B
    c, s = _dft_cos_sin(a)
    m_hi, m_lo = _hi_lo(np.block([[c, s], [-s, c]]) / math.sqrt(a))
    k1 = np.arange(a, dtype=np.int64)[None, :, None]
    b = (np.arange(B_BLOCKS, dtype=np.int64)[:, None, None] * SUBLANES
         + np.arange(SUBLANES, dtype=np.int64)[None, None, :])
    ang = 2.0 * np.pi * ((k1 * b) % n) / n
    return m_hi, m_lo, jnp.asarray(np.cos(ang), F32), jnp.asarray(np.sin(ang), F32)


def _stage2_tables():
    c, s = _dft_cos_sin(DFT_B)
    return _hi_lo(np.concatenate([c, s], axis=1) / math.sqrt(DFT_B))


def _ctx_dft_tables(n_ctx):
    c, s = _dft_cos_sin(n_ctx)
    return _hi_lo(np.concatenate([c, s], axis=1) / math.sqrt(n_ctx))


def _const_spec(shape):
    nd = len(shape)
    return pl.BlockSpec(shape, lambda *_: (0,) * nd, pipeline_mode=pl.Buffered(1))


def _layer_spec(shape, layer):
    nd = len(shape) - 1
    return pl.BlockSpec((None,) + tuple(shape[1:]), lambda *_: (layer,) + (0,) * nd,
                        pipeline_mode=pl.Buffered(1))


def _params(*sem):
    return pltpu.CompilerParams(dimension_semantics=sem, vmem_limit_bytes=VMEM_LIMIT)


def _ada(cvt, w_ada, b_ada):
    tn = 1536
    return pl.pallas_call(
        _ada_kernel,
        grid=(DEPTH, 6 * D_MODEL // tn),
        in_specs=[pl.BlockSpec((D_MODEL, SUBLANES), lambda l, j: (0, 0)),
                  pl.BlockSpec((None, D_MODEL, tn), lambda l, j: (l, 0, j)),
                  pl.BlockSpec((None, 1, tn), lambda l, j: (l, 0, j))],
        out_specs=pl.BlockSpec((None, 8, tn), lambda l, j: (l, 0, j)),
        out_shape=jax.ShapeDtypeStruct((DEPTH, 8, 6 * D_MODEL), F32),
        compiler_params=_params("arbitrary", "arbitrary"),
        name="ada",
    )(cvt, w_ada, b_ada.reshape(DEPTH, 1, 6 * D_MODEL))


def _wprep(w_in, bdc, wf_bd):
    depth = w_in.shape[0]
    layer = lambda shape: pl.BlockSpec((None,) + tuple(shape[1:]), lambda l: (l, 0, 0))
    return pl.pallas_call(
        _wprep_kernel,
        grid=(depth,),
        in_specs=[layer(w_in.shape), pl.BlockSpec(bdc.shape, lambda l: (0, 0)), layer(wf_bd.shape)],
        out_specs=layer((depth, D_MODEL, D_PROJ)),
        out_shape=jax.ShapeDtypeStruct((depth, D_MODEL, D_PROJ), BF16),
        compiler_params=_params("arbitrary"),
        name="wprep",
    )(w_in, bdc, wf_bd)


def _inproj(x, mods, w_in, layer, rope, shared, sgu, tm, n_sub):
    n = x.shape[0]
    a = n // DFT_B
    row = lambda w: pl.BlockSpec((tm, w), lambda i: (i, 0))
    consts = list(rope) + list(shared)
    fx_rows = tm // DFT_B * SUBLANES
    return pl.pallas_call(
        functools.partial(_inproj_kernel, n_sub=n_sub),
        grid=(n // tm,),
        in_specs=[row(D_MODEL), _layer_spec(mods.shape, layer), _layer_spec(w_in.shape, layer)]
        + [_const_spec(c.shape) for c in consts] + [_layer_spec(c.shape, layer) for c in sgu],
        out_specs=[row(ATTN_W), row(KV_W), row(KV_W),
                   pl.BlockSpec((4, B_BLOCKS, fx_rows, LANES), lambda i: (0, 0, i, 0)), row(C_W)],
        out_shape=[jax.ShapeDtypeStruct((n, ATTN_W), BF16), jax.ShapeDtypeStruct((n, KV_W), BF16),
                   jax.ShapeDtypeStruct((n, KV_W), BF16),
                   jax.ShapeDtypeStruct((4, B_BLOCKS, a * SUBLANES, LANES), F32),
                   jax.ShapeDtypeStruct((n, C_W), BF16)],
        compiler_params=_params("arbitrary"),
        name="inproj",
    )(x, mods, w_in, *consts, *sgu)


def _attention(q, kd, vd, kc, vc, sink_tab, layer, tq):
    n = q.shape[0]
    return pl.pallas_call(
        functools.partial(_attn_kernel, n_tokens=n),
        grid=(n // tq,),
        in_specs=[pl.BlockSpec((tq, ATTN_W), lambda i: (i, 0)), _const_spec(kd.shape),
                  _const_spec(vd.shape), _const_spec(kc.shape), _const_spec(vc.shape),
                  _layer_spec(sink_tab.shape, layer)],
        out_specs=pl.BlockSpec((tq, ATTN_W), lambda i: (i, 0)),
        out_shape=jax.ShapeDtypeStruct((n, ATTN_W), BF16),
        compiler_params=_params("arbitrary"),
        name="attention",
    )(q, kd, vd, kc, vc, sink_tab)


def _fourier(fx, tabs1, tabs2, bf_row, layer):
    a = fx.shape[2] // SUBLANES
    kb = a // SUBLANES
    m_hi, m_lo, tc, ts = tabs1
    tw = pl.BlockSpec((None, a, SUBLANES), lambda j: (j, 0, 0))
    yarr = pl.pallas_call(
        _dft_stage1_kernel,
        grid=(B_BLOCKS,),
        in_specs=[pl.BlockSpec((4, None, a * SUBLANES, LANES), lambda j: (0, j, 0, 0)),
                  _const_spec(m_hi.shape), _const_spec(m_lo.shape), tw, tw],
        out_specs=pl.BlockSpec((4, kb, SUBLANES * SUBLANES, LANES), lambda j: (0, 0, j, 0)),
        out_shape=jax.ShapeDtypeStruct((4, kb, DFT_B * SUBLANES, LANES), F32),
        compiler_params=_params("arbitrary"),
        name="dft_stage1",
    )(fx, m_hi, m_lo, tc, ts)
    cs_hi, cs_lo = tabs2
    return pl.pallas_call(
        _dft_stage2_kernel,
        grid=(kb,),
        in_specs=[pl.BlockSpec((4, None, DFT_B * SUBLANES, LANES), lambda i: (0, i, 0, 0)),
                  _const_spec(cs_hi.shape), _const_spec(cs_lo.shape),
                  _layer_spec(bf_row.shape, layer)],
        out_specs=pl.BlockSpec((F_W // LANES, None, DFT_B * SUBLANES, LANES), lambda i: (0, i, 0, 0)),
        out_shape=jax.ShapeDtypeStruct((F_W // LANES, kb, DFT_B * SUBLANES, LANES), F32),
        compiler_params=_params("arbitrary"),
        name="dft_stage2",
    )(yarr, cs_hi, cs_lo, bf_row)


def _outffn(x, oa, of, sg, mods, weights, layer, tm, n_sub):
    n = x.shape[0]
    a = n // DFT_B
    row = lambda w: pl.BlockSpec((tm, w), lambda i: (i, 0))
    of_spec = pl.BlockSpec((F_W // LANES, a // SUBLANES, tm // a * SUBLANES, LANES),
                           lambda i: (0, 0, i, 0))
    consts = [mods] + list(weights)
    return pl.pallas_call(
        functools.partial(_outffn_kernel, n_sub=n_sub),
        grid=(n // tm,),
        in_specs=[row(D_MODEL), row(ATTN_W), of_spec, row(C_W)]
        + [_layer_spec(c.shape, layer) for c in consts],
        out_specs=row(D_MODEL),
        out_shape=jax.ShapeDtypeStruct((n, D_MODEL), F32),
        compiler_params=_params("arbitrary"),
        name="outffn",
    )(x, oa, of, sg, *consts)


def _context(xc, mods, w_in, shared, sgu, sink_tab, ctx_tabs, bf_row, weights):
    n_ctx = xc.shape[0]
    args = ([(xc, None), (mods, 0), (mods, 1), (w_in, 0), (w_in, 1)] + [(t, None) for t in shared]
            + [(t, 0) for t in sgu] + [(sink_tab, 0)] + [(t, None) for t in ctx_tabs]
            + [(bf_row, 0)] + [(t, 0) for t in weights])
    outs = [jax.ShapeDtypeStruct((KV_HEADS, n_ctx, w), BF16) for w in (LANES, 2 * LANES)] * 2
    return pl.pallas_call(
        _ctx_kernel,
        grid=(1,),
        in_specs=[_const_spec(t.shape) if l is None else _layer_spec(t.shape, l) for t, l in args],
        out_specs=[pl.BlockSpec(o.shape, lambda i: (0, 0, 0)) for o in outs],
        out_shape=outs,
        compiler_params=_params("arbitrary"),
        name="context",
    )(*[t for t, _ in args])


def _block_diag_wf(w_f):
    bd = jnp.zeros((w_f.shape[0], F_W, F_W), F32)
    for g in range(F_GROUPS):
        bd = bd.at[:, g * F_DIM:(g + 1) * F_DIM, g * F_DIM:(g + 1) * F_DIM].set(w_f[:, g])
    return bd


def _forward(x, c, ctx, c_ctx, w_ada, b_ada, w_in, w_out, attn_sink, w_fourier, b_fourier,
             sgu_ln_g, sgu_ln_b, w_spatial, b_spatial, ln1_g, ln1_b, w_ffn_in, w_ffn_out,
             ln2_g, ln2_b, tm_in=1024, n_sub_in=4, tq=512, tm_out=1024, n_sub=4):
    n = x.shape[1]
    n_ctx = ctx.shape[1]
    depth = w_in.shape[0]
    xs = x[0]
    cvt = jnp.zeros((D_MODEL, SUBLANES), F32).at[:, 0].set(c[0]).at[:, 1].set(c_ctx)
    mods = _ada(cvt, w_ada, b_ada)

    w_in_b = _wprep(w_in, _channel_dft_tables(), _block_diag_wf(w_fourier))
    rows = lambda v: v.reshape(depth, 1, -1)
    w_out_p = jnp.concatenate([w_out[:, h * HEAD_DIM:(h + 1) * HEAD_DIM, :] for h in HEAD_ORDER]
                              + [w_out[:, ATTN_W:, :]], axis=1)
    weights = (w_out_p.astype(BF16), rows(ln1_g), rows(ln1_b), w_ffn_in.astype(BF16),
               w_ffn_out.astype(BF16), rows(ln2_g), rows(ln2_b))
    shared = (_group_avg_table(),)
    rope = _rope_tables(n)
    tabs1 = _stage1_tables(n // DFT_B)
    tabs2 = _stage2_tables()
    ctx_tabs = _ctx_dft_tables(n_ctx)
    sgu = (sgu_ln_g.reshape(depth, 1, C_W), sgu_ln_b.reshape(depth, 1, C_W),
           jnp.concatenate([w_spatial[:, h] for h in range(C_HEADS)], axis=2).astype(BF16),
           jnp.repeat(jnp.swapaxes(b_spatial, 1, 2), C_DIM, axis=2))
    sink_tab = jnp.broadcast_to((attn_sink * math.log2(math.e))[:, :, None], (depth, ATTN_HEADS, LANES))
    bf_row = b_fourier.reshape(depth, 1, F_W)

    ctx_kv = _context(ctx[0], mods, w_in_b, shared, sgu, sink_tab, ctx_tabs, bf_row, weights)

    for l in range(depth):
        q, kd, vd, fx, sg = _inproj(xs, mods, w_in_b, l, rope, shared, sgu, tm_in, n_sub_in)
        oa = _attention(q, kd, vd, ctx_kv[2 * l], ctx_kv[2 * l + 1], sink_tab, l, tq)
        of = _fourier(fx, tabs1, tabs2, bf_row, l)
        xs = _outffn(xs, oa, of, sg, mods, weights, l, tm_out, n_sub)
    return xs[None]


def kernel(x, c, ctx, c_ctx, w_ada, b_ada, w_in, w_out, attn_sink, w_fourier, b_fourier, sgu_ln_g,
           sgu_ln_b, w_spatial, b_spatial, ln1_g, ln1_b, w_ffn_in, w_ffn_out, ln2_g, ln2_b):
    return _forward(x, c, ctx, c_ctx, w_ada, b_ada, w_in, w_out, attn_sink, w_fourier, b_fourier,
                    sgu_ln_g, sgu_ln_b, w_spatial, b_spatial, ln1_g, ln1_b, w_ffn_in, w_ffn_out,
                    ln2_g, ln2_b)
```

```python
import functools
import math

import numpy as np
import jax
import jax.numpy as jnp
from jax import lax
from jax.experimental import pallas as pl
from jax.experimental.pallas import tpu as pltpu

F32 = jnp.float32
BF16 = jnp.bfloat16

D_MODEL = 1024
DEPTH = 2
GRID_W = 64
HEAD_DIM = 64
ATTN_HEADS = 8
KV_HEADS = 2
WINDOW = 128
BLOCK = 128
ROPE_BASE = 10000.0
AXIS_DIM = HEAD_DIM // 2
F_GROUPS = 4
F_DIM = 64
C_HEADS = 4
C_DIM = 64
CHUNK = 128
D_FF = 2816
ATTN_W = ATTN_HEADS * HEAD_DIM
KV_W = KV_HEADS * HEAD_DIM
F_W = F_GROUPS * F_DIM
C_W = C_HEADS * C_DIM
D_IN = ATTN_W + 2 * KV_W + F_W + 2 * C_W
COL_K = ATTN_W
COL_V = COL_K + KV_W
COL_F = COL_V + KV_W
COL_U = COL_F + 2 * F_W
COL_G = COL_U + C_W
D_PROJ = COL_G + C_W
N_COND = 2
GQA_GROUP = ATTN_HEADS // KV_HEADS
HEAD_ORDER = tuple(kv * GQA_GROUP + g for g in range(GQA_GROUP) for kv in range(KV_HEADS))
Q_SCALE = HEAD_DIM ** -0.5 * math.log2(math.e)
ALPHA = (2 * DEPTH) ** 0.25
LN_EPS = 1e-6
NEG_BIG = -1e30

LANES = 128
SUBLANES = 8
MXU_TILE = 256
FFN_CHUNKS = (0, 4 * MXU_TILE, 8 * MXU_TILE, D_FF)
DFT_B = 128
B_BLOCKS = DFT_B // SUBLANES
VMEM_LIMIT = 56 * 1024 * 1024


def _dot(a, b):
    return jnp.dot(a, b, preferred_element_type=F32)


def _split(a):
    hi = a.astype(BF16)
    lo = (a - hi.astype(F32)).astype(BF16)
    return hi, lo


def _dot3(a_hi, a_lo, b_hi, b_lo):
    return _dot(a_hi, b_hi) + _dot(a_lo, b_hi) + _dot(a_hi, b_lo)


def _ln(x):
    mu = jnp.mean(x, axis=-1, keepdims=True)
    xc = x - mu
    var = jnp.mean(xc * xc, axis=-1, keepdims=True)
    return xc * lax.rsqrt(var + LN_EPS)


def _gelu(x):
    return 0.5 * x * (1.0 + jnp.tanh(math.sqrt(2.0 / math.pi) * (x + 0.044715 * (x * x * x))))


def _silu(x):
    return x / (1.0 + jnp.exp(-x))


def _mod_rows(mod, row):
    return [mod[row:row + 1, i * D_MODEL:(i + 1) * D_MODEL] for i in range(6)]


def _rope(t, tabs):
    cos, sin_a, sin_b = tabs
    return (t * cos + pltpu.roll(t, LANES - AXIS_DIM // 2, 1) * sin_a
            + pltpu.roll(t, AXIS_DIM // 2, 1) * sin_b)


def _rope_tile(rt_ref, ct_ref, first_grid_row, rows):
    tabs = []
    for t in range(3):
        groups = [rt_ref[t, pl.ds(first_grid_row + r, 1), :] + ct_ref[t]
                  for r in range(rows // GRID_W)]
        tabs.append(jnp.concatenate(groups, axis=0) if len(groups) > 1 else groups[0])
    return tabs


def _project(h, w_in_ref, lo, hi):
    return _dot(h, w_in_ref[:, lo:hi])


def _spatial_gate_stages(u, g, avg_ref, lng_ref, lnb_ref, wscat_ref, bs_ref):
    rows = u.shape[0]
    ug = _gelu(u)
    vg = _gelu(g)
    avg = avg_ref[...]
    v_hi, v_lo = _split(vg)
    mu = _dot(v_hi, avg) + _dot(v_lo, avg)
    yield None
    vc = vg - mu
    c_hi, c_lo = _split(vc * vc)
    var = _dot(c_hi, avg) + _dot(c_lo, avg)
    yield None
    vn = vc * lax.rsqrt(var + LN_EPS) * lng_ref[...] + lnb_ref[...]
    lane = lax.broadcasted_iota(jnp.int32, (1, C_W), 1)
    vnb = vn.astype(BF16)
    zero = jnp.zeros_like(vnb)
    outs = []
    for c in range(rows // CHUNK):
        blk = vnb[c * CHUNK:(c + 1) * CHUNK]
        rhs = jnp.concatenate(
            [jnp.where((lane >= h * C_DIM) & (lane < (h + 1) * C_DIM), blk, zero[:CHUNK])
             for h in range(C_HEADS)], axis=0)
        vs = _dot(wscat_ref[...], rhs) + bs_ref[...]
        outs.append(ug[c * CHUNK:(c + 1) * CHUNK] * vs)
    yield jnp.concatenate(outs, axis=0) if len(outs) > 1 else outs[0]


def _spatial_gate(*args):
    return list(_spatial_gate_stages(*args))[-1]


def _low_lanes():
    return lax.broadcasted_iota(jnp.int32, (1, LANES), 1) < HEAD_DIM


def _split_keys(k):
    low = _low_lanes()
    z = jnp.zeros_like(k)
    return jnp.where(low, k, z), jnp.where(low, z, k)


def _split_values(v):
    top, bot = _split_keys(v)
    low_ones = jnp.where(_low_lanes(), 1.0, 0.0)
    return (jnp.concatenate([top, jnp.broadcast_to(low_ones, v.shape).astype(BF16)], axis=1),
            jnp.concatenate([bot, jnp.broadcast_to(1.0 - low_ones, v.shape).astype(BF16)], axis=1))


def _stack_groups(q):
    return jnp.concatenate([q[:, g * LANES:(g + 1) * LANES] for g in range(ATTN_W // LANES)], axis=0)


def _sink_cols(sink_ref, rows):
    return [jnp.concatenate([jnp.broadcast_to(sink_ref[4 * kh + g:4 * kh + g + 1, 0:1], (rows, 1))
                             for g in range(4)], axis=0) for kh in range(KV_HEADS)]


def _pair_softmax(s, sinks):
    half = s.shape[1] // 2
    es, ts = [], []
    for kh in range(KV_HEADS):
        sh = s[:, kh * half:(kh + 1) * half]
        m = jnp.max(sh, axis=1, keepdims=True)
        es.append(jnp.exp2(sh - m).astype(BF16))
        ts.append(jnp.exp2(sinks[kh] - m))
    return jnp.concatenate(es, axis=1), ts


def _pair_normalize(o, ts):
    return o[:, :LANES] / (o[:, LANES:] + jnp.where(_low_lanes(), ts[0], ts[1]))


def _out_ffn_stages(x, o_cat, mod_row, w_out_ref, ln1g, ln1b, w1_ref, w2_ref, ln2g, ln2b):
    g_m, sh_f, sc_f, g_f = mod_row[2], mod_row[3], mod_row[4], mod_row[5]
    y = _dot(o_cat, w_out_ref[...])
    yield None
    x1 = _ln(ALPHA * x + g_m * y) * ln1g + ln1b
    h2 = (_ln(x1) * (1.0 + sc_f) + sh_f).astype(BF16)
    yield None
    y2 = None
    for lo, hi in zip(FFN_CHUNKS[:-1], FFN_CHUNKS[1:]):
        a = _dot(h2, w1_ref[:, lo:hi])
        b = _dot(h2, w1_ref[:, D_FF + lo:D_FF + hi])
        t = (_silu(a) * b).astype(BF16)
        part = _dot(t, w2_ref[lo:hi, :])
        y2 = part if y2 is None else y2 + part
        yield None
    yield _ln(ALPHA * x1 + g_f * y2) * ln2g + ln2b


OUT_FFN_STAGES = len(FFN_CHUNKS) + 2


def _run_skewed(gens, n_stages, skew):
    results = [None] * len(gens)
    for t in range(n_stages + skew * (len(gens) - 1)):
        for r, gen in enumerate(gens):
            if 0 <= t - r * skew < n_stages:
                results[r] = next(gen)
    return results


def _out_ffn(*args):
    return _run_skewed([_out_ffn_stages(*args)], OUT_FFN_STAGES, 0)[0]


def _strided_rows(ref, part, first, count):
    return ref[part, pl.ds(first, count, stride=SUBLANES), :]


def _ada_kernel(cvt_ref, w_ref, b_ref, o_ref):
    s = _silu(cvt_ref[...])
    w = w_ref[...]
    o_ref[...] = jnp.zeros_like(o_ref)
    for r in range(N_COND):
        o_ref[r:r + 1, :] = jnp.sum(w * s[:, r:r + 1], axis=0, keepdims=True) + b_ref[...]


def _wprep_kernel(w_in_ref, bdc_ref, wf_ref, o_ref):
    hp = functools.partial(jnp.dot, precision=lax.Precision.HIGHEST, preferred_element_type=F32)
    wf = wf_ref[...]
    fold = jnp.concatenate([hp(bdc_ref[:, :F_W], wf), hp(bdc_ref[:, F_W:], wf)], axis=1)
    for j, head in enumerate(HEAD_ORDER):
        o_ref[:, j * HEAD_DIM:(j + 1) * HEAD_DIM] = (
            w_in_ref[:, head * HEAD_DIM:(head + 1) * HEAD_DIM] * Q_SCALE).astype(BF16)
    o_ref[:, COL_K:COL_F] = w_in_ref[:, COL_K:COL_F].astype(BF16)
    o_ref[:, COL_F:COL_U] = hp(w_in_ref[:, COL_F:COL_F + F_W], fold).astype(BF16)
    o_ref[:, COL_U:] = w_in_ref[:, COL_F + F_W:].astype(BF16)


def _inproj_stages(x, sh, sc, w_ref, tabs, avg_ref, lng_ref, lnb_ref, wscat_ref, bs_ref,
                   q_ref, kd_ref, vd_ref, fx_ref, sg_ref, row0):
    rows = x.shape[0]
    rs = slice(row0, row0 + rows)
    h = (_ln(x) * (1.0 + sc) + sh).astype(BF16)
    yield
    for p in range(ATTN_W // MXU_TILE):
        qq = _project(h, w_ref, p * MXU_TILE, (p + 1) * MXU_TILE)
        for j in range(MXU_TILE // LANES):
            c0 = p * MXU_TILE + j * LANES
            q_ref[rs, c0:c0 + LANES] = _rope(qq[:, j * LANES:(j + 1) * LANES], tabs).astype(BF16)
    kv = _project(h, w_ref, COL_K, COL_F)
    kd_ref[rs, :] = _rope(kv[:, :KV_W], tabs).astype(BF16)
    vd_ref[rs, :] = kv[:, KV_W:].astype(BF16)
    yield
    f = _project(h, w_ref, COL_F, COL_U)
    for al in range(rows // DFT_B):
        for bb in range(B_BLOCKS):
            r0 = al * DFT_B + bb * SUBLANES
            a0 = (row0 // DFT_B + al) * SUBLANES
            for p in range(4):
                fx_ref[p, bb, a0:a0 + SUBLANES, :] = f[r0:r0 + SUBLANES, p * LANES:(p + 1) * LANES]
    u = _project(h, w_ref, COL_U, COL_G)
    g = _project(h, w_ref, COL_G, D_PROJ)
    gate = _spatial_gate_stages(u, g, avg_ref, lng_ref, lnb_ref, wscat_ref, bs_ref)
    next(gate)
    yield
    next(gate)
    yield
    sg_ref[rs, :] = next(gate).astype(BF16)
    yield


INPROJ_STAGES = 5


def _inproj_kernel(x_ref, mod_ref, w_ref, rt_ref, ct_ref, avg_ref, lng_ref, lnb_ref, wscat_ref,
                   bs_ref, q_ref, kd_ref, vd_ref, fx_ref, sg_ref, *, n_sub):
    tm = x_ref.shape[0]
    sub = tm // n_sub
    sh, sc = _mod_rows(mod_ref[...], 0)[:2]
    gens = []
    for r in range(n_sub):
        grid_row = pl.program_id(0) * (tm // GRID_W) + r * (sub // GRID_W)
        gens.append(_inproj_stages(
            x_ref[r * sub:(r + 1) * sub, :], sh, sc, w_ref, _rope_tile(rt_ref, ct_ref, grid_row, sub),
            avg_ref, lng_ref, lnb_ref, wscat_ref, bs_ref, q_ref, kd_ref, vd_ref, fx_ref, sg_ref,
            r * sub))
    _run_skewed(gens, INPROJ_STAGES, 1)


def _attn_block_stages(q_ref, k_ref, v_ref, kc_ref, vc_ref, sinks, o_ref, blk, row0, n_tokens):
    span = 3 * BLOCK
    n_ctx = kc_ref.shape[1]
    half = span + n_ctx
    start = pl.multiple_of(jnp.clip((blk - 1) * BLOCK, 0, n_tokens - span), BLOCK)
    k_top, k_bot = _split_keys(k_ref[pl.ds(start, span), :])
    v_top, v_bot = _split_values(v_ref[pl.ds(start, span), :])
    kk = jnp.concatenate([k_top, kc_ref[0], k_bot, kc_ref[1]], axis=0)
    vv = jnp.concatenate([v_top, vc_ref[0], v_bot, vc_ref[1]], axis=0)
    q4 = _stack_groups(q_ref[row0:row0 + BLOCK, :])
    yield
    s = lax.dot_general(q4, kk, (((1,), (1,)), ((), ())), preferred_element_type=F32)
    yield
    qpos = blk * BLOCK + lax.broadcasted_iota(jnp.int32, (BLOCK, span), 0)
    kpos = start + lax.broadcasted_iota(jnp.int32, (BLOCK, span), 1)
    ok = jnp.concatenate([jnp.abs(kpos - qpos) <= WINDOW] * 4, axis=0)
    s = jnp.concatenate([jnp.where(ok, s[:, :span], NEG_BIG), s[:, span:half],
                         jnp.where(ok, s[:, half:half + span], NEG_BIG), s[:, half + span:]], axis=1)
    e, ts = _pair_softmax(s, sinks)
    yield
    o = _dot(e, vv)
    yield
    res = _pair_normalize(o, ts).astype(BF16)
    for g in range(ATTN_W // LANES):
        o_ref[row0:row0 + BLOCK, g * LANES:(g + 1) * LANES] = res[g * BLOCK:(g + 1) * BLOCK]
    yield


ATTN_STAGES = 5


def _attn_kernel(q_ref, k_ref, v_ref, kc_ref, vc_ref, sink_ref, o_ref, *, n_tokens):
    n_blocks = q_ref.shape[0] // BLOCK
    sinks = _sink_cols(sink_ref, BLOCK)
    gens = [_attn_block_stages(q_ref, k_ref, v_ref, kc_ref, vc_ref, sinks, o_ref,
                               pl.program_id(0) * n_blocks + r, r * BLOCK, n_tokens)
            for r in range(n_blocks)]
    _run_skewed(gens, ATTN_STAGES, 1)


def _complex_rows(parts):
    return _split(jnp.concatenate([jnp.concatenate(parts[0:2], axis=1),
                                   jnp.concatenate(parts[2:4], axis=1)], axis=0))


def _dft_stage1_stages(x_ref, m_hi_ref, m_lo_ref, tc_ref, ts_ref, y_scr, bl, row0):
    a = m_hi_ref.shape[0] // 2
    x_hi, x_lo = _complex_rows([_strided_rows(x_ref, i, bl, a) for i in range(4)])
    yield
    y = _dot3(m_hi_ref[...], m_lo_ref[...], x_hi, x_lo)
    yield
    yr, yi = y[:a], y[a:]
    tc, ts = tc_ref[:, bl:bl + 1], ts_ref[:, bl:bl + 1]
    wr = yr * tc + yi * ts
    wi = yi * tc - yr * ts
    parts = (wr[:, :LANES], wr[:, LANES:], wi[:, :LANES], wi[:, LANES:])
    for kb in range(a // SUBLANES):
        for i in range(4):
            y_scr[i, kb, pl.ds(row0, SUBLANES), :] = parts[i][kb * SUBLANES:(kb + 1) * SUBLANES]
    yield


def _dft_stage2_stages(y_scr, kb, cs_hi_ref, cs_lo_ref, bf_ref, o_ref, kl):
    r_hi, r_lo = _complex_rows([y_scr[i, kb, pl.ds(kl, DFT_B, stride=SUBLANES), :] for i in range(4)])
    yield
    out = _dot3(cs_hi_ref[...], cs_lo_ref[...], r_hi, r_lo) + bf_ref[...]
    yield
    for hf in range(F_W // LANES):
        o_ref[hf, pl.ds(kl, DFT_B, stride=SUBLANES), :] = out[:, hf * LANES:(hf + 1) * LANES]
    yield


DFT_STAGES = 3


def _dft_kernel(x_ref, m_hi_ref, m_lo_ref, tc_ref, ts_ref, cs_hi_ref, cs_lo_ref, bf_ref, o_ref, y_scr):
    step = pl.program_id(0)

    @pl.when(step < B_BLOCKS)
    def _():
        gens = [_dft_stage1_stages(x_ref, m_hi_ref, m_lo_ref, tc_ref, ts_ref, y_scr, bl,
                                   pl.multiple_of((step * SUBLANES + bl) * SUBLANES, SUBLANES))
                for bl in range(SUBLANES)]
        _run_skewed(gens, DFT_STAGES, 1)

    @pl.when(step >= B_BLOCKS)
    def _():
        gens = [_dft_stage2_stages(y_scr, step - B_BLOCKS, cs_hi_ref, cs_lo_ref, bf_ref, o_ref, kl)
                for kl in range(SUBLANES)]
        _run_skewed(gens, DFT_STAGES, 1)


def _outffn_kernel(x_ref, oa_ref, of_ref, sg_ref, mod_ref, w_out_ref, ln1g_ref, ln1b_ref,
                   w1_ref, w2_ref, ln2g_ref, ln2b_ref, o_ref, *, n_sub):
    n_kb, k2_rows = of_ref.shape[1], of_ref.shape[2]
    halves = []
    for hf in range(F_W // LANES):
        pieces = [of_ref[hf, kb, k2 * SUBLANES:(k2 + 1) * SUBLANES, :]
                  for k2 in range(k2_rows // SUBLANES) for kb in range(n_kb)]
        halves.append(jnp.concatenate(pieces, axis=0).astype(BF16))
    o_cat = jnp.concatenate([oa_ref[...]] + halves + [sg_ref[...]], axis=1)
    tm = x_ref.shape[0]
    sub = tm // n_sub
    mod_row = _mod_rows(mod_ref[...], 0)
    gens = [_out_ffn_stages(x_ref[r * sub:(r + 1) * sub, :], o_cat[r * sub:(r + 1) * sub], mod_row,
                            w_out_ref, ln1g_ref[...], ln1b_ref[...], w1_ref, w2_ref,
                            ln2g_ref[...], ln2b_ref[...]) for r in range(n_sub)]
    for r, res in enumerate(_run_skewed(gens, OUT_FFN_STAGES, 1)):
        o_ref[r * sub:(r + 1) * sub, :] = res


def _ctx_kernel(xc_ref, mod0_ref, mod1_ref, w_in0_ref, w_in1_ref, avg_ref,
                lng_ref, lnb_ref, wscat_ref, bs_ref, sink_ref, cs_hi_ref, cs_lo_ref, bf_ref,
                w_out_ref, ln1g_ref, ln1b_ref, w1_ref, w2_ref, ln2g_ref, ln2b_ref,
                kc0_ref, vc0_ref, kc1_ref, vc1_ref):
    xc = xc_ref[...]
    n_ctx = xc.shape[0]
    m0 = _mod_rows(mod0_ref[...], 1)
    h = (_ln(xc) * (1.0 + m0[1]) + m0[0]).astype(BF16)
    q4 = _stack_groups(_project(h, w_in0_ref, 0, COL_K).astype(BF16))
    k_parts = _split_keys(_project(h, w_in0_ref, COL_K, COL_V).astype(BF16))
    v_parts = _split_values(_project(h, w_in0_ref, COL_V, COL_F).astype(BF16))
    kc0_ref[0], kc0_ref[1] = k_parts
    vc0_ref[0], vc0_ref[1] = v_parts
    kk = jnp.concatenate(k_parts, axis=0)
    vv = jnp.concatenate(v_parts, axis=0)
    s = lax.dot_general(q4, kk, (((1,), (1,)), ((), ())), preferred_element_type=F32)
    e, ts = _pair_softmax(s, _sink_cols(sink_ref, n_ctx))
    o4 = _pair_normalize(_dot(e, vv), ts)
    pairs = [o4[g * n_ctx:(g + 1) * n_ctx] for g in range(ATTN_W // LANES)]
    f = _project(h, w_in0_ref, COL_F, COL_U)
    y_hi, y_lo = _split(jnp.concatenate([f[:, :F_W], f[:, F_W:]], axis=0))
    o_f = _dot3(cs_hi_ref[...], cs_lo_ref[...], y_hi, y_lo) + bf_ref[...]
    u = _project(h, w_in0_ref, COL_U, COL_G)
    g = _project(h, w_in0_ref, COL_G, D_PROJ)
    sg = _spatial_gate(u, g, avg_ref, lng_ref, lnb_ref, wscat_ref, bs_ref)
    o_cat = jnp.concatenate([p.astype(BF16) for p in pairs] + [o_f.astype(BF16), sg.astype(BF16)],
                            axis=1)
    xc1 = _out_ffn(xc, o_cat, m0, w_out_ref, ln1g_ref[...], ln1b_ref[...], w1_ref, w2_ref,
                   ln2g_ref[...], ln2b_ref[...])
    m1 = _mod_rows(mod1_ref[...], 1)
    h1 = (_ln(xc1) * (1.0 + m1[1]) + m1[0]).astype(BF16)
    kc1_ref[0], kc1_ref[1] = _split_keys(_project(h1, w_in1_ref, COL_K, COL_V).astype(BF16))
    vc1_ref[0], vc1_ref[1] = _split_values(_project(h1, w_in1_ref, COL_V, COL_F).astype(BF16))


def _hi_lo(a):
    a32 = jnp.asarray(np.asarray(a, np.float64), F32)
    hi = a32.astype(BF16)
    return hi, (a32 - hi.astype(F32)).astype(BF16)


def _dft_cos_sin(n):
    idx = np.arange(n, dtype=np.int64)
    ang = 2.0 * np.pi * ((idx[:, None] * idx[None, :]) % n) / n
    return np.cos(ang), np.sin(ang)


def _channel_dft_tables():
    c, s = _dft_cos_sin(F_DIM)
    bd = np.zeros((F_W, 2 * F_W))
    for g in range(F_GROUPS):
        sl = slice(g * F_DIM, (g + 1) * F_DIM)
        bd[sl, sl] = c / math.sqrt(F_DIM)
        bd[sl, F_W + g * F_DIM:F_W + (g + 1) * F_DIM] = -s / math.sqrt(F_DIM)
    return jnp.asarray(bd, F32)


def _group_avg_table():
    a = np.zeros((C_W, C_W))
    for h in range(C_HEADS):
        a[h * C_DIM:(h + 1) * C_DIM, h * C_DIM:(h + 1) * C_DIM] = 1.0 / C_DIM
    return jnp.asarray(a, BF16)


def _rope_tables(n):
    freqs = jnp.asarray(ROPE_BASE, F32) ** (-jnp.arange(0, AXIS_DIM, 2, dtype=F32) / AXIS_DIM)
    reps = LANES // HEAD_DIM

    def tables(pos, row_axis):
        ang = pos[:, None] * freqs
        c, s = jnp.cos(ang), jnp.sin(ang)
        z = jnp.zeros_like(s)
        pad = [z, z]
        pick = (lambda t: t + pad) if row_axis else (lambda t: pad + t)
        return jnp.stack([jnp.tile(jnp.concatenate(pick(t), axis=1), (1, reps))
                          for t in ([c, c], [-s, z], [z, s])])

    rt = tables(jnp.arange(n // GRID_W, dtype=F32), True)
    ct = tables(jnp.arange(GRID_W, dtype=F32), False)
    return rt, ct


def _stage1_tables(a):
    n = a * DFT_B
    c, s = _dft_cos_sin(a)
    m_hi, m_lo = _hi_lo(np.block([[c, s], [-s, c]]) / math.sqrt(a))
    k1 = np.arange(a, dtype=np.int64)[None, :, None]
    b = (np.arange(B_BLOCKS, dtype=np.int64)[:, None, None] * SUBLANES
         + np.arange(SUBLANES, dtype=np.int64)[None, None, :])
    ang = 2.0 * np.pi * ((k1 * b) % n) / n
    return m_hi, m_lo, jnp.asarray(np.cos(ang), F32), jnp.asarray(np.sin(ang), F32)


def _stage2_tables():
    c, s = _dft_cos_sin(DFT_B)
    return _hi_lo(np.concatenate([c, s], axis=1) / math.sqrt(DFT_B))


def _ctx_dft_tables(n_ctx):
    c, s = _dft_cos_sin(n_ctx)
    return _hi_lo(np.concatenate([c, s], axis=1) / math.sqrt(n_ctx))


def _const_spec(shape):
    nd = len(shape)
    return pl.BlockSpec(shape, lambda *_: (0,) * nd, pipeline_mode=pl.Buffered(1))


def _layer_spec(shape, layer):
    nd = len(shape) - 1
    return pl.BlockSpec((None,) + tuple(shape[1:]), lambda *_: (layer,) + (0,) * nd,
                        pipeline_mode=pl.Buffered(1))


def _params(*sem):
    return pltpu.CompilerParams(dimension_semantics=sem, vmem_limit_bytes=VMEM_LIMIT)


def _ada(cvt, w_ada, b_ada):
    tn = 1536
    return pl.pallas_call(
        _ada_kernel,
        grid=(DEPTH, 6 * D_MODEL // tn),
        in_specs=[pl.BlockSpec((D_MODEL, SUBLANES), lambda l, j: (0, 0)),
                  pl.BlockSpec((None, D_MODEL, tn), lambda l, j: (l, 0, j)),
                  pl.BlockSpec((None, 1, tn), lambda l, j: (l, 0, j))],
        out_specs=pl.BlockSpec((None, 8, tn), lambda l, j: (l, 0, j)),
        out_shape=jax.ShapeDtypeStruct((DEPTH, 8, 6 * D_MODEL), F32),
        compiler_params=_params("arbitrary", "arbitrary"),
        name="ada",
    )(cvt, w_ada, b_ada.reshape(DEPTH, 1, 6 * D_MODEL))


def _wprep(w_in, bdc, wf_bd):
    depth = w_in.shape[0]
    layer = lambda shape: pl.BlockSpec((None,) + tuple(shape[1:]), lambda l: (l, 0, 0))
    return pl.pallas_call(
        _wprep_kernel,
        grid=(depth,),
        in_specs=[layer(w_in.shape), pl.BlockSpec(bdc.shape, lambda l: (0, 0)), layer(wf_bd.shape)],
        out_specs=layer((depth, D_MODEL, D_PROJ)),
        out_shape=jax.ShapeDtypeStruct((depth, D_MODEL, D_PROJ), BF16),
        compiler_params=_params("arbitrary"),
        name="wprep",
    )(w_in, bdc, wf_bd)


def _inproj(x, mods, w_in, layer, rope, shared, sgu, tm, n_sub):
    n = x.shape[0]
    a = n // DFT_B
    row = lambda w: pl.BlockSpec((tm, w), lambda i: (i, 0))
    consts = list(rope) + list(shared)
    fx_rows = tm // DFT_B * SUBLANES
    return pl.pallas_call(
        functools.partial(_inproj_kernel, n_sub=n_sub),
        grid=(n // tm,),
        in_specs=[row(D_MODEL), _layer_spec(mods.shape, layer), _layer_spec(w_in.shape, layer)]
        + [_const_spec(c.shape) for c in consts] + [_layer_spec(c.shape, layer) for c in sgu],
        out_specs=[row(ATTN_W), row(KV_W), row(KV_W),
                   pl.BlockSpec((4, B_BLOCKS, fx_rows, LANES), lambda i: (0, 0, i, 0)), row(C_W)],
        out_shape=[jax.ShapeDtypeStruct((n, ATTN_W), BF16), jax.ShapeDtypeStruct((n, KV_W), BF16),
                   jax.ShapeDtypeStruct((n, KV_W), BF16),
                   jax.ShapeDtypeStruct((4, B_BLOCKS, a * SUBLANES, LANES), F32),
                   jax.ShapeDtypeStruct((n, C_W), BF16)],
        compiler_params=_params("arbitrary"),
        name="inproj",
    )(x, mods, w_in, *consts, *sgu)


def _attention(q, kd, vd, kc, vc, sink_tab, layer, tq):
    n = q.shape[0]
    return pl.pallas_call(
        functools.partial(_attn_kernel, n_tokens=n),
        grid=(n // tq,),
        in_specs=[pl.BlockSpec((tq, ATTN_W), lambda i: (i, 0)), _const_spec(kd.shape),
                  _const_spec(vd.shape), _const_spec(kc.shape), _const_spec(vc.shape),
                  _layer_spec(sink_tab.shape, layer)],
        out_specs=pl.BlockSpec((tq, ATTN_W), lambda i: (i, 0)),
        out_shape=jax.ShapeDtypeStruct((n, ATTN_W), BF16),
        compiler_params=_params("arbitrary"),
        name="attention",
    )(q, kd, vd, kc, vc, sink_tab)


def _fourier(fx, tabs1, tabs2, bf_row, layer):
    a = fx.shape[2] // SUBLANES
    kb = a // SUBLANES
    m_hi, m_lo, tc, ts = tabs1
    cs_hi, cs_lo = tabs2
    in_blk = lambda s: jnp.minimum(s, B_BLOCKS - 1)
    out_blk = lambda s: jnp.maximum(s - B_BLOCKS, 0)
    tw = pl.BlockSpec((None, a, SUBLANES), lambda s: (in_blk(s), 0, 0))
    return pl.pallas_call(
        _dft_kernel,
        grid=(B_BLOCKS + kb,),
        in_specs=[pl.BlockSpec((4, None, a * SUBLANES, LANES), lambda s: (0, in_blk(s), 0, 0)),
                  _const_spec(m_hi.shape), _const_spec(m_lo.shape), tw, tw,
                  _const_spec(cs_hi.shape), _const_spec(cs_lo.shape),
                  _layer_spec(bf_row.shape, layer)],
        out_specs=pl.BlockSpec((F_W // LANES, None, DFT_B * SUBLANES, LANES),
                               lambda s: (0, out_blk(s), 0, 0)),
        out_shape=jax.ShapeDtypeStruct((F_W // LANES, kb, DFT_B * SUBLANES, LANES), F32),
        scratch_shapes=[pltpu.VMEM((4, kb, DFT_B * SUBLANES, LANES), F32)],
        compiler_params=_params("arbitrary"),
        name="dft",
    )(fx, m_hi, m_lo, tc, ts, cs_hi, cs_lo, bf_row)


def _outffn(x, oa, of, sg, mods, weights, layer, tm, n_sub):
    n = x.shape[0]
    a = n // DFT_B
    row = lambda w: pl.BlockSpec((tm, w), lambda i: (i, 0))
    of_spec = pl.BlockSpec((F_W // LANES, a // SUBLANES, tm // a * SUBLANES, LANES),
                           lambda i: (0, 0, i, 0))
    consts = [mods] + list(weights)
    return pl.pallas_call(
        functools.partial(_outffn_kernel, n_sub=n_sub),
        grid=(n // tm,),
        in_specs=[row(D_MODEL), row(ATTN_W), of_spec, row(C_W)]
        + [_layer_spec(c.shape, layer) for c in consts],
        out_specs=row(D_MODEL),
        out_shape=jax.ShapeDtypeStruct((n, D_MODEL), F32),
        compiler_params=_params("arbitrary"),
        name="outffn",
    )(x, oa, of, sg, *consts)


def _context(xc, mods, w_in, shared, sgu, sink_tab, ctx_tabs, bf_row, weights):
    n_ctx = xc.shape[0]
    args = ([(xc, None), (mods, 0), (mods, 1), (w_in, 0), (w_in, 1)] + [(t, None) for t in shared]
            + [(t, 0) for t in sgu] + [(sink_tab, 0)] + [(t, None) for t in ctx_tabs]
            + [(bf_row, 0)] + [(t, 0) for t in weights])
    outs = [jax.ShapeDtypeStruct((KV_HEADS, n_ctx, w), BF16) for w in (LANES, 2 * LANES)] * 2
    return pl.pallas_call(
        _ctx_kernel,
        grid=(1,),
        in_specs=[_const_spec(t.shape) if l is None else _layer_spec(t.shape, l) for t, l in args],
        out_specs=[pl.BlockSpec(o.shape, lambda i: (0, 0, 0)) for o in outs],
        out_shape=outs,
        compiler_params=_params("arbitrary"),
        name="context",
    )(*[t for t, _ in args])


def _block_diag_wf(w_f):
    bd = jnp.zeros((w_f.shape[0], F_W, F_W), F32)
    for g in range(F_GROUPS):
        bd = bd.at[:, g * F_DIM:(g + 1) * F_DIM, g * F_DIM:(g + 1) * F_DIM].set(w_f[:, g])
    return bd


def _forward(x, c, ctx, c_ctx, w_ada, b_ada, w_in, w_out, attn_sink, w_fourier, b_fourier,
             sgu_ln_g, sgu_ln_b, w_spatial, b_spatial, ln1_g, ln1_b, w_ffn_in, w_ffn_out,
             ln2_g, ln2_b, tm_in=1024, n_sub_in=4, tq=512, tm_out=1024, n_sub=4):
    n = x.shape[1]
    n_ctx = ctx.shape[1]
    depth = w_in.shape[0]
    xs = x[0]
    cvt = jnp.zeros((D_MODEL, SUBLANES), F32).at[:, 0].set(c[0]).at[:, 1].set(c_ctx)
    mods = _ada(cvt, w_ada, b_ada)

    w_in_b = _wprep(w_in, _channel_dft_tables(), _block_diag_wf(w_fourier))
    rows = lambda v: v.reshape(depth, 1, -1)
    w_out_p = jnp.concatenate([w_out[:, h * HEAD_DIM:(h + 1) * HEAD_DIM, :] for h in HEAD_ORDER]
                              + [w_out[:, ATTN_W:, :]], axis=1)
    weights = (w_out_p.astype(BF16), rows(ln1_g), rows(ln1_b), w_ffn_in.astype(BF16),
               w_ffn_out.astype(BF16), rows(ln2_g), rows(ln2_b))
    shared = (_group_avg_table(),)
    rope = _rope_tables(n)
    tabs1 = _stage1_tables(n // DFT_B)
    tabs2 = _stage2_tables()
    ctx_tabs = _ctx_dft_tables(n_ctx)
    sgu = (sgu_ln_g.reshape(depth, 1, C_W), sgu_ln_b.reshape(depth, 1, C_W),
           jnp.concatenate([w_spatial[:, h] for h in range(C_HEADS)], axis=2).astype(BF16),
           jnp.repeat(jnp.swapaxes(b_spatial, 1, 2), C_DIM, axis=2))
    sink_tab = jnp.broadcast_to((attn_sink * math.log2(math.e))[:, :, None], (depth, ATTN_HEADS, LANES))
    bf_row = b_fourier.reshape(depth, 1, F_W)

    ctx_kv = _context(ctx[0], mods, w_in_b, shared, sgu, sink_tab, ctx_tabs, bf_row, weights)

    for l in range(depth):
        q, kd, vd, fx, sg = _inproj(xs, mods, w_in_b, l, rope, shared, sgu, tm_in, n_sub_in)
        oa = _attention(q, kd, vd, ctx_kv[2 * l], ctx_kv[2 * l + 1], sink_tab, l, tq)
        of = _fourier(fx, tabs1, tabs2, bf_row, l)
        xs = _outffn(xs, oa, of, sg, mods, weights, l, tm_out, n_sub)
    return xs[None]


def kernel(x, c, ctx, c_ctx, w_ada, b_ada, w_in, w_out, attn_sink, w_fourier, b_fourier, sgu_ln_g,
           sgu_ln_b, w_spatial, b_spatial, ln1_g, ln1_b, w_ffn_in, w_ffn_out, ln2_g, ln2_b):
    return _forward(x, c, ctx, c_ctx, w_ada, b_ada, w_in, w_out, attn_sink, w_fourier, b_fourier,
                    sgu_ln_g, sgu_ln_b, w_spatial, b_spatial, ln1_g, ln1_b, w_ffn_in, w_ffn_out,
                    ln2_g, ln2_b)
```

```python
import functools
import math

import numpy as np
import jax
import jax.numpy as jnp
from jax import lax
from jax.experimental import pallas as pl
from jax.experimental.pallas import tpu as pltpu

F32 = jnp.float32
BF16 = jnp.bfloat16

D_MODEL = 1024
DEPTH = 2
GRID_W = 64
HEAD_DIM = 64
ATTN_HEADS = 8
KV_HEADS = 2
WINDOW = 128
BLOCK = 128
ROPE_BASE = 10000.0
AXIS_DIM = HEAD_DIM // 2
F_GROUPS = 4
F_DIM = 64
C_HEADS = 4
C_DIM = 64
CHUNK = 128
D_FF = 2816
ATTN_W = ATTN_HEADS * HEAD_DIM
KV_W = KV_HEADS * HEAD_DIM
F_W = F_GROUPS * F_DIM
C_W = C_HEADS * C_DIM
D_IN = ATTN_W + 2 * KV_W + F_W + 2 * C_W
COL_K = ATTN_W
COL_V = COL_K + KV_W
COL_F = COL_V + KV_W
COL_U = COL_F + 2 * F_W
COL_G = COL_U + C_W
D_PROJ = COL_G + C_W
N_COND = 2
GQA_GROUP = ATTN_HEADS // KV_HEADS
HEAD_ORDER = tuple(kv * GQA_GROUP + g for g in range(GQA_GROUP) for kv in range(KV_HEADS))
Q_SCALE = HEAD_DIM ** -0.5 * math.log2(math.e)
ALPHA = (2 * DEPTH) ** 0.25
LN_EPS = 1e-6
NEG_BIG = -1e30

LANES = 128
SUBLANES = 8
MXU_TILE = 256
FFN_CHUNKS = (0, 4 * MXU_TILE, 8 * MXU_TILE, D_FF)
DFT_B = 128
B_BLOCKS = DFT_B // SUBLANES
DFT_BLOCKS_PER_STEP = 2
VMEM_LIMIT = 56 * 1024 * 1024


def _dot(a, b):
    return jnp.dot(a, b, preferred_element_type=F32)


def _split(a):
    hi = a.astype(BF16)
    lo = (a - hi.astype(F32)).astype(BF16)
    return hi, lo


def _dot3(a_hi, a_lo, b_hi, b_lo):
    return _dot(a_hi, b_hi) + _dot(a_lo, b_hi) + _dot(a_hi, b_lo)


def _ln(x):
    mu = jnp.mean(x, axis=-1, keepdims=True)
    xc = x - mu
    var = jnp.mean(xc * xc, axis=-1, keepdims=True)
    return xc * lax.rsqrt(var + LN_EPS)


def _gelu(x):
    return 0.5 * x * (1.0 + jnp.tanh(math.sqrt(2.0 / math.pi) * (x + 0.044715 * (x * x * x))))


def _silu(x):
    return x / (1.0 + jnp.exp(-x))


def _mod_rows(mod, row):
    return [mod[row:row + 1, i * D_MODEL:(i + 1) * D_MODEL] for i in range(6)]


def _rope(t, tabs):
    cos, sin_a, sin_b = tabs
    return (t * cos + pltpu.roll(t, LANES - AXIS_DIM // 2, 1) * sin_a
            + pltpu.roll(t, AXIS_DIM // 2, 1) * sin_b)


def _rope_tile(rt_ref, ct_ref, first_grid_row, rows):
    tabs = []
    for t in range(3):
        groups = [rt_ref[t, pl.ds(first_grid_row + r, 1), :] + ct_ref[t]
                  for r in range(rows // GRID_W)]
        tabs.append(jnp.concatenate(groups, axis=0) if len(groups) > 1 else groups[0])
    return tabs


def _project(h, w_in_ref, lo, hi):
    return _dot(h, w_in_ref[:, lo:hi])


def _spatial_gate_stages(u, g, avg_ref, lng_ref, lnb_ref, wscat_ref, bs_ref):
    rows = u.shape[0]
    ug = _gelu(u)
    vg = _gelu(g)
    avg = avg_ref[...]
    v_hi, v_lo = _split(vg)
    mu = _dot(v_hi, avg) + _dot(v_lo, avg)
    yield None
    vc = vg - mu
    c_hi, c_lo = _split(vc * vc)
    var = _dot(c_hi, avg) + _dot(c_lo, avg)
    yield None
    vn = vc * lax.rsqrt(var + LN_EPS) * lng_ref[...] + lnb_ref[...]
    lane = lax.broadcasted_iota(jnp.int32, (1, C_W), 1)
    vnb = vn.astype(BF16)
    zero = jnp.zeros_like(vnb)
    outs = []
    for c in range(rows // CHUNK):
        blk = vnb[c * CHUNK:(c + 1) * CHUNK]
        rhs = jnp.concatenate(
            [jnp.where((lane >= h * C_DIM) & (lane < (h + 1) * C_DIM), blk, zero[:CHUNK])
             for h in range(C_HEADS)], axis=0)
        vs = _dot(wscat_ref[...], rhs) + bs_ref[...]
        outs.append(ug[c * CHUNK:(c + 1) * CHUNK] * vs)
    yield jnp.concatenate(outs, axis=0) if len(outs) > 1 else outs[0]


def _spatial_gate(*args):
    return list(_spatial_gate_stages(*args))[-1]


def _low_lanes():
    return lax.broadcasted_iota(jnp.int32, (1, LANES), 1) < HEAD_DIM


def _split_keys(k):
    low = _low_lanes()
    z = jnp.zeros_like(k)
    return jnp.where(low, k, z), jnp.where(low, z, k)


def _split_values(v):
    top, bot = _split_keys(v)
    low_ones = jnp.where(_low_lanes(), 1.0, 0.0)
    return (jnp.concatenate([top, jnp.broadcast_to(low_ones, v.shape).astype(BF16)], axis=1),
            jnp.concatenate([bot, jnp.broadcast_to(1.0 - low_ones, v.shape).astype(BF16)], axis=1))


def _stack_groups(q):
    return jnp.concatenate([q[:, g * LANES:(g + 1) * LANES] for g in range(ATTN_W // LANES)], axis=0)


def _sink_cols(sink_ref, rows):
    return [jnp.concatenate([jnp.broadcast_to(sink_ref[4 * kh + g:4 * kh + g + 1, 0:1], (rows, 1))
                             for g in range(4)], axis=0) for kh in range(KV_HEADS)]


def _pair_softmax(s, sinks):
    half = s.shape[1] // 2
    es, ts = [], []
    for kh in range(KV_HEADS):
        sh = s[:, kh * half:(kh + 1) * half]
        m = jnp.max(sh, axis=1, keepdims=True)
        es.append(jnp.exp2(sh - m).astype(BF16))
        ts.append(jnp.exp2(sinks[kh] - m))
    return jnp.concatenate(es, axis=1), ts


def _pair_normalize(o, ts):
    return o[:, :LANES] / (o[:, LANES:] + jnp.where(_low_lanes(), ts[0], ts[1]))


def _out_ffn_stages(x, o_cat, mod_row, w_out_ref, ln1g, ln1b, w1_ref, w2_ref, ln2g, ln2b):
    g_m, sh_f, sc_f, g_f = mod_row[2], mod_row[3], mod_row[4], mod_row[5]
    y = _dot(o_cat, w_out_ref[...])
    yield None
    x1 = _ln(ALPHA * x + g_m * y) * ln1g + ln1b
    h2 = (_ln(x1) * (1.0 + sc_f) + sh_f).astype(BF16)
    yield None
    y2 = None
    for lo, hi in zip(FFN_CHUNKS[:-1], FFN_CHUNKS[1:]):
        a = _dot(h2, w1_ref[:, lo:hi])
        b = _dot(h2, w1_ref[:, D_FF + lo:D_FF + hi])
        t = (_silu(a) * b).astype(BF16)
        part = _dot(t, w2_ref[lo:hi, :])
        y2 = part if y2 is None else y2 + part
        yield None
    yield _ln(ALPHA * x1 + g_f * y2) * ln2g + ln2b


OUT_FFN_STAGES = len(FFN_CHUNKS) + 2


def _run_skewed(gens, n_stages, skew):
    results = [None] * len(gens)
    for t in range(n_stages + skew * (len(gens) - 1)):
        for r, gen in enumerate(gens):
            if 0 <= t - r * skew < n_stages:
                results[r] = next(gen)
    return results


def _out_ffn(*args):
    return _run_skewed([_out_ffn_stages(*args)], OUT_FFN_STAGES, 0)[0]


def _layer_rows(layer, *refs):
    return [r.at[pl.ds(layer, 1)] for r in refs]


def _ada_kernel(cvt_ref, w_ref, b_ref, o_ref):
    s = _silu(cvt_ref[...])
    w = w_ref[...]
    bias = b_ref[pl.ds(pl.program_id(0), 1), :]
    o_ref[...] = jnp.zeros_like(o_ref)
    for r in range(N_COND):
        o_ref[r:r + 1, :] = jnp.sum(w * s[:, r:r + 1], axis=0, keepdims=True) + bias


def _wprep_kernel(w_in_ref, bdc_ref, wf_ref, o_ref):
    hp = functools.partial(jnp.dot, precision=lax.Precision.HIGHEST, preferred_element_type=F32)
    wf = wf_ref[...]
    fold = jnp.concatenate([hp(bdc_ref[:, :F_W], wf), hp(bdc_ref[:, F_W:], wf)], axis=1)
    for j, head in enumerate(HEAD_ORDER):
        o_ref[:, j * HEAD_DIM:(j + 1) * HEAD_DIM] = (
            w_in_ref[:, head * HEAD_DIM:(head + 1) * HEAD_DIM] * Q_SCALE).astype(BF16)
    o_ref[:, COL_K:COL_F] = w_in_ref[:, COL_K:COL_F].astype(BF16)
    o_ref[:, COL_F:COL_U] = hp(w_in_ref[:, COL_F:COL_F + F_W], fold).astype(BF16)
    o_ref[:, COL_U:] = w_in_ref[:, COL_F + F_W:].astype(BF16)


def _inproj_stages(x, sh, sc, w_ref, tabs, avg_ref, lng_ref, lnb_ref, wscat_ref, bs_ref,
                   q_ref, kd_ref, vd_ref, fx_ref, sg_ref, row0):
    rows = x.shape[0]
    rs = slice(row0, row0 + rows)
    h = (_ln(x) * (1.0 + sc) + sh).astype(BF16)
    yield
    for p in range(ATTN_W // MXU_TILE):
        qq = _project(h, w_ref, p * MXU_TILE, (p + 1) * MXU_TILE)
        for j in range(MXU_TILE // LANES):
            c0 = p * MXU_TILE + j * LANES
            q_ref[rs, c0:c0 + LANES] = _rope(qq[:, j * LANES:(j + 1) * LANES], tabs).astype(BF16)
    kv = _project(h, w_ref, COL_K, COL_F)
    kd_ref[rs, :] = _rope(kv[:, :KV_W], tabs).astype(BF16)
    vd_ref[rs, :] = kv[:, KV_W:].astype(BF16)
    yield
    f = _project(h, w_ref, COL_F, COL_U)
    for al in range(rows // DFT_B):
        for bb in range(B_BLOCKS):
            r0 = al * DFT_B + bb * SUBLANES
            a0 = (row0 // DFT_B + al) * SUBLANES
            for p in range(4):
                fx_ref[p, bb, a0:a0 + SUBLANES, :] = f[r0:r0 + SUBLANES, p * LANES:(p + 1) * LANES]
    u = _project(h, w_ref, COL_U, COL_G)
    g = _project(h, w_ref, COL_G, D_PROJ)
    gate = _spatial_gate_stages(u, g, avg_ref, lng_ref, lnb_ref, wscat_ref, bs_ref)
    next(gate)
    yield
    next(gate)
    yield
    sg_ref[rs, :] = next(gate).astype(BF16)
    yield


INPROJ_STAGES = 5


def _inproj_kernel(x_ref, mod_ref, w_ref, rt_ref, ct_ref, avg_ref, lng_ref, lnb_ref, wscat_ref,
                   bs_ref, q_ref, kd_ref, vd_ref, fx_ref, sg_ref, *, n_sub, layer):
    lng_ref, lnb_ref = _layer_rows(layer, lng_ref, lnb_ref)
    tm = x_ref.shape[0]
    sub = tm // n_sub
    sh, sc = _mod_rows(mod_ref[...], 0)[:2]
    gens = []
    for r in range(n_sub):
        grid_row = pl.program_id(0) * (tm // GRID_W) + r * (sub // GRID_W)
        gens.append(_inproj_stages(
            x_ref[r * sub:(r + 1) * sub, :], sh, sc, w_ref, _rope_tile(rt_ref, ct_ref, grid_row, sub),
            avg_ref, lng_ref, lnb_ref, wscat_ref, bs_ref, q_ref, kd_ref, vd_ref, fx_ref, sg_ref,
            r * sub))
    _run_skewed(gens, INPROJ_STAGES, 1)


def _attn_block_stages(q_ref, k_ref, v_ref, kc_ref, vc_ref, sinks, o_ref, blk, row0, n_tokens):
    span = 3 * BLOCK
    n_ctx = kc_ref.shape[1]
    half = span + n_ctx
    start = pl.multiple_of(jnp.clip((blk - 1) * BLOCK, 0, n_tokens - span), BLOCK)
    k_top, k_bot = _split_keys(k_ref[pl.ds(start, span), :])
    v_top, v_bot = _split_values(v_ref[pl.ds(start, span), :])
    kk = jnp.concatenate([k_top, kc_ref[0], k_bot, kc_ref[1]], axis=0)
    vv = jnp.concatenate([v_top, vc_ref[0], v_bot, vc_ref[1]], axis=0)
    q4 = _stack_groups(q_ref[row0:row0 + BLOCK, :])
    yield
    s = lax.dot_general(q4, kk, (((1,), (1,)), ((), ())), preferred_element_type=F32)
    yield
    qpos = blk * BLOCK + lax.broadcasted_iota(jnp.int32, (BLOCK, span), 0)
    kpos = start + lax.broadcasted_iota(jnp.int32, (BLOCK, span), 1)
    ok = jnp.concatenate([jnp.abs(kpos - qpos) <= WINDOW] * 4, axis=0)
    s = jnp.concatenate([jnp.where(ok, s[:, :span], NEG_BIG), s[:, span:half],
                         jnp.where(ok, s[:, half:half + span], NEG_BIG), s[:, half + span:]], axis=1)
    e, ts = _pair_softmax(s, sinks)
    yield
    o = _dot(e, vv)
    yield
    res = _pair_normalize(o, ts).astype(BF16)
    for g in range(ATTN_W // LANES):
        o_ref[row0:row0 + BLOCK, g * LANES:(g + 1) * LANES] = res[g * BLOCK:(g + 1) * BLOCK]
    yield


ATTN_STAGES = 5


def _attn_kernel(q_ref, k_ref, v_ref, kc_ref, vc_ref, sink_ref, o_ref, *, n_tokens):
    n_blocks = q_ref.shape[0] // BLOCK
    sinks = _sink_cols(sink_ref, BLOCK)
    gens = [_attn_block_stages(q_ref, k_ref, v_ref, kc_ref, vc_ref, sinks, o_ref,
                               pl.program_id(0) * n_blocks + r, r * BLOCK, n_tokens)
            for r in range(n_blocks)]
    _run_skewed(gens, ATTN_STAGES, 1)


def _complex_rows(parts):
    return _split(jnp.concatenate([jnp.concatenate(parts[0:2], axis=1),
                                   jnp.concatenate(parts[2:4], axis=1)], axis=0))


def _dft_stage1_stages(x_ref, m_hi_ref, m_lo_ref, tc_ref, ts_ref, y_scr, j, bl, row0):
    a = m_hi_ref.shape[0] // 2
    x_hi, x_lo = _complex_rows([x_ref[i, j, pl.ds(bl, a, stride=SUBLANES), :] for i in range(4)])
    yield
    y = _dot3(m_hi_ref[...], m_lo_ref[...], x_hi, x_lo)
    yield
    yr, yi = y[:a], y[a:]
    tc, ts = tc_ref[j, :, bl:bl + 1], ts_ref[j, :, bl:bl + 1]
    wr = yr * tc + yi * ts
    wi = yi * tc - yr * ts
    parts = (wr[:, :LANES], wr[:, LANES:], wi[:, :LANES], wi[:, LANES:])
    for kb in range(a // SUBLANES):
        for i in range(4):
            y_scr[i, kb, pl.ds(row0, SUBLANES), :] = parts[i][kb * SUBLANES:(kb + 1) * SUBLANES]
    yield


def _dft_stage2_stages(y_scr, kb, cs_hi_ref, cs_lo_ref, bf_ref, o_ref, j, kl):
    r_hi, r_lo = _complex_rows([y_scr[i, kb, pl.ds(kl, DFT_B, stride=SUBLANES), :] for i in range(4)])
    yield
    out = _dot3(cs_hi_ref[...], cs_lo_ref[...], r_hi, r_lo) + bf_ref[...]
    yield
    for hf in range(F_W // LANES):
        o_ref[hf, j, pl.ds(kl, DFT_B, stride=SUBLANES), :] = out[:, hf * LANES:(hf + 1) * LANES]
    yield


DFT_STAGES = 3


def _dft_kernel(x_ref, m_hi_ref, m_lo_ref, tc_ref, ts_ref, cs_hi_ref, cs_lo_ref, bf_ref, o_ref, y_scr,
                *, layer):
    bf_ref, = _layer_rows(layer, bf_ref)
    step = pl.program_id(0)
    bps = x_ref.shape[1]
    n1 = B_BLOCKS // bps

    @pl.when(step < n1)
    def _():
        gens = [_dft_stage1_stages(
            x_ref, m_hi_ref, m_lo_ref, tc_ref, ts_ref, y_scr, j, bl,
            pl.multiple_of(((step * bps + j) * SUBLANES + bl) * SUBLANES, SUBLANES))
            for j in range(bps) for bl in range(SUBLANES)]
        _run_skewed(gens, DFT_STAGES, 1)

    @pl.when(step >= n1)
    def _():
        gens = [_dft_stage2_stages(y_scr, (step - n1) * bps + j, cs_hi_ref, cs_lo_ref, bf_ref, o_ref, j, kl)
                for j in range(bps) for kl in range(SUBLANES)]
        _run_skewed(gens, DFT_STAGES, 1)


def _outffn_kernel(x_ref, oa_ref, of_ref, sg_ref, mod_ref, w_out_ref, ln1g_ref, ln1b_ref,
                   w1_ref, w2_ref, ln2g_ref, ln2b_ref, o_ref, *, n_sub, layer):
    ln1g_ref, ln1b_ref, ln2g_ref, ln2b_ref = _layer_rows(layer, ln1g_ref, ln1b_ref, ln2g_ref, ln2b_ref)
    n_kb, k2_rows = of_ref.shape[1], of_ref.shape[2]
    halves = []
    for hf in range(F_W // LANES):
        pieces = [of_ref[hf, kb, k2 * SUBLANES:(k2 + 1) * SUBLANES, :]
                  for k2 in range(k2_rows // SUBLANES) for kb in range(n_kb)]
        halves.append(jnp.concatenate(pieces, axis=0).astype(BF16))
    o_cat = jnp.concatenate([oa_ref[...]] + halves + [sg_ref[...]], axis=1)
    tm = x_ref.shape[0]
    sub = tm // n_sub
    mod_row = _mod_rows(mod_ref[...], 0)
    gens = [_out_ffn_stages(x_ref[r * sub:(r + 1) * sub, :], o_cat[r * sub:(r + 1) * sub], mod_row,
                            w_out_ref, ln1g_ref[...], ln1b_ref[...], w1_ref, w2_ref,
                            ln2g_ref[...], ln2b_ref[...]) for r in range(n_sub)]
    for r, res in enumerate(_run_skewed(gens, OUT_FFN_STAGES, 1)):
        o_ref[r * sub:(r + 1) * sub, :] = res


def _ctx_kernel(xc_ref, mod0_ref, mod1_ref, w_in0_ref, w_in1_ref, avg_ref,
                lng_ref, lnb_ref, wscat_ref, bs_ref, sink_ref, cs_hi_ref, cs_lo_ref, bf_ref,
                w_out_ref, ln1g_ref, ln1b_ref, w1_ref, w2_ref, ln2g_ref, ln2b_ref,
                kc0_ref, vc0_ref, kc1_ref, vc1_ref):
    lng_ref, lnb_ref, bf_ref, ln1g_ref, ln1b_ref, ln2g_ref, ln2b_ref = _layer_rows(
        0, lng_ref, lnb_ref, bf_ref, ln1g_ref, ln1b_ref, ln2g_ref, ln2b_ref)
    xc = xc_ref[...]
    n_ctx = xc.shape[0]
    m0 = _mod_rows(mod0_ref[...], 1)
    h = (_ln(xc) * (1.0 + m0[1]) + m0[0]).astype(BF16)
    q4 = _stack_groups(_project(h, w_in0_ref, 0, COL_K).astype(BF16))
    k_parts = _split_keys(_project(h, w_in0_ref, COL_K, COL_V).astype(BF16))
    v_parts = _split_values(_project(h, w_in0_ref, COL_V, COL_F).astype(BF16))
    kc0_ref[0], kc0_ref[1] = k_parts
    vc0_ref[0], vc0_ref[1] = v_parts
    kk = jnp.concatenate(k_parts, axis=0)
    vv = jnp.concatenate(v_parts, axis=0)
    s = lax.dot_general(q4, kk, (((1,), (1,)), ((), ())), preferred_element_type=F32)
    e, ts = _pair_softmax(s, _sink_cols(sink_ref, n_ctx))
    o4 = _pair_normalize(_dot(e, vv), ts)
    pairs = [o4[g * n_ctx:(g + 1) * n_ctx] for g in range(ATTN_W // LANES)]
    f = _project(h, w_in0_ref, COL_F, COL_U)
    y_hi, y_lo = _split(jnp.concatenate([f[:, :F_W], f[:, F_W:]], axis=0))
    o_f = _dot3(cs_hi_ref[...], cs_lo_ref[...], y_hi, y_lo) + bf_ref[...]
    u = _project(h, w_in0_ref, COL_U, COL_G)
    g = _project(h, w_in0_ref, COL_G, D_PROJ)
    sg = _spatial_gate(u, g, avg_ref, lng_ref, lnb_ref, wscat_ref, bs_ref)
    o_cat = jnp.concatenate([p.astype(BF16) for p in pairs] + [o_f.astype(BF16), sg.astype(BF16)],
                            axis=1)
    xc1 = _out_ffn(xc, o_cat, m0, w_out_ref, ln1g_ref[...], ln1b_ref[...], w1_ref, w2_ref,
                   ln2g_ref[...], ln2b_ref[...])
    m1 = _mod_rows(mod1_ref[...], 1)
    h1 = (_ln(xc1) * (1.0 + m1[1]) + m1[0]).astype(BF16)
    kc1_ref[0], kc1_ref[1] = _split_keys(_project(h1, w_in1_ref, COL_K, COL_V).astype(BF16))
    vc1_ref[0], vc1_ref[1] = _split_values(_project(h1, w_in1_ref, COL_V, COL_F).astype(BF16))


def _hi_lo(a):
    a32 = jnp.asarray(np.asarray(a, np.float64), F32)
    hi = a32.astype(BF16)
    return hi, (a32 - hi.astype(F32)).astype(BF16)


def _dft_cos_sin(n):
    idx = np.arange(n, dtype=np.int64)
    ang = 2.0 * np.pi * ((idx[:, None] * idx[None, :]) % n) / n
    return np.cos(ang), np.sin(ang)


def _channel_dft_tables():
    c, s = _dft_cos_sin(F_DIM)
    bd = np.zeros((F_W, 2 * F_W))
    for g in range(F_GROUPS):
        sl = slice(g * F_DIM, (g + 1) * F_DIM)
        bd[sl, sl] = c / math.sqrt(F_DIM)
        bd[sl, F_W + g * F_DIM:F_W + (g + 1) * F_DIM] = -s / math.sqrt(F_DIM)
    return jnp.asarray(bd, F32)


def _group_avg_table():
    a = np.zeros((C_W, C_W))
    for h in range(C_HEADS):
        a[h * C_DIM:(h + 1) * C_DIM, h * C_DIM:(h + 1) * C_DIM] = 1.0 / C_DIM
    return jnp.asarray(a, BF16)


def _rope_tables(n):
    freqs = jnp.asarray(ROPE_BASE, F32) ** (-jnp.arange(0, AXIS_DIM, 2, dtype=F32) / AXIS_DIM)
    reps = LANES // HEAD_DIM

    def tables(pos, row_axis):
        ang = pos[:, None] * freqs
        c, s = jnp.cos(ang), jnp.sin(ang)
        z = jnp.zeros_like(s)
        pad = [z, z]
        pick = (lambda t: t + pad) if row_axis else (lambda t: pad + t)
        return jnp.stack([jnp.tile(jnp.concatenate(pick(t), axis=1), (1, reps))
                          for t in ([c, c], [-s, z], [z, s])])

    rt = tables(jnp.arange(n // GRID_W, dtype=F32), True)
    ct = tables(jnp.arange(GRID_W, dtype=F32), False)
    return rt, ct


def _stage1_tables(a):
    n = a * DFT_B
    c, s = _dft_cos_sin(a)
    m_hi, m_lo = _hi_lo(np.block([[c, s], [-s, c]]) / math.sqrt(a))
    k1 = np.arange(a, dtype=np.int64)[None, :, None]
    b = (np.arange(B_BLOCKS, dtype=np.int64)[:, None, None] * SUBLANES
         + np.arange(SUBLANES, dtype=np.int64)[None, None, :])
    ang = 2.0 * np.pi * ((k1 * b) % n) / n
    return m_hi, m_lo, jnp.asarray(np.cos(ang), F32), jnp.asarray(np.sin(ang), F32)


def _stage2_tables():
    c, s = _dft_cos_sin(DFT_B)
    return _hi_lo(np.concatenate([c, s], axis=1) / math.sqrt(DFT_B))


def _ctx_dft_tables(n_ctx):
    c, s = _dft_cos_sin(n_ctx)
    return _hi_lo(np.concatenate([c, s], axis=1) / math.sqrt(n_ctx))


def _const_spec(shape):
    nd = len(shape)
    return pl.BlockSpec(shape, lambda *_: (0,) * nd, pipeline_mode=pl.Buffered(1))


def _layer_spec(shape, layer):
    if len(shape) == 2:
        return _const_spec(shape)
    nd = len(shape) - 1
    return pl.BlockSpec((None,) + tuple(shape[1:]), lambda *_: (layer,) + (0,) * nd,
                        pipeline_mode=pl.Buffered(1))


def _params(*sem):
    return pltpu.CompilerParams(dimension_semantics=sem, vmem_limit_bytes=VMEM_LIMIT)


def _ada(cvt, w_ada, b_ada):
    tn = 1536
    return pl.pallas_call(
        _ada_kernel,
        grid=(DEPTH, 6 * D_MODEL // tn),
        in_specs=[pl.BlockSpec((D_MODEL, SUBLANES), lambda l, j: (0, 0)),
                  pl.BlockSpec((None, D_MODEL, tn), lambda l, j: (l, 0, j)),
                  pl.BlockSpec((DEPTH, tn), lambda l, j: (0, j))],
        out_specs=pl.BlockSpec((None, 8, tn), lambda l, j: (l, 0, j)),
        out_shape=jax.ShapeDtypeStruct((DEPTH, 8, 6 * D_MODEL), F32),
        compiler_params=_params("arbitrary", "arbitrary"),
        name="ada",
    )(cvt, w_ada, b_ada)


def _wprep(w_in, bdc, wf_bd):
    depth = w_in.shape[0]
    layer = lambda shape: pl.BlockSpec((None,) + tuple(shape[1:]), lambda l: (l, 0, 0))
    return pl.pallas_call(
        _wprep_kernel,
        grid=(depth,),
        in_specs=[layer(w_in.shape), pl.BlockSpec(bdc.shape, lambda l: (0, 0)), layer(wf_bd.shape)],
        out_specs=layer((depth, D_MODEL, D_PROJ)),
        out_shape=jax.ShapeDtypeStruct((depth, D_MODEL, D_PROJ), BF16),
        compiler_params=_params("arbitrary"),
        name="wprep",
    )(w_in, bdc, wf_bd)


def _inproj(x, mods, w_in, layer, rope, shared, sgu, tm, n_sub):
    n = x.shape[0]
    a = n // DFT_B
    row = lambda w: pl.BlockSpec((tm, w), lambda i: (i, 0))
    consts = list(rope) + list(shared)
    fx_rows = tm // DFT_B * SUBLANES
    return pl.pallas_call(
        functools.partial(_inproj_kernel, n_sub=n_sub, layer=layer),
        grid=(n // tm,),
        in_specs=[row(D_MODEL), _layer_spec(mods.shape, layer), _layer_spec(w_in.shape, layer)]
        + [_const_spec(c.shape) for c in consts] + [_layer_spec(c.shape, layer) for c in sgu],
        out_specs=[row(ATTN_W), row(KV_W), row(KV_W),
                   pl.BlockSpec((4, B_BLOCKS, fx_rows, LANES), lambda i: (0, 0, i, 0)), row(C_W)],
        out_shape=[jax.ShapeDtypeStruct((n, ATTN_W), BF16), jax.ShapeDtypeStruct((n, KV_W), BF16),
                   jax.ShapeDtypeStruct((n, KV_W), BF16),
                   jax.ShapeDtypeStruct((4, B_BLOCKS, a * SUBLANES, LANES), F32),
                   jax.ShapeDtypeStruct((n, C_W), BF16)],
        compiler_params=_params("arbitrary"),
        name="inproj",
    )(x, mods, w_in, *consts, *sgu)


def _attention(q, kd, vd, kc, vc, sink_tab, layer, tq):
    n = q.shape[0]
    return pl.pallas_call(
        functools.partial(_attn_kernel, n_tokens=n),
        grid=(n // tq,),
        in_specs=[pl.BlockSpec((tq, ATTN_W), lambda i: (i, 0)), _const_spec(kd.shape),
                  _const_spec(vd.shape), _const_spec(kc.shape), _const_spec(vc.shape),
                  _layer_spec(sink_tab.shape, layer)],
        out_specs=pl.BlockSpec((tq, ATTN_W), lambda i: (i, 0)),
        out_shape=jax.ShapeDtypeStruct((n, ATTN_W), BF16),
        compiler_params=_params("arbitrary"),
        name="attention",
    )(q, kd, vd, kc, vc, sink_tab)


def _fourier(fx, tabs1, tabs2, bf_row, layer):
    a = fx.shape[2] // SUBLANES
    kb = a // SUBLANES
    m_hi, m_lo, tc, ts = tabs1
    cs_hi, cs_lo = tabs2
    bps = math.gcd(DFT_BLOCKS_PER_STEP, kb)
    n1 = B_BLOCKS // bps
    in_blk = lambda s: jnp.minimum(s, n1 - 1)
    out_blk = lambda s: jnp.maximum(s - n1, 0)
    tw = pl.BlockSpec((bps, a, SUBLANES), lambda s: (in_blk(s), 0, 0))
    return pl.pallas_call(
        functools.partial(_dft_kernel, layer=layer),
        grid=(n1 + kb // bps,),
        in_specs=[pl.BlockSpec((4, bps, a * SUBLANES, LANES), lambda s: (0, in_blk(s), 0, 0)),
                  _const_spec(m_hi.shape), _const_spec(m_lo.shape), tw, tw,
                  _const_spec(cs_hi.shape), _const_spec(cs_lo.shape),
                  _layer_spec(bf_row.shape, layer)],
        out_specs=pl.BlockSpec((F_W // LANES, bps, DFT_B * SUBLANES, LANES),
                               lambda s: (0, out_blk(s), 0, 0)),
        out_shape=jax.ShapeDtypeStruct((F_W // LANES, kb, DFT_B * SUBLANES, LANES), F32),
        scratch_shapes=[pltpu.VMEM((4, kb, DFT_B * SUBLANES, LANES), F32)],
        compiler_params=_params("arbitrary"),
        name="dft",
    )(fx, m_hi, m_lo, tc, ts, cs_hi, cs_lo, bf_row)


def _outffn(x, oa, of, sg, mods, weights, layer, tm, n_sub):
    n = x.shape[0]
    a = n // DFT_B
    row = lambda w: pl.BlockSpec((tm, w), lambda i: (i, 0))
    of_spec = pl.BlockSpec((F_W // LANES, a // SUBLANES, tm // a * SUBLANES, LANES),
                           lambda i: (0, 0, i, 0))
    consts = [mods] + list(weights)
    return pl.pallas_call(
        functools.partial(_outffn_kernel, n_sub=n_sub, layer=layer),
        grid=(n // tm,),
        in_specs=[row(D_MODEL), row(ATTN_W), of_spec, row(C_W)]
        + [_layer_spec(c.shape, layer) for c in consts],
        out_specs=row(D_MODEL),
        out_shape=jax.ShapeDtypeStruct((n, D_MODEL), F32),
        compiler_params=_params("arbitrary"),
        name="outffn",
    )(x, oa, of, sg, *consts)


def _context(xc, mods, w_in, shared, sgu, sink_tab, ctx_tabs, bf_row, weights):
    n_ctx = xc.shape[0]
    args = ([(xc, None), (mods, 0), (mods, 1), (w_in, 0), (w_in, 1)] + [(t, None) for t in shared]
            + [(t, 0) for t in sgu] + [(sink_tab, 0)] + [(t, None) for t in ctx_tabs]
            + [(bf_row, 0)] + [(t, 0) for t in weights])
    outs = [jax.ShapeDtypeStruct((KV_HEADS, n_ctx, w), BF16) for w in (LANES, 2 * LANES)] * 2
    return pl.pallas_call(
        _ctx_kernel,
        grid=(1,),
        in_specs=[_const_spec(t.shape) if l is None else _layer_spec(t.shape, l) for t, l in args],
        out_specs=[pl.BlockSpec(o.shape, lambda i: (0, 0, 0)) for o in outs],
        out_shape=outs,
        compiler_params=_params("arbitrary"),
        name="context",
    )(*[t for t, _ in args])


def _block_diag_wf(w_f):
    bd = jnp.zeros((w_f.shape[0], F_W, F_W), F32)
    for g in range(F_GROUPS):
        bd = bd.at[:, g * F_DIM:(g + 1) * F_DIM, g * F_DIM:(g + 1) * F_DIM].set(w_f[:, g])
    return bd


def _forward(x, c, ctx, c_ctx, w_ada, b_ada, w_in, w_out, attn_sink, w_fourier, b_fourier,
             sgu_ln_g, sgu_ln_b, w_spatial, b_spatial, ln1_g, ln1_b, w_ffn_in, w_ffn_out,
             ln2_g, ln2_b, tm_in=2048, n_sub_in=8, tq=1024, tm_out=1024, n_sub=4):
    n = x.shape[1]
    n_ctx = ctx.shape[1]
    depth = w_in.shape[0]
    assert x.shape[0] == 1 and n % max(tm_in, tq, tm_out) == 0 and n >= 3 * BLOCK
    assert (n // DFT_B) % SUBLANES == 0 and tm_out % (n // DFT_B) == 0 and n_ctx % CHUNK == 0
    xs = x[0]
    cvt = jnp.concatenate([c[0][:, None], c_ctx[:, None],
                           jnp.zeros((D_MODEL, SUBLANES - N_COND), F32)], axis=1)
    mods = _ada(cvt, w_ada, b_ada)

    w_in_b = _wprep(w_in, _channel_dft_tables(), _block_diag_wf(w_fourier))
    w_out_p = jnp.concatenate([w_out[:, h * HEAD_DIM:(h + 1) * HEAD_DIM, :] for h in HEAD_ORDER]
                              + [w_out[:, ATTN_W:, :]], axis=1)
    weights = (w_out_p.astype(BF16), ln1_g, ln1_b, w_ffn_in.astype(BF16),
               w_ffn_out.astype(BF16), ln2_g, ln2_b)
    shared = (_group_avg_table(),)
    rope = _rope_tables(n)
    tabs1 = _stage1_tables(n // DFT_B)
    tabs2 = _stage2_tables()
    ctx_tabs = _ctx_dft_tables(n_ctx)
    sgu = (sgu_ln_g.reshape(depth, C_W), sgu_ln_b.reshape(depth, C_W),
           jnp.concatenate([w_spatial[:, h] for h in range(C_HEADS)], axis=2).astype(BF16),
           jnp.repeat(jnp.swapaxes(b_spatial, 1, 2), C_DIM, axis=2))
    sink_tab = jnp.broadcast_to((attn_sink * math.log2(math.e))[:, :, None], (depth, ATTN_HEADS, LANES))
    bf_row = b_fourier.reshape(depth, F_W)

    ctx_kv = _context(ctx[0], mods, w_in_b, shared, sgu, sink_tab, ctx_tabs, bf_row, weights)

    for l in range(depth):
        q, kd, vd, fx, sg = _inproj(xs, mods, w_in_b, l, rope, shared, sgu, tm_in, n_sub_in)
        oa = _attention(q, kd, vd, ctx_kv[2 * l], ctx_kv[2 * l + 1], sink_tab, l, tq)
        of = _fourier(fx, tabs1, tabs2, bf_row, l)
        xs = _outffn(xs, oa, of, sg, mods, weights, l, tm_out, n_sub)
    return xs[None]


def kernel(x, c, ctx, c_ctx, w_ada, b_ada, w_in, w_out, attn_sink, w_fourier, b_fourier, sgu_ln_g,
           sgu_ln_b, w_spatial, b_spatial, ln1_g, ln1_b, w_ffn_in, w_ffn_out, ln2_g, ln2_b):
    return _forward(x, c, ctx, c_ctx, w_ada, b_ada, w_in, w_out, attn_sink, w_fourier, b_fourier,
                    sgu_ln_g, sgu_ln_b, w_spatial, b_spatial, ln1_g, ln1_b, w_ffn_in, w_ffn_out,
                    ln2_g, ln2_b)
```

```python
import functools
import math

import numpy as np
import jax
import jax.numpy as jnp
from jax import lax
from jax.experimental import pallas as pl
from jax.experimental.pallas import tpu as pltpu

F32 = jnp.float32
BF16 = jnp.bfloat16

D_MODEL = 1024
DEPTH = 2
GRID_W = 64
HEAD_DIM = 64
ATTN_HEADS = 8
KV_HEADS = 2
WINDOW = 128
BLOCK = 128
ROPE_BASE = 10000.0
AXIS_DIM = HEAD_DIM // 2
F_GROUPS = 4
F_DIM = 64
C_HEADS = 4
C_DIM = 64
CHUNK = 128
D_FF = 2816
ATTN_W = ATTN_HEADS * HEAD_DIM
KV_W = KV_HEADS * HEAD_DIM
F_W = F_GROUPS * F_DIM
C_W = C_HEADS * C_DIM
D_IN = ATTN_W + 2 * KV_W + F_W + 2 * C_W
COL_K = ATTN_W
COL_V = COL_K + KV_W
COL_F = COL_V + KV_W
COL_U = COL_F + 2 * F_W
COL_G = COL_U + C_W
D_PROJ = COL_G + C_W
N_COND = 2
GQA_GROUP = ATTN_HEADS // KV_HEADS
HEAD_ORDER = tuple(kv * GQA_GROUP + g for g in range(GQA_GROUP) for kv in range(KV_HEADS))
Q_SCALE = HEAD_DIM ** -0.5 * math.log2(math.e)
ALPHA = (2 * DEPTH) ** 0.25
LN_EPS = 1e-6
NEG_BIG = -1e30

LANES = 128
SUBLANES = 8
MXU_TILE = 256
FFN_CHUNKS = (0, 4 * MXU_TILE, 8 * MXU_TILE, D_FF)
PREP_STEPS = 12
FFN_CAST_COLS = 2 * MXU_TILE
FFN_CAST_STEPS = 2 * D_FF // FFN_CAST_COLS
DFT_B = 128
B_BLOCKS = DFT_B // SUBLANES
DFT_BLOCKS_PER_STEP = 2
VMEM_LIMIT = 56 * 1024 * 1024


def _dot(a, b):
    return jnp.dot(a, b, preferred_element_type=F32)


def _split(a):
    hi = a.astype(BF16)
    lo = (a - hi.astype(F32)).astype(BF16)
    return hi, lo


def _dot3(a_hi, a_lo, b_hi, b_lo):
    return _dot(a_hi, b_hi) + _dot(a_lo, b_hi) + _dot(a_hi, b_lo)


def _ln(x):
    mu = jnp.mean(x, axis=-1, keepdims=True)
    xc = x - mu
    var = jnp.mean(xc * xc, axis=-1, keepdims=True)
    return xc * lax.rsqrt(var + LN_EPS)


def _gelu(x):
    return 0.5 * x * (1.0 + jnp.tanh(math.sqrt(2.0 / math.pi) * (x + 0.044715 * (x * x * x))))


def _silu(x):
    return x / (1.0 + jnp.exp(-x))


def _mod_rows(mod, row):
    return [mod[row:row + 1, i * D_MODEL:(i + 1) * D_MODEL] for i in range(6)]


def _rope(t, tabs):
    cos, sin_a, sin_b = tabs
    return (t * cos + pltpu.roll(t, LANES - AXIS_DIM // 2, 1) * sin_a
            + pltpu.roll(t, AXIS_DIM // 2, 1) * sin_b)


def _rope_tile(rt_ref, ct_ref, first_grid_row, rows):
    tabs = []
    for t in range(3):
        groups = [rt_ref[t, pl.ds(first_grid_row + r, 1), :] + ct_ref[t]
                  for r in range(rows // GRID_W)]
        tabs.append(jnp.concatenate(groups, axis=0) if len(groups) > 1 else groups[0])
    return tabs


def _project(h, w_in_ref, lo, hi):
    return _dot(h, w_in_ref[:, lo:hi])


def _spatial_gate_stages(u, g, avg_ref, lng_ref, lnb_ref, wscat_ref, bs_ref):
    rows = u.shape[0]
    ug = _gelu(u)
    vg = _gelu(g)
    avg = avg_ref[...]
    v_hi, v_lo = _split(vg)
    mu = _dot(v_hi, avg) + _dot(v_lo, avg)
    yield None
    vc = vg - mu
    c_hi, c_lo = _split(vc * vc)
    var = _dot(c_hi, avg) + _dot(c_lo, avg)
    yield None
    vn = vc * lax.rsqrt(var + LN_EPS) * lng_ref[...] + lnb_ref[...]
    lane = lax.broadcasted_iota(jnp.int32, (1, C_W), 1)
    vnb = vn.astype(BF16)
    zero = jnp.zeros_like(vnb)
    outs = []
    for c in range(rows // CHUNK):
        blk = vnb[c * CHUNK:(c + 1) * CHUNK]
        rhs = jnp.concatenate(
            [jnp.where((lane >= h * C_DIM) & (lane < (h + 1) * C_DIM), blk, zero[:CHUNK])
             for h in range(C_HEADS)], axis=0)
        vs = _dot(wscat_ref[...], rhs) + bs_ref[...]
        outs.append(ug[c * CHUNK:(c + 1) * CHUNK] * vs)
    yield jnp.concatenate(outs, axis=0) if len(outs) > 1 else outs[0]


def _spatial_gate(*args):
    return list(_spatial_gate_stages(*args))[-1]


def _low_lanes():
    return lax.broadcasted_iota(jnp.int32, (1, LANES), 1) < HEAD_DIM


def _split_keys(k):
    low = _low_lanes()
    z = jnp.zeros_like(k)
    return jnp.where(low, k, z), jnp.where(low, z, k)


def _split_values(v):
    top, bot = _split_keys(v)
    low_ones = jnp.where(_low_lanes(), 1.0, 0.0)
    return (jnp.concatenate([top, jnp.broadcast_to(low_ones, v.shape).astype(BF16)], axis=1),
            jnp.concatenate([bot, jnp.broadcast_to(1.0 - low_ones, v.shape).astype(BF16)], axis=1))


def _stack_groups(q):
    return jnp.concatenate([q[:, g * LANES:(g + 1) * LANES] for g in range(ATTN_W // LANES)], axis=0)


def _sink_cols(sink_ref, rows):
    return [jnp.concatenate([jnp.broadcast_to(sink_ref[4 * kh + g:4 * kh + g + 1, 0:1], (rows, 1))
                             for g in range(4)], axis=0) for kh in range(KV_HEADS)]


def _pair_softmax(s, sinks):
    half = s.shape[1] // 2
    es, ts = [], []
    for kh in range(KV_HEADS):
        sh = s[:, kh * half:(kh + 1) * half]
        m = jnp.max(sh, axis=1, keepdims=True)
        es.append(jnp.exp2(sh - m).astype(BF16))
        ts.append(jnp.exp2(sinks[kh] - m))
    return jnp.concatenate(es, axis=1), ts


def _pair_normalize(o, ts):
    return o[:, :LANES] / (o[:, LANES:] + jnp.where(_low_lanes(), ts[0], ts[1]))


def _out_ffn_stages(x, o_cat, mod_row, w_out_ref, ln1g, ln1b, w1_ref, w2_ref, ln2g, ln2b):
    g_m, sh_f, sc_f, g_f = mod_row[2], mod_row[3], mod_row[4], mod_row[5]
    y = _dot(o_cat, w_out_ref[...])
    yield None
    x1 = _ln(ALPHA * x + g_m * y) * ln1g + ln1b
    h2 = (_ln(x1) * (1.0 + sc_f) + sh_f).astype(BF16)
    yield None
    y2 = None
    for lo, hi in zip(FFN_CHUNKS[:-1], FFN_CHUNKS[1:]):
        a = _dot(h2, w1_ref[:, lo:hi])
        b = _dot(h2, w1_ref[:, D_FF + lo:D_FF + hi])
        t = (_silu(a) * b).astype(BF16)
        part = _dot(t, w2_ref[lo:hi, :])
        y2 = part if y2 is None else y2 + part
        yield None
    yield _ln(ALPHA * x1 + g_f * y2) * ln2g + ln2b


OUT_FFN_STAGES = len(FFN_CHUNKS) + 2


def _run_skewed(gens, n_stages, skew):
    results = [None] * len(gens)
    for t in range(n_stages + skew * (len(gens) - 1)):
        for r, gen in enumerate(gens):
            if 0 <= t - r * skew < n_stages:
                results[r] = next(gen)
    return results


def _out_ffn(*args):
    return _run_skewed([_out_ffn_stages(*args)], OUT_FFN_STAGES, 0)[0]


def _layer_rows(layer, *refs):
    return [r.at[pl.ds(layer, 1)] for r in refs]


def _ada_kernel(cvt_ref, w_ref, b_ref, o_ref):
    s = _silu(cvt_ref[...])
    w = w_ref[...]
    bias = b_ref[pl.ds(pl.program_id(0), 1), :]
    o_ref[...] = jnp.zeros_like(o_ref)
    for r in range(N_COND):
        o_ref[r:r + 1, :] = jnp.sum(w * s[:, r:r + 1], axis=0, keepdims=True) + bias


def _wprep_kernel(w_in_ref, bdc_ref, wf_ref, o_ref):
    hp = functools.partial(jnp.dot, precision=lax.Precision.HIGHEST, preferred_element_type=F32)
    wf = wf_ref[...]
    fold = jnp.concatenate([hp(bdc_ref[:, :F_W], wf), hp(bdc_ref[:, F_W:], wf)], axis=1)
    for j, head in enumerate(HEAD_ORDER):
        o_ref[:, j * HEAD_DIM:(j + 1) * HEAD_DIM] = (
            w_in_ref[:, head * HEAD_DIM:(head + 1) * HEAD_DIM] * Q_SCALE).astype(BF16)
    o_ref[:, COL_K:COL_F] = w_in_ref[:, COL_K:COL_F].astype(BF16)
    o_ref[:, COL_F:COL_U] = hp(w_in_ref[:, COL_F:COL_F + F_W], fold).astype(BF16)
    o_ref[:, COL_U:] = w_in_ref[:, COL_F + F_W:].astype(BF16)


def _prep_kernel(cvt_ref, w_ada_ref, b_ada_ref, w_in_ref, bdc_ref, wf_ref, w1_ref, w2_ref,
                 mods_ref, wcat_ref, w1b_ref, w2b_ref):
    j = pl.program_id(1)
    _ada_kernel(cvt_ref, w_ada_ref, b_ada_ref, mods_ref)

    @pl.when(j == 0)
    def _():
        _wprep_kernel(w_in_ref, bdc_ref, wf_ref, wcat_ref)

    @pl.when(j < FFN_CAST_STEPS)
    def _():
        w1b_ref[...] = w1_ref[...].astype(BF16)
        w2b_ref[...] = w2_ref[...].astype(BF16)


def _inproj_stages(x, sh, sc, w_ref, tabs, avg_ref, lng_ref, lnb_ref, wscat_ref, bs_ref,
                   q_ref, kd_ref, vd_ref, fx_ref, sg_ref, row0):
    rows = x.shape[0]
    rs = slice(row0, row0 + rows)
    h = (_ln(x) * (1.0 + sc) + sh).astype(BF16)
    yield
    for p in range(ATTN_W // MXU_TILE):
        qq = _project(h, w_ref, p * MXU_TILE, (p + 1) * MXU_TILE)
        for j in range(MXU_TILE // LANES):
            c0 = p * MXU_TILE + j * LANES
            q_ref[rs, c0:c0 + LANES] = _rope(qq[:, j * LANES:(j + 1) * LANES], tabs).astype(BF16)
    kv = _project(h, w_ref, COL_K, COL_F)
    kd_ref[rs, :] = _rope(kv[:, :KV_W], tabs).astype(BF16)
    vd_ref[rs, :] = kv[:, KV_W:].astype(BF16)
    yield
    f = _project(h, w_ref, COL_F, COL_U)
    for al in range(rows // DFT_B):
        for bb in range(B_BLOCKS):
            r0 = al * DFT_B + bb * SUBLANES
            a0 = (row0 // DFT_B + al) * SUBLANES
            for p in range(4):
                fx_ref[p, bb, a0:a0 + SUBLANES, :] = f[r0:r0 + SUBLANES, p * LANES:(p + 1) * LANES]
    u = _project(h, w_ref, COL_U, COL_G)
    g = _project(h, w_ref, COL_G, D_PROJ)
    gate = _spatial_gate_stages(u, g, avg_ref, lng_ref, lnb_ref, wscat_ref, bs_ref)
    next(gate)
    yield
    next(gate)
    yield
    sg_ref[rs, :] = next(gate).astype(BF16)
    yield


INPROJ_STAGES = 5


def _inproj_kernel(x_ref, mod_ref, w_ref, rt_ref, ct_ref, avg_ref, lng_ref, lnb_ref, wscat_ref,
                   bs_ref, q_ref, kd_ref, vd_ref, fx_ref, sg_ref, *, n_sub, layer):
    lng_ref, lnb_ref = _layer_rows(layer, lng_ref, lnb_ref)
    tm = x_ref.shape[0]
    sub = tm // n_sub
    sh, sc = _mod_rows(mod_ref[...], 0)[:2]
    gens = []
    for r in range(n_sub):
        grid_row = pl.program_id(0) * (tm // GRID_W) + r * (sub // GRID_W)
        gens.append(_inproj_stages(
            x_ref[r * sub:(r + 1) * sub, :], sh, sc, w_ref, _rope_tile(rt_ref, ct_ref, grid_row, sub),
            avg_ref, lng_ref, lnb_ref, wscat_ref, bs_ref, q_ref, kd_ref, vd_ref, fx_ref, sg_ref,
            r * sub))
    _run_skewed(gens, INPROJ_STAGES, 1)


def _attn_block_stages(q_ref, k_ref, v_ref, kc_ref, vc_ref, sinks, o_ref, blk, row0, n_tokens):
    span = 3 * BLOCK
    n_ctx = kc_ref.shape[1]
    half = span + n_ctx
    start = pl.multiple_of(jnp.clip((blk - 1) * BLOCK, 0, n_tokens - span), BLOCK)
    k_top, k_bot = _split_keys(k_ref[pl.ds(start, span), :])
    v_top, v_bot = _split_values(v_ref[pl.ds(start, span), :])
    kk = jnp.concatenate([k_top, kc_ref[0], k_bot, kc_ref[1]], axis=0)
    vv = jnp.concatenate([v_top, vc_ref[0], v_bot, vc_ref[1]], axis=0)
    q4 = _stack_groups(q_ref[row0:row0 + BLOCK, :])
    yield
    s = lax.dot_general(q4, kk, (((1,), (1,)), ((), ())), preferred_element_type=F32)
    yield
    qpos = blk * BLOCK + lax.broadcasted_iota(jnp.int32, (BLOCK, span), 0)
    kpos = start + lax.broadcasted_iota(jnp.int32, (BLOCK, span), 1)
    ok = jnp.concatenate([jnp.abs(kpos - qpos) <= WINDOW] * 4, axis=0)
    s = jnp.concatenate([jnp.where(ok, s[:, :span], NEG_BIG), s[:, span:half],
                         jnp.where(ok, s[:, half:half + span], NEG_BIG), s[:, half + span:]], axis=1)
    e, ts = _pair_softmax(s, sinks)
    yield
    o = _dot(e, vv)
    yield
    res = _pair_normalize(o, ts).astype(BF16)
    for g in range(ATTN_W // LANES):
        o_ref[row0:row0 + BLOCK, g * LANES:(g + 1) * LANES] = res[g * BLOCK:(g + 1) * BLOCK]
    yield


ATTN_STAGES = 5


def _attn_kernel(q_ref, k_ref, v_ref, kc_ref, vc_ref, sink_ref, o_ref, *, n_tokens):
    n_blocks = q_ref.shape[0] // BLOCK
    sinks = _sink_cols(sink_ref, BLOCK)
    gens = [_attn_block_stages(q_ref, k_ref, v_ref, kc_ref, vc_ref, sinks, o_ref,
                               pl.program_id(0) * n_blocks + r, r * BLOCK, n_tokens)
            for r in range(n_blocks)]
    _run_skewed(gens, ATTN_STAGES, 1)


def _complex_rows(parts):
    return _split(jnp.concatenate([jnp.concatenate(parts[0:2], axis=1),
                                   jnp.concatenate(parts[2:4], axis=1)], axis=0))


def _dft_stage1_stages(x_ref, m_hi_ref, m_lo_ref, tc_ref, ts_ref, y_scr, j, bl, row0):
    a = m_hi_ref.shape[0] // 2
    x_hi, x_lo = _complex_rows([x_ref[i, j, pl.ds(bl, a, stride=SUBLANES), :] for i in range(4)])
    yield
    y = _dot3(m_hi_ref[...], m_lo_ref[...], x_hi, x_lo)
    yield
    yr, yi = y[:a], y[a:]
    tc, ts = tc_ref[j, :, bl:bl + 1], ts_ref[j, :, bl:bl + 1]
    wr = yr * tc + yi * ts
    wi = yi * tc - yr * ts
    parts = (wr[:, :LANES], wr[:, LANES:], wi[:, :LANES], wi[:, LANES:])
    for kb in range(a // SUBLANES):
        for i in range(4):
            y_scr[i, kb, pl.ds(row0, SUBLANES), :] = parts[i][kb * SUBLANES:(kb + 1) * SUBLANES]
    yield


def _dft_stage2_stages(y_scr, kb, cs_hi_ref, cs_lo_ref, bf_ref, o_ref, j, kl):
    r_hi, r_lo = _complex_rows([y_scr[i, kb, pl.ds(kl, DFT_B, stride=SUBLANES), :] for i in range(4)])
    yield
    out = _dot3(cs_hi_ref[...], cs_lo_ref[...], r_hi, r_lo) + bf_ref[...]
    yield
    for hf in range(F_W // LANES):
        o_ref[hf, j, pl.ds(kl, DFT_B, stride=SUBLANES), :] = out[:, hf * LANES:(hf + 1) * LANES]
    yield


DFT_STAGES = 3


def _dft_kernel(x_ref, m_hi_ref, m_lo_ref, tc_ref, ts_ref, cs_hi_ref, cs_lo_ref, bf_ref, o_ref, y_scr,
                *, layer):
    bf_ref, = _layer_rows(layer, bf_ref)
    step = pl.program_id(0)
    bps = x_ref.shape[1]
    n1 = B_BLOCKS // bps

    @pl.when(step < n1)
    def _():
        gens = [_dft_stage1_stages(
            x_ref, m_hi_ref, m_lo_ref, tc_ref, ts_ref, y_scr, j, bl,
            pl.multiple_of(((step * bps + j) * SUBLANES + bl) * SUBLANES, SUBLANES))
            for j in range(bps) for bl in range(SUBLANES)]
        _run_skewed(gens, DFT_STAGES, 1)

    @pl.when(step >= n1)
    def _():
        gens = [_dft_stage2_stages(y_scr, (step - n1) * bps + j, cs_hi_ref, cs_lo_ref, bf_ref, o_ref, j, kl)
                for j in range(bps) for kl in range(SUBLANES)]
        _run_skewed(gens, DFT_STAGES, 1)


def _outffn_kernel(x_ref, oa_ref, of_ref, sg_ref, mod_ref, w_out_ref, ln1g_ref, ln1b_ref,
                   w1_ref, w2_ref, ln2g_ref, ln2b_ref, o_ref, *, n_sub, layer):
    ln1g_ref, ln1b_ref, ln2g_ref, ln2b_ref = _layer_rows(layer, ln1g_ref, ln1b_ref, ln2g_ref, ln2b_ref)
    n_kb, k2_rows = of_ref.shape[1], of_ref.shape[2]
    halves = []
    for hf in range(F_W // LANES):
        pieces = [of_ref[hf, kb, k2 * SUBLANES:(k2 + 1) * SUBLANES, :]
                  for k2 in range(k2_rows // SUBLANES) for kb in range(n_kb)]
        halves.append(jnp.concatenate(pieces, axis=0).astype(BF16))
    o_cat = jnp.concatenate([oa_ref[...]] + halves + [sg_ref[...]], axis=1)
    tm = x_ref.shape[0]
    sub = tm // n_sub
    mod_row = _mod_rows(mod_ref[...], 0)
    gens = [_out_ffn_stages(x_ref[r * sub:(r + 1) * sub, :], o_cat[r * sub:(r + 1) * sub], mod_row,
                            w_out_ref, ln1g_ref[...], ln1b_ref[...], w1_ref, w2_ref,
                            ln2g_ref[...], ln2b_ref[...]) for r in range(n_sub)]
    for r, res in enumerate(_run_skewed(gens, OUT_FFN_STAGES, 1)):
        o_ref[r * sub:(r + 1) * sub, :] = res


def _ctx_kernel(xc_ref, mod0_ref, mod1_ref, w_in0_ref, w_in1_ref, avg_ref,
                lng_ref, lnb_ref, wscat_ref, bs_ref, sink_ref, cs_hi_ref, cs_lo_ref, bf_ref,
                w_out_ref, ln1g_ref, ln1b_ref, w1_ref, w2_ref, ln2g_ref, ln2b_ref,
                kc0_ref, vc0_ref, kc1_ref, vc1_ref):
    lng_ref, lnb_ref, bf_ref, ln1g_ref, ln1b_ref, ln2g_ref, ln2b_ref = _layer_rows(
        0, lng_ref, lnb_ref, bf_ref, ln1g_ref, ln1b_ref, ln2g_ref, ln2b_ref)
    xc = xc_ref[...]
    n_ctx = xc.shape[0]
    m0 = _mod_rows(mod0_ref[...], 1)
    h = (_ln(xc) * (1.0 + m0[1]) + m0[0]).astype(BF16)
    q4 = _stack_groups(_project(h, w_in0_ref, 0, COL_K).astype(BF16))
    k_parts = _split_keys(_project(h, w_in0_ref, COL_K, COL_V).astype(BF16))
    v_parts = _split_values(_project(h, w_in0_ref, COL_V, COL_F).astype(BF16))
    kc0_ref[0], kc0_ref[1] = k_parts
    vc0_ref[0], vc0_ref[1] = v_parts
    kk = jnp.concatenate(k_parts, axis=0)
    vv = jnp.concatenate(v_parts, axis=0)
    s = lax.dot_general(q4, kk, (((1,), (1,)), ((), ())), preferred_element_type=F32)
    e, ts = _pair_softmax(s, _sink_cols(sink_ref, n_ctx))
    o4 = _pair_normalize(_dot(e, vv), ts)
    pairs = [o4[g * n_ctx:(g + 1) * n_ctx] for g in range(ATTN_W // LANES)]
    f = _project(h, w_in0_ref, COL_F, COL_U)
    y_hi, y_lo = _split(jnp.concatenate([f[:, :F_W], f[:, F_W:]], axis=0))
    o_f = _dot3(cs_hi_ref[...], cs_lo_ref[...], y_hi, y_lo) + bf_ref[...]
    u = _project(h, w_in0_ref, COL_U, COL_G)
    g = _project(h, w_in0_ref, COL_G, D_PROJ)
    sg = _spatial_gate(u, g, avg_ref, lng_ref, lnb_ref, wscat_ref, bs_ref)
    o_cat = jnp.concatenate([p.astype(BF16) for p in pairs] + [o_f.astype(BF16), sg.astype(BF16)],
                            axis=1)
    xc1 = _out_ffn(xc, o_cat, m0, w_out_ref, ln1g_ref[...], ln1b_ref[...], w1_ref, w2_ref,
                   ln2g_ref[...], ln2b_ref[...])
    m1 = _mod_rows(mod1_ref[...], 1)
    h1 = (_ln(xc1) * (1.0 + m1[1]) + m1[0]).astype(BF16)
    kc1_ref[0], kc1_ref[1] = _split_keys(_project(h1, w_in1_ref, COL_K, COL_V).astype(BF16))
    vc1_ref[0], vc1_ref[1] = _split_values(_project(h1, w_in1_ref, COL_V, COL_F).astype(BF16))


def _hi_lo(a):
    a32 = jnp.asarray(np.asarray(a, np.float64), F32)
    hi = a32.astype(BF16)
    return hi, (a32 - hi.astype(F32)).astype(BF16)


def _dft_cos_sin(n):
    idx = np.arange(n, dtype=np.int64)
    ang = 2.0 * np.pi * ((idx[:, None] * idx[None, :]) % n) / n
    return np.cos(ang), np.sin(ang)


def _channel_dft_tables():
    c, s = _dft_cos_sin(F_DIM)
    bd = np.zeros((F_W, 2 * F_W))
    for g in range(F_GROUPS):
        sl = slice(g * F_DIM, (g + 1) * F_DIM)
        bd[sl, sl] = c / math.sqrt(F_DIM)
        bd[sl, F_W + g * F_DIM:F_W + (g + 1) * F_DIM] = -s / math.sqrt(F_DIM)
    return jnp.asarray(bd, F32)


def _group_avg_table():
    a = np.zeros((C_W, C_W))
    for h in range(C_HEADS):
        a[h * C_DIM:(h + 1) * C_DIM, h * C_DIM:(h + 1) * C_DIM] = 1.0 / C_DIM
    return jnp.asarray(a, BF16)


def _rope_tables(n):
    freqs = jnp.asarray(ROPE_BASE, F32) ** (-jnp.arange(0, AXIS_DIM, 2, dtype=F32) / AXIS_DIM)
    reps = LANES // HEAD_DIM

    def tables(pos, row_axis):
        ang = pos[:, None] * freqs
        c, s = jnp.cos(ang), jnp.sin(ang)
        z = jnp.zeros_like(s)
        pad = [z, z]
        pick = (lambda t: t + pad) if row_axis else (lambda t: pad + t)
        return jnp.stack([jnp.tile(jnp.concatenate(pick(t), axis=1), (1, reps))
                          for t in ([c, c], [-s, z], [z, s])])

    rt = tables(jnp.arange(n // GRID_W, dtype=F32), True)
    ct = tables(jnp.arange(GRID_W, dtype=F32), False)
    return rt, ct


def _stage1_tables(a):
    n = a * DFT_B
    c, s = _dft_cos_sin(a)
    m_hi, m_lo = _hi_lo(np.block([[c, s], [-s, c]]) / math.sqrt(a))
    k1 = np.arange(a, dtype=np.int64)[None, :, None]
    b = (np.arange(B_BLOCKS, dtype=np.int64)[:, None, None] * SUBLANES
         + np.arange(SUBLANES, dtype=np.int64)[None, None, :])
    ang = 2.0 * np.pi * ((k1 * b) % n) / n
    return m_hi, m_lo, jnp.asarray(np.cos(ang), F32), jnp.asarray(np.sin(ang), F32)


def _stage2_tables():
    c, s = _dft_cos_sin(DFT_B)
    return _hi_lo(np.concatenate([c, s], axis=1) / math.sqrt(DFT_B))


def _ctx_dft_tables(n_ctx):
    c, s = _dft_cos_sin(n_ctx)
    return _hi_lo(np.concatenate([c, s], axis=1) / math.sqrt(n_ctx))


def _const_spec(shape):
    nd = len(shape)
    return pl.BlockSpec(shape, lambda *_: (0,) * nd, pipeline_mode=pl.Buffered(1))


def _layer_spec(shape, layer):
    if len(shape) == 2:
        return _const_spec(shape)
    nd = len(shape) - 1
    return pl.BlockSpec((None,) + tuple(shape[1:]), lambda *_: (layer,) + (0,) * nd,
                        pipeline_mode=pl.Buffered(1))


def _params(*sem):
    return pltpu.CompilerParams(dimension_semantics=sem, vmem_limit_bytes=VMEM_LIMIT)


def _prep(cvt, w_ada, b_ada, w_in, bdc, wf_bd, w1, w2):
    depth = w_in.shape[0]
    tn = 6 * D_MODEL // PREP_STEPS
    cast = lambda j: jnp.minimum(j, FFN_CAST_STEPS - 1)
    per_layer = lambda shape: pl.BlockSpec((None,) + tuple(shape[1:]), lambda l, j: (l, 0, 0))
    w1_spec = pl.BlockSpec((None, D_MODEL, FFN_CAST_COLS), lambda l, j: (l, 0, cast(j)))
    w2_spec = pl.BlockSpec((None, FFN_CAST_COLS // 2, D_MODEL), lambda l, j: (l, cast(j), 0))
    return pl.pallas_call(
        _prep_kernel,
        grid=(depth, PREP_STEPS),
        in_specs=[pl.BlockSpec((D_MODEL, SUBLANES), lambda l, j: (0, 0)),
                  pl.BlockSpec((None, D_MODEL, tn), lambda l, j: (l, 0, j)),
                  pl.BlockSpec((depth, tn), lambda l, j: (0, j)),
                  per_layer(w_in.shape), pl.BlockSpec(bdc.shape, lambda l, j: (0, 0)),
                  per_layer(wf_bd.shape), w1_spec, w2_spec],
        out_specs=[pl.BlockSpec((None, SUBLANES, tn), lambda l, j: (l, 0, j)),
                   per_layer((depth, D_MODEL, D_PROJ)), w1_spec, w2_spec],
        out_shape=[jax.ShapeDtypeStruct((depth, SUBLANES, 6 * D_MODEL), F32),
                   jax.ShapeDtypeStruct((depth, D_MODEL, D_PROJ), BF16),
                   jax.ShapeDtypeStruct(w1.shape, BF16), jax.ShapeDtypeStruct(w2.shape, BF16)],
        compiler_params=_params("arbitrary", "arbitrary"),
        name="prep",
    )(cvt, w_ada, b_ada, w_in, bdc, wf_bd, w1, w2)


def _inproj(x, mods, w_in, layer, rope, shared, sgu, tm, n_sub):
    n = x.shape[0]
    a = n // DFT_B
    row = lambda w: pl.BlockSpec((tm, w), lambda i: (i, 0))
    consts = list(rope) + list(shared)
    fx_rows = tm // DFT_B * SUBLANES
    return pl.pallas_call(
        functools.partial(_inproj_kernel, n_sub=n_sub, layer=layer),
        grid=(n // tm,),
        in_specs=[row(D_MODEL), _layer_spec(mods.shape, layer), _layer_spec(w_in.shape, layer)]
        + [_const_spec(c.shape) for c in consts] + [_layer_spec(c.shape, layer) for c in sgu],
        out_specs=[row(ATTN_W), row(KV_W), row(KV_W),
                   pl.BlockSpec((4, B_BLOCKS, fx_rows, LANES), lambda i: (0, 0, i, 0)), row(C_W)],
        out_shape=[jax.ShapeDtypeStruct((n, ATTN_W), BF16), jax.ShapeDtypeStruct((n, KV_W), BF16),
                   jax.ShapeDtypeStruct((n, KV_W), BF16),
                   jax.ShapeDtypeStruct((4, B_BLOCKS, a * SUBLANES, LANES), F32),
                   jax.ShapeDtypeStruct((n, C_W), BF16)],
        compiler_params=_params("arbitrary"),
        name="inproj",
    )(x, mods, w_in, *consts, *sgu)


def _attention(q, kd, vd, kc, vc, sink_tab, layer, tq):
    n = q.shape[0]
    return pl.pallas_call(
        functools.partial(_attn_kernel, n_tokens=n),
        grid=(n // tq,),
        in_specs=[pl.BlockSpec((tq, ATTN_W), lambda i: (i, 0)), _const_spec(kd.shape),
                  _const_spec(vd.shape), _const_spec(kc.shape), _const_spec(vc.shape),
                  _layer_spec(sink_tab.shape, layer)],
        out_specs=pl.BlockSpec((tq, ATTN_W), lambda i: (i, 0)),
        out_shape=jax.ShapeDtypeStruct((n, ATTN_W), BF16),
        compiler_params=_params("arbitrary"),
        name="attention",
    )(q, kd, vd, kc, vc, sink_tab)


def _fourier(fx, tabs1, tabs2, bf_row, layer):
    a = fx.shape[2] // SUBLANES
    kb = a // SUBLANES
    m_hi, m_lo, tc, ts = tabs1
    cs_hi, cs_lo = tabs2
    bps = math.gcd(DFT_BLOCKS_PER_STEP, kb)
    n1 = B_BLOCKS // bps
    in_blk = lambda s: jnp.minimum(s, n1 - 1)
    out_blk = lambda s: jnp.maximum(s - n1, 0)
    tw = pl.BlockSpec((bps, a, SUBLANES), lambda s: (in_blk(s), 0, 0))
    return pl.pallas_call(
        functools.partial(_dft_kernel, layer=layer),
        grid=(n1 + kb // bps,),
        in_specs=[pl.BlockSpec((4, bps, a * SUBLANES, LANES), lambda s: (0, in_blk(s), 0, 0)),
                  _const_spec(m_hi.shape), _const_spec(m_lo.shape), tw, tw,
                  _const_spec(cs_hi.shape), _const_spec(cs_lo.shape),
                  _layer_spec(bf_row.shape, layer)],
        out_specs=pl.BlockSpec((F_W // LANES, bps, DFT_B * SUBLANES, LANES),
                               lambda s: (0, out_blk(s), 0, 0)),
        out_shape=jax.ShapeDtypeStruct((F_W // LANES, kb, DFT_B * SUBLANES, LANES), F32),
        scratch_shapes=[pltpu.VMEM((4, kb, DFT_B * SUBLANES, LANES), F32)],
        compiler_params=_params("arbitrary"),
        name="dft",
    )(fx, m_hi, m_lo, tc, ts, cs_hi, cs_lo, bf_row)


def _outffn(x, oa, of, sg, mods, weights, layer, tm, n_sub):
    n = x.shape[0]
    a = n // DFT_B
    row = lambda w: pl.BlockSpec((tm, w), lambda i: (i, 0))
    of_spec = pl.BlockSpec((F_W // LANES, a // SUBLANES, tm // a * SUBLANES, LANES),
                           lambda i: (0, 0, i, 0))
    consts = [mods] + list(weights)
    return pl.pallas_call(
        functools.partial(_outffn_kernel, n_sub=n_sub, layer=layer),
        grid=(n // tm,),
        in_specs=[row(D_MODEL), row(ATTN_W), of_spec, row(C_W)]
        + [_layer_spec(c.shape, layer) for c in consts],
        out_specs=row(D_MODEL),
        out_shape=jax.ShapeDtypeStruct((n, D_MODEL), F32),
        compiler_params=_params("arbitrary"),
        name="outffn",
    )(x, oa, of, sg, *consts)


def _context(xc, mods, w_in, shared, sgu, sink_tab, ctx_tabs, bf_row, weights):
    n_ctx = xc.shape[0]
    args = ([(xc, None), (mods, 0), (mods, 1), (w_in, 0), (w_in, 1)] + [(t, None) for t in shared]
            + [(t, 0) for t in sgu] + [(sink_tab, 0)] + [(t, None) for t in ctx_tabs]
            + [(bf_row, 0)] + [(t, 0) for t in weights])
    outs = [jax.ShapeDtypeStruct((KV_HEADS, n_ctx, w), BF16) for w in (LANES, 2 * LANES)] * 2
    return pl.pallas_call(
        _ctx_kernel,
        grid=(1,),
        in_specs=[_const_spec(t.shape) if l is None else _layer_spec(t.shape, l) for t, l in args],
        out_specs=[pl.BlockSpec(o.shape, lambda i: (0, 0, 0)) for o in outs],
        out_shape=outs,
        compiler_params=_params("arbitrary"),
        name="context",
    )(*[t for t, _ in args])


def _block_diag_wf(w_f):
    bd = jnp.zeros((w_f.shape[0], F_W, F_W), F32)
    for g in range(F_GROUPS):
        bd = bd.at[:, g * F_DIM:(g + 1) * F_DIM, g * F_DIM:(g + 1) * F_DIM].set(w_f[:, g])
    return bd


def _forward(x, c, ctx, c_ctx, w_ada, b_ada, w_in, w_out, attn_sink, w_fourier, b_fourier,
             sgu_ln_g, sgu_ln_b, w_spatial, b_spatial, ln1_g, ln1_b, w_ffn_in, w_ffn_out,
             ln2_g, ln2_b, tm_in=2048, n_sub_in=8, tq=1024, tm_out=1024, n_sub=4):
    n = x.shape[1]
    n_ctx = ctx.shape[1]
    depth = w_in.shape[0]
    assert x.shape[0] == 1 and n % max(tm_in, tq, tm_out) == 0 and n >= 3 * BLOCK
    assert (n // DFT_B) % SUBLANES == 0 and tm_out % (n // DFT_B) == 0 and n_ctx % CHUNK == 0
    xs = x[0]
    cvt = jnp.concatenate([c[0][:, None], c_ctx[:, None],
                           jnp.zeros((D_MODEL, SUBLANES - N_COND), F32)], axis=1)
    mods, w_in_b, w1_b, w2_b = _prep(cvt, w_ada, b_ada, w_in, _channel_dft_tables(),
                                     _block_diag_wf(w_fourier), w_ffn_in, w_ffn_out)
    w_out_p = jnp.concatenate([w_out[:, h * HEAD_DIM:(h + 1) * HEAD_DIM, :] for h in HEAD_ORDER]
                              + [w_out[:, ATTN_W:, :]], axis=1)
    weights = (w_out_p.astype(BF16), ln1_g, ln1_b, w1_b, w2_b, ln2_g, ln2_b)
    shared = (_group_avg_table(),)
    rope = _rope_tables(n)
    tabs1 = _stage1_tables(n // DFT_B)
    tabs2 = _stage2_tables()
    ctx_tabs = _ctx_dft_tables(n_ctx)
    sgu = (sgu_ln_g.reshape(depth, C_W), sgu_ln_b.reshape(depth, C_W),
           jnp.concatenate([w_spatial[:, h] for h in range(C_HEADS)], axis=2).astype(BF16),
           jnp.repeat(jnp.swapaxes(b_spatial, 1, 2), C_DIM, axis=2))
    sink_tab = jnp.broadcast_to((attn_sink * math.log2(math.e))[:, :, None], (depth, ATTN_HEADS, LANES))
    bf_row = b_fourier.reshape(depth, F_W)

    ctx_kv = _context(ctx[0], mods, w_in_b, shared, sgu, sink_tab, ctx_tabs, bf_row, weights)

    for l in range(depth):
        q, kd, vd, fx, sg = _inproj(xs, mods, w_in_b, l, rope, shared, sgu, tm_in, n_sub_in)
        oa = _attention(q, kd, vd, ctx_kv[2 * l], ctx_kv[2 * l + 1], sink_tab, l, tq)
        of = _fourier(fx, tabs1, tabs2, bf_row, l)
        xs = _outffn(xs, oa, of, sg, mods, weights, l, tm_out, n_sub)
    return xs[None]


def kernel(x, c, ctx, c_ctx, w_ada, b_ada, w_in, w_out, attn_sink, w_fourier, b_fourier, sgu_ln_g,
           sgu_ln_b, w_spatial, b_spatial, ln1_g, ln1_b, w_ffn_in, w_ffn_out, ln2_g, ln2_b):
    return _forward(x, c, ctx, c_ctx, w_ada, b_ada, w_in, w_out, attn_sink, w_fourier, b_fourier,
                    sgu_ln_g, sgu_ln_b, w_spatial, b_spatial, ln1_g, ln1_b, w_ffn_in, w_ffn_out,
                    ln2_g, ln2_b)
```

```python
import functools
import math

import numpy as np
import jax
import jax.numpy as jnp
from jax import lax
from jax.experimental import pallas as pl
from jax.experimental.pallas import tpu as pltpu

F32 = jnp.float32
BF16 = jnp.bfloat16

D_MODEL = 1024
DEPTH = 2
GRID_W = 64
HEAD_DIM = 64
ATTN_HEADS = 8
KV_HEADS = 2
WINDOW = 128
BLOCK = 128
ROPE_BASE = 10000.0
AXIS_DIM = HEAD_DIM // 2
F_GROUPS = 4
F_DIM = 64
C_HEADS = 4
C_DIM = 64
CHUNK = 128
D_FF = 2816
ATTN_W = ATTN_HEADS * HEAD_DIM
KV_W = KV_HEADS * HEAD_DIM
F_W = F_GROUPS * F_DIM
C_W = C_HEADS * C_DIM
D_IN = ATTN_W + 2 * KV_W + F_W + 2 * C_W
COL_K = ATTN_W
COL_V = COL_K + KV_W
COL_F = COL_V + KV_W
COL_U = COL_F + 2 * F_W
COL_G = COL_U + C_W
D_PROJ = COL_G + C_W
N_COND = 2
GQA_GROUP = ATTN_HEADS // KV_HEADS
HEAD_ORDER = tuple(kv * GQA_GROUP + g for g in range(GQA_GROUP) for kv in range(KV_HEADS))
Q_SCALE = HEAD_DIM ** -0.5 * math.log2(math.e)
ALPHA = (2 * DEPTH) ** 0.25
LN_EPS = 1e-6
NEG_BIG = -1e30

LANES = 128
SUBLANES = 8
MXU_TILE = 256
FFN_CHUNKS = (0, 4 * MXU_TILE, 8 * MXU_TILE, D_FF)
PREP_STEPS = 4
DFT_B = 128
B_BLOCKS = DFT_B // SUBLANES
DFT_BLOCKS_PER_STEP = 2
VMEM_LIMIT = 56 * 1024 * 1024


def _dot(a, b):
    return jnp.dot(a, b, preferred_element_type=F32)


def _split(a):
    hi = a.astype(BF16)
    lo = (a - hi.astype(F32)).astype(BF16)
    return hi, lo


def _dot3(a_hi, a_lo, b_hi, b_lo):
    return _dot(a_hi, b_hi) + _dot(a_lo, b_hi) + _dot(a_hi, b_lo)


def _ln(x):
    mu = jnp.mean(x, axis=-1, keepdims=True)
    xc = x - mu
    var = jnp.mean(xc * xc, axis=-1, keepdims=True)
    return xc * lax.rsqrt(var + LN_EPS)


def _gelu(x):
    return 0.5 * x * (1.0 + jnp.tanh(math.sqrt(2.0 / math.pi) * (x + 0.044715 * (x * x * x))))


def _silu(x):
    return x / (1.0 + jnp.exp(-x))


def _mod_rows(mod, row):
    return [mod[row:row + 1, i * D_MODEL:(i + 1) * D_MODEL] for i in range(6)]


def _rope(t, tabs):
    cos, sin_a, sin_b = tabs
    return (t * cos + pltpu.roll(t, LANES - AXIS_DIM // 2, 1) * sin_a
            + pltpu.roll(t, AXIS_DIM // 2, 1) * sin_b)


def _rope_tile(rt_ref, ct_ref, first_grid_row, rows):
    tabs = []
    for t in range(3):
        groups = [rt_ref[t, pl.ds(first_grid_row + r, 1), :] + ct_ref[t]
                  for r in range(rows // GRID_W)]
        tabs.append(jnp.concatenate(groups, axis=0) if len(groups) > 1 else groups[0])
    return tabs


def _project(h, w_in_ref, lo, hi):
    return _dot(h, w_in_ref[:, lo:hi])


def _spatial_gate_stages(u, g, avg_ref, lng_ref, lnb_ref, wscat_ref, bs_ref):
    rows = u.shape[0]
    ug = _gelu(u)
    vg = _gelu(g)
    avg = avg_ref[...]
    v_hi, v_lo = _split(vg)
    mu = _dot(v_hi, avg) + _dot(v_lo, avg)
    yield None
    vc = vg - mu
    c_hi, c_lo = _split(vc * vc)
    var = _dot(c_hi, avg) + _dot(c_lo, avg)
    yield None
    vn = vc * lax.rsqrt(var + LN_EPS) * lng_ref[...] + lnb_ref[...]
    lane = lax.broadcasted_iota(jnp.int32, (1, C_W), 1)
    vnb = vn.astype(BF16)
    zero = jnp.zeros_like(vnb)
    outs = []
    for c in range(rows // CHUNK):
        blk = vnb[c * CHUNK:(c + 1) * CHUNK]
        rhs = jnp.concatenate(
            [jnp.where((lane >= h * C_DIM) & (lane < (h + 1) * C_DIM), blk, zero[:CHUNK])
             for h in range(C_HEADS)], axis=0)
        vs = _dot(wscat_ref[...], rhs) + bs_ref[...]
        outs.append(ug[c * CHUNK:(c + 1) * CHUNK] * vs)
    yield jnp.concatenate(outs, axis=0) if len(outs) > 1 else outs[0]


def _spatial_gate(*args):
    return list(_spatial_gate_stages(*args))[-1]


def _low_lanes():
    return lax.broadcasted_iota(jnp.int32, (1, LANES), 1) < HEAD_DIM


def _split_keys(k):
    low = _low_lanes()
    z = jnp.zeros_like(k)
    return jnp.where(low, k, z), jnp.where(low, z, k)


def _split_values(v):
    top, bot = _split_keys(v)
    low_ones = jnp.where(_low_lanes(), 1.0, 0.0)
    return (jnp.concatenate([top, jnp.broadcast_to(low_ones, v.shape).astype(BF16)], axis=1),
            jnp.concatenate([bot, jnp.broadcast_to(1.0 - low_ones, v.shape).astype(BF16)], axis=1))


def _stack_groups(q):
    return jnp.concatenate([q[:, g * LANES:(g + 1) * LANES] for g in range(ATTN_W // LANES)], axis=0)


def _sink_cols(sink_ref, rows):
    return [jnp.concatenate([jnp.broadcast_to(sink_ref[4 * kh + g:4 * kh + g + 1, 0:1], (rows, 1))
                             for g in range(4)], axis=0) for kh in range(KV_HEADS)]


def _pair_softmax(s, sinks):
    half = s.shape[1] // 2
    es, ts = [], []
    for kh in range(KV_HEADS):
        sh = s[:, kh * half:(kh + 1) * half]
        m = jnp.max(sh, axis=1, keepdims=True)
        es.append(jnp.exp2(sh - m).astype(BF16))
        ts.append(jnp.exp2(sinks[kh] - m))
    return jnp.concatenate(es, axis=1), ts


def _pair_normalize(o, ts):
    return o[:, :LANES] / (o[:, LANES:] + jnp.where(_low_lanes(), ts[0], ts[1]))


def _out_ffn_stages(x, o_cat, mod_row, w_out_ref, ln1g, ln1b, w1_ref, w2_ref, ln2g, ln2b):
    g_m, sh_f, sc_f, g_f = mod_row[2], mod_row[3], mod_row[4], mod_row[5]
    y = _dot(o_cat, w_out_ref[...])
    yield None
    x1 = _ln(ALPHA * x + g_m * y) * ln1g + ln1b
    h2 = (_ln(x1) * (1.0 + sc_f) + sh_f).astype(BF16)
    yield None
    y2 = None
    for lo, hi in zip(FFN_CHUNKS[:-1], FFN_CHUNKS[1:]):
        a = _dot(h2, w1_ref[:, lo:hi])
        b = _dot(h2, w1_ref[:, D_FF + lo:D_FF + hi])
        t = (_silu(a) * b).astype(BF16)
        part = _dot(t, w2_ref[lo:hi, :])
        y2 = part if y2 is None else y2 + part
        yield None
    yield _ln(ALPHA * x1 + g_f * y2) * ln2g + ln2b


OUT_FFN_STAGES = len(FFN_CHUNKS) + 2


def _run_skewed(gens, n_stages, skew):
    results = [None] * len(gens)
    for t in range(n_stages + skew * (len(gens) - 1)):
        for r, gen in enumerate(gens):
            if 0 <= t - r * skew < n_stages:
                results[r] = next(gen)
    return results


def _out_ffn(*args):
    return _run_skewed([_out_ffn_stages(*args)], OUT_FFN_STAGES, 0)[0]


def _layer_rows(layer, *refs):
    return [r.at[pl.ds(layer, 1)] for r in refs]


def _ada_kernel(cvt_ref, w_ref, b_ref, o_ref):
    s = _silu(cvt_ref[...])
    w = w_ref[...]
    bias = b_ref[pl.ds(pl.program_id(0), 1), :]
    o_ref[...] = jnp.zeros_like(o_ref)
    for r in range(N_COND):
        o_ref[r:r + 1, :] = jnp.sum(w * s[:, r:r + 1], axis=0, keepdims=True) + bias


def _wprep_kernel(w_in_ref, bdc_ref, wf_ref, o_ref):
    hp = functools.partial(jnp.dot, precision=lax.Precision.HIGHEST, preferred_element_type=F32)
    wf = wf_ref[...]
    fold = jnp.concatenate([hp(bdc_ref[:, :F_W], wf), hp(bdc_ref[:, F_W:], wf)], axis=1)
    for j, head in enumerate(HEAD_ORDER):
        o_ref[:, j * HEAD_DIM:(j + 1) * HEAD_DIM] = (
            w_in_ref[:, head * HEAD_DIM:(head + 1) * HEAD_DIM] * Q_SCALE).astype(BF16)
    o_ref[:, COL_K:COL_F] = w_in_ref[:, COL_K:COL_F].astype(BF16)
    o_ref[:, COL_F:COL_U] = hp(w_in_ref[:, COL_F:COL_F + F_W], fold).astype(BF16)
    o_ref[:, COL_U:] = w_in_ref[:, COL_F + F_W:].astype(BF16)


def _prep_kernel(cvt_ref, w_ada_ref, b_ada_ref, w_in_ref, bdc_ref, wf_ref, mods_ref, wcat_ref):
    _ada_kernel(cvt_ref, w_ada_ref, b_ada_ref, mods_ref)

    @pl.when(pl.program_id(1) == 0)
    def _():
        _wprep_kernel(w_in_ref, bdc_ref, wf_ref, wcat_ref)


def _cast_stages(pairs, n_stages):
    for s in range(n_stages):
        for src, dst in pairs:
            step = -(-src.shape[0] // (n_stages * 2 * SUBLANES)) * 2 * SUBLANES
            lo, hi = min(s * step, src.shape[0]), min((s + 1) * step, src.shape[0])
            if hi > lo:
                dst[lo:hi, :] = src[lo:hi, :].astype(BF16)
        yield


def _inproj_stages(x, sh, sc, w_ref, tabs, avg_ref, lng_ref, lnb_ref, wscat_ref, bs_ref,
                   q_ref, kd_ref, vd_ref, fx_ref, sg_ref, row0):
    rows = x.shape[0]
    rs = slice(row0, row0 + rows)
    h = (_ln(x) * (1.0 + sc) + sh).astype(BF16)
    yield
    for p in range(ATTN_W // MXU_TILE):
        qq = _project(h, w_ref, p * MXU_TILE, (p + 1) * MXU_TILE)
        for j in range(MXU_TILE // LANES):
            c0 = p * MXU_TILE + j * LANES
            q_ref[rs, c0:c0 + LANES] = _rope(qq[:, j * LANES:(j + 1) * LANES], tabs).astype(BF16)
    kv = _project(h, w_ref, COL_K, COL_F)
    kd_ref[rs, :] = _rope(kv[:, :KV_W], tabs).astype(BF16)
    vd_ref[rs, :] = kv[:, KV_W:].astype(BF16)
    yield
    f = _project(h, w_ref, COL_F, COL_U)
    for al in range(rows // DFT_B):
        for bb in range(B_BLOCKS):
            r0 = al * DFT_B + bb * SUBLANES
            a0 = (row0 // DFT_B + al) * SUBLANES
            for p in range(4):
                fx_ref[p, bb, a0:a0 + SUBLANES, :] = f[r0:r0 + SUBLANES, p * LANES:(p + 1) * LANES]
    u = _project(h, w_ref, COL_U, COL_G)
    g = _project(h, w_ref, COL_G, D_PROJ)
    gate = _spatial_gate_stages(u, g, avg_ref, lng_ref, lnb_ref, wscat_ref, bs_ref)
    next(gate)
    yield
    next(gate)
    yield
    sg_ref[rs, :] = next(gate).astype(BF16)
    yield


INPROJ_STAGES = 5


def _inproj_kernel(x_ref, mod_ref, w_ref, rt_ref, ct_ref, avg_ref, lng_ref, lnb_ref, wscat_ref,
                   bs_ref, w1_ref, w2_ref, q_ref, kd_ref, vd_ref, fx_ref, sg_ref, w1b_ref, w2b_ref,
                   *, n_sub, layer):
    lng_ref, lnb_ref = _layer_rows(layer, lng_ref, lnb_ref)
    tm = x_ref.shape[0]
    sub = tm // n_sub
    sh, sc = _mod_rows(mod_ref[...], 0)[:2]
    gens = []
    for r in range(n_sub):
        grid_row = pl.program_id(0) * (tm // GRID_W) + r * (sub // GRID_W)
        gens.append(_inproj_stages(
            x_ref[r * sub:(r + 1) * sub, :], sh, sc, w_ref, _rope_tile(rt_ref, ct_ref, grid_row, sub),
            avg_ref, lng_ref, lnb_ref, wscat_ref, bs_ref, q_ref, kd_ref, vd_ref, fx_ref, sg_ref,
            r * sub))
    gens.append(_cast_stages([(w1_ref, w1b_ref), (w2_ref, w2b_ref)], INPROJ_STAGES))
    _run_skewed(gens, INPROJ_STAGES, 1)


def _attn_block_stages(q_ref, k_ref, v_ref, kc_ref, vc_ref, sinks, o_ref, blk, row0, n_tokens):
    span = 3 * BLOCK
    n_ctx = kc_ref.shape[1]
    half = span + n_ctx
    start = pl.multiple_of(jnp.clip((blk - 1) * BLOCK, 0, n_tokens - span), BLOCK)
    k_top, k_bot = _split_keys(k_ref[pl.ds(start, span), :])
    v_top, v_bot = _split_values(v_ref[pl.ds(start, span), :])
    kk = jnp.concatenate([k_top, kc_ref[0], k_bot, kc_ref[1]], axis=0)
    vv = jnp.concatenate([v_top, vc_ref[0], v_bot, vc_ref[1]], axis=0)
    q4 = _stack_groups(q_ref[row0:row0 + BLOCK, :])
    yield
    s = lax.dot_general(q4, kk, (((1,), (1,)), ((), ())), preferred_element_type=F32)
    yield
    qpos = blk * BLOCK + lax.broadcasted_iota(jnp.int32, (BLOCK, span), 0)
    kpos = start + lax.broadcasted_iota(jnp.int32, (BLOCK, span), 1)
    ok = jnp.concatenate([jnp.abs(kpos - qpos) <= WINDOW] * 4, axis=0)
    s = jnp.concatenate([jnp.where(ok, s[:, :span], NEG_BIG), s[:, span:half],
                         jnp.where(ok, s[:, half:half + span], NEG_BIG), s[:, half + span:]], axis=1)
    e, ts = _pair_softmax(s, sinks)
    yield
    o = _dot(e, vv)
    yield
    res = _pair_normalize(o, ts).astype(BF16)
    for g in range(ATTN_W // LANES):
        o_ref[row0:row0 + BLOCK, g * LANES:(g + 1) * LANES] = res[g * BLOCK:(g + 1) * BLOCK]
    yield


ATTN_STAGES = 5


def _attn_kernel(q_ref, k_ref, v_ref, kc_ref, vc_ref, sink_ref, o_ref, *, n_tokens):
    n_blocks = q_ref.shape[0] // BLOCK
    sinks = _sink_cols(sink_ref, BLOCK)
    gens = [_attn_block_stages(q_ref, k_ref, v_ref, kc_ref, vc_ref, sinks, o_ref,
                               pl.program_id(0) * n_blocks + r, r * BLOCK, n_tokens)
            for r in range(n_blocks)]
    _run_skewed(gens, ATTN_STAGES, 1)


def _complex_rows(parts):
    return _split(jnp.concatenate([jnp.concatenate(parts[0:2], axis=1),
                                   jnp.concatenate(parts[2:4], axis=1)], axis=0))


def _dft_stage1_stages(x_ref, m_hi_ref, m_lo_ref, tc_ref, ts_ref, y_scr, j, bl, row0):
    a = m_hi_ref.shape[0] // 2
    x_hi, x_lo = _complex_rows([x_ref[i, j, pl.ds(bl, a, stride=SUBLANES), :] for i in range(4)])
    yield
    y = _dot3(m_hi_ref[...], m_lo_ref[...], x_hi, x_lo)
    yield
    yr, yi = y[:a], y[a:]
    tc, ts = tc_ref[j, :, bl:bl + 1], ts_ref[j, :, bl:bl + 1]
    wr = yr * tc + yi * ts
    wi = yi * tc - yr * ts
    parts = (wr[:, :LANES], wr[:, LANES:], wi[:, :LANES], wi[:, LANES:])
    for kb in range(a // SUBLANES):
        for i in range(4):
            y_scr[i, kb, pl.ds(row0, SUBLANES), :] = parts[i][kb * SUBLANES:(kb + 1) * SUBLANES]
    yield


def _dft_stage2_stages(y_scr, kb, cs_hi_ref, cs_lo_ref, bf_ref, o_ref, j, kl):
    r_hi, r_lo = _complex_rows([y_scr[i, kb, pl.ds(kl, DFT_B, stride=SUBLANES), :] for i in range(4)])
    yield
    out = _dot3(cs_hi_ref[...], cs_lo_ref[...], r_hi, r_lo) + bf_ref[...]
    yield
    for hf in range(F_W // LANES):
        o_ref[hf, j, pl.ds(kl, DFT_B, stride=SUBLANES), :] = out[:, hf * LANES:(hf + 1) * LANES]
    yield


DFT_STAGES = 3


def _dft_kernel(x_ref, m_hi_ref, m_lo_ref, tc_ref, ts_ref, cs_hi_ref, cs_lo_ref, bf_ref, o_ref, y_scr,
                *, layer):
    bf_ref, = _layer_rows(layer, bf_ref)
    step = pl.program_id(0)
    bps = x_ref.shape[1]
    n1 = B_BLOCKS // bps

    @pl.when(step < n1)
    def _():
        gens = [_dft_stage1_stages(
            x_ref, m_hi_ref, m_lo_ref, tc_ref, ts_ref, y_scr, j, bl,
            pl.multiple_of(((step * bps + j) * SUBLANES + bl) * SUBLANES, SUBLANES))
            for j in range(bps) for bl in range(SUBLANES)]
        _run_skewed(gens, DFT_STAGES, 1)

    @pl.when(step >= n1)
    def _():
        gens = [_dft_stage2_stages(y_scr, (step - n1) * bps + j, cs_hi_ref, cs_lo_ref, bf_ref, o_ref, j, kl)
                for j in range(bps) for kl in range(SUBLANES)]
        _run_skewed(gens, DFT_STAGES, 1)


def _outffn_kernel(x_ref, oa_ref, of_ref, sg_ref, mod_ref, w_out_ref, ln1g_ref, ln1b_ref,
                   w1_ref, w2_ref, ln2g_ref, ln2b_ref, o_ref, *, n_sub, layer):
    ln1g_ref, ln1b_ref, ln2g_ref, ln2b_ref = _layer_rows(layer, ln1g_ref, ln1b_ref, ln2g_ref, ln2b_ref)
    n_kb, k2_rows = of_ref.shape[1], of_ref.shape[2]
    halves = []
    for hf in range(F_W // LANES):
        pieces = [of_ref[hf, kb, k2 * SUBLANES:(k2 + 1) * SUBLANES, :]
                  for k2 in range(k2_rows // SUBLANES) for kb in range(n_kb)]
        halves.append(jnp.concatenate(pieces, axis=0).astype(BF16))
    o_cat = jnp.concatenate([oa_ref[...]] + halves + [sg_ref[...]], axis=1)
    tm = x_ref.shape[0]
    sub = tm // n_sub
    mod_row = _mod_rows(mod_ref[...], 0)
    gens = [_out_ffn_stages(x_ref[r * sub:(r + 1) * sub, :], o_cat[r * sub:(r + 1) * sub], mod_row,
                            w_out_ref, ln1g_ref[...], ln1b_ref[...], w1_ref, w2_ref,
                            ln2g_ref[...], ln2b_ref[...]) for r in range(n_sub)]
    for r, res in enumerate(_run_skewed(gens, OUT_FFN_STAGES, 1)):
        o_ref[r * sub:(r + 1) * sub, :] = res


def _ctx_kernel(xc_ref, mod0_ref, mod1_ref, w_in0_ref, w_in1_ref, avg_ref,
                lng_ref, lnb_ref, wscat_ref, bs_ref, sink_ref, cs_hi_ref, cs_lo_ref, bf_ref,
                w_out_ref, ln1g_ref, ln1b_ref, w1_ref, w2_ref, ln2g_ref, ln2b_ref,
                kc0_ref, vc0_ref, kc1_ref, vc1_ref):
    lng_ref, lnb_ref, bf_ref, ln1g_ref, ln1b_ref, ln2g_ref, ln2b_ref = _layer_rows(
        0, lng_ref, lnb_ref, bf_ref, ln1g_ref, ln1b_ref, ln2g_ref, ln2b_ref)
    xc = xc_ref[...]
    n_ctx = xc.shape[0]
    m0 = _mod_rows(mod0_ref[...], 1)
    h = (_ln(xc) * (1.0 + m0[1]) + m0[0]).astype(BF16)
    q4 = _stack_groups(_project(h, w_in0_ref, 0, COL_K).astype(BF16))
    k_parts = _split_keys(_project(h, w_in0_ref, COL_K, COL_V).astype(BF16))
    v_parts = _split_values(_project(h, w_in0_ref, COL_V, COL_F).astype(BF16))
    kc0_ref[0], kc0_ref[1] = k_parts
    vc0_ref[0], vc0_ref[1] = v_parts
    kk = jnp.concatenate(k_parts, axis=0)
    vv = jnp.concatenate(v_parts, axis=0)
    s = lax.dot_general(q4, kk, (((1,), (1,)), ((), ())), preferred_element_type=F32)
    e, ts = _pair_softmax(s, _sink_cols(sink_ref, n_ctx))
    o4 = _pair_normalize(_dot(e, vv), ts)
    pairs = [o4[g * n_ctx:(g + 1) * n_ctx] for g in range(ATTN_W // LANES)]
    f = _project(h, w_in0_ref, COL_F, COL_U)
    y_hi, y_lo = _split(jnp.concatenate([f[:, :F_W], f[:, F_W:]], axis=0))
    o_f = _dot3(cs_hi_ref[...], cs_lo_ref[...], y_hi, y_lo) + bf_ref[...]
    u = _project(h, w_in0_ref, COL_U, COL_G)
    g = _project(h, w_in0_ref, COL_G, D_PROJ)
    sg = _spatial_gate(u, g, avg_ref, lng_ref, lnb_ref, wscat_ref, bs_ref)
    o_cat = jnp.concatenate([p.astype(BF16) for p in pairs] + [o_f.astype(BF16), sg.astype(BF16)],
                            axis=1)
    xc1 = _out_ffn(xc, o_cat, m0, w_out_ref, ln1g_ref[...], ln1b_ref[...], w1_ref, w2_ref,
                   ln2g_ref[...], ln2b_ref[...])
    m1 = _mod_rows(mod1_ref[...], 1)
    h1 = (_ln(xc1) * (1.0 + m1[1]) + m1[0]).astype(BF16)
    kc1_ref[0], kc1_ref[1] = _split_keys(_project(h1, w_in1_ref, COL_K, COL_V).astype(BF16))
    vc1_ref[0], vc1_ref[1] = _split_values(_project(h1, w_in1_ref, COL_V, COL_F).astype(BF16))


def _hi_lo(a):
    a32 = jnp.asarray(np.asarray(a, np.float64), F32)
    hi = a32.astype(BF16)
    return hi, (a32 - hi.astype(F32)).astype(BF16)


def _dft_cos_sin(n):
    idx = np.arange(n, dtype=np.int64)
    ang = 2.0 * np.pi * ((idx[:, None] * idx[None, :]) % n) / n
    return np.cos(ang), np.sin(ang)


def _channel_dft_tables():
    c, s = _dft_cos_sin(F_DIM)
    bd = np.zeros((F_W, 2 * F_W))
    for g in range(F_GROUPS):
        sl = slice(g * F_DIM, (g + 1) * F_DIM)
        bd[sl, sl] = c / math.sqrt(F_DIM)
        bd[sl, F_W + g * F_DIM:F_W + (g + 1) * F_DIM] = -s / math.sqrt(F_DIM)
    return jnp.asarray(bd, F32)


def _group_avg_table():
    a = np.zeros((C_W, C_W))
    for h in range(C_HEADS):
        a[h * C_DIM:(h + 1) * C_DIM, h * C_DIM:(h + 1) * C_DIM] = 1.0 / C_DIM
    return jnp.asarray(a, BF16)


def _rope_tables(n):
    freqs = jnp.asarray(ROPE_BASE, F32) ** (-jnp.arange(0, AXIS_DIM, 2, dtype=F32) / AXIS_DIM)
    reps = LANES // HEAD_DIM

    def tables(pos, row_axis):
        ang = pos[:, None] * freqs
        c, s = jnp.cos(ang), jnp.sin(ang)
        z = jnp.zeros_like(s)
        pad = [z, z]
        pick = (lambda t: t + pad) if row_axis else (lambda t: pad + t)
        return jnp.stack([jnp.tile(jnp.concatenate(pick(t), axis=1), (1, reps))
                          for t in ([c, c], [-s, z], [z, s])])

    rt = tables(jnp.arange(n // GRID_W, dtype=F32), True)
    ct = tables(jnp.arange(GRID_W, dtype=F32), False)
    return rt, ct


def _stage1_tables(a):
    n = a * DFT_B
    c, s = _dft_cos_sin(a)
    m_hi, m_lo = _hi_lo(np.block([[c, s], [-s, c]]) / math.sqrt(a))
    k1 = np.arange(a, dtype=np.int64)[None, :, None]
    b = (np.arange(B_BLOCKS, dtype=np.int64)[:, None, None] * SUBLANES
         + np.arange(SUBLANES, dtype=np.int64)[None, None, :])
    ang = 2.0 * np.pi * ((k1 * b) % n) / n
    return m_hi, m_lo, jnp.asarray(np.cos(ang), F32), jnp.asarray(np.sin(ang), F32)


def _stage2_tables():
    c, s = _dft_cos_sin(DFT_B)
    return _hi_lo(np.concatenate([c, s], axis=1) / math.sqrt(DFT_B))


def _ctx_dft_tables(n_ctx):
    c, s = _dft_cos_sin(n_ctx)
    return _hi_lo(np.concatenate([c, s], axis=1) / math.sqrt(n_ctx))


def _const_spec(shape):
    nd = len(shape)
    return pl.BlockSpec(shape, lambda *_: (0,) * nd, pipeline_mode=pl.Buffered(1))


def _layer_spec(shape, layer):
    if len(shape) == 2:
        return _const_spec(shape)
    nd = len(shape) - 1
    return pl.BlockSpec((None,) + tuple(shape[1:]), lambda *_: (layer,) + (0,) * nd,
                        pipeline_mode=pl.Buffered(1))


def _params(*sem):
    return pltpu.CompilerParams(dimension_semantics=sem, vmem_limit_bytes=VMEM_LIMIT)


def _prep(cvt, w_ada, b_ada, w_in, bdc, wf_bd):
    depth = w_in.shape[0]
    tn = 6 * D_MODEL // PREP_STEPS
    per_layer = lambda shape: pl.BlockSpec((None,) + tuple(shape[1:]), lambda l, j: (l, 0, 0))
    return pl.pallas_call(
        _prep_kernel,
        grid=(depth, PREP_STEPS),
        in_specs=[pl.BlockSpec((D_MODEL, SUBLANES), lambda l, j: (0, 0)),
                  pl.BlockSpec((None, D_MODEL, tn), lambda l, j: (l, 0, j)),
                  pl.BlockSpec((depth, tn), lambda l, j: (0, j)),
                  per_layer(w_in.shape), pl.BlockSpec(bdc.shape, lambda l, j: (0, 0)),
                  per_layer(wf_bd.shape)],
        out_specs=[pl.BlockSpec((None, SUBLANES, tn), lambda l, j: (l, 0, j)),
                   per_layer((depth, D_MODEL, D_PROJ))],
        out_shape=[jax.ShapeDtypeStruct((depth, SUBLANES, 6 * D_MODEL), F32),
                   jax.ShapeDtypeStruct((depth, D_MODEL, D_PROJ), BF16)],
        compiler_params=_params("arbitrary", "arbitrary"),
        name="prep",
    )(cvt, w_ada, b_ada, w_in, bdc, wf_bd)


def _inproj(x, mods, w_in, layer, rope, shared, sgu, w1, w2, tm, n_sub):
    n = x.shape[0]
    a = n // DFT_B
    steps = n // tm
    row = lambda w: pl.BlockSpec((tm, w), lambda i: (i, 0))
    chunk = lambda w: pl.BlockSpec((None, w.shape[1] // steps, w.shape[2]), lambda i: (layer, i, 0))
    consts = list(rope) + list(shared)
    fx_rows = tm // DFT_B * SUBLANES
    q, kd, vd, fx, sg, w1b, w2b = pl.pallas_call(
        functools.partial(_inproj_kernel, n_sub=n_sub, layer=layer),
        grid=(steps,),
        in_specs=[row(D_MODEL), _layer_spec(mods.shape, layer), _layer_spec(w_in.shape, layer)]
        + [_const_spec(c.shape) for c in consts] + [_layer_spec(c.shape, layer) for c in sgu]
        + [chunk(w1), chunk(w2)],
        out_specs=[row(ATTN_W), row(KV_W), row(KV_W),
                   pl.BlockSpec((4, B_BLOCKS, fx_rows, LANES), lambda i: (0, 0, i, 0)), row(C_W)]
        + [pl.BlockSpec((w.shape[1] // steps, w.shape[2]), lambda i: (i, 0)) for w in (w1, w2)],
        out_shape=[jax.ShapeDtypeStruct((n, ATTN_W), BF16), jax.ShapeDtypeStruct((n, KV_W), BF16),
                   jax.ShapeDtypeStruct((n, KV_W), BF16),
                   jax.ShapeDtypeStruct((4, B_BLOCKS, a * SUBLANES, LANES), F32),
                   jax.ShapeDtypeStruct((n, C_W), BF16)]
        + [jax.ShapeDtypeStruct(w.shape[1:], BF16) for w in (w1, w2)],
        compiler_params=_params("arbitrary"),
        name="inproj",
    )(x, mods, w_in, *consts, *sgu, w1, w2)
    return q, kd, vd, fx, sg, w1b, w2b


def _attention(q, kd, vd, kc, vc, sink_tab, layer, tq):
    n = q.shape[0]
    return pl.pallas_call(
        functools.partial(_attn_kernel, n_tokens=n),
        grid=(n // tq,),
        in_specs=[pl.BlockSpec((tq, ATTN_W), lambda i: (i, 0)), _const_spec(kd.shape),
                  _const_spec(vd.shape), _const_spec(kc.shape), _const_spec(vc.shape),
                  _layer_spec(sink_tab.shape, layer)],
        out_specs=pl.BlockSpec((tq, ATTN_W), lambda i: (i, 0)),
        out_shape=jax.ShapeDtypeStruct((n, ATTN_W), BF16),
        compiler_params=_params("arbitrary"),
        name="attention",
    )(q, kd, vd, kc, vc, sink_tab)


def _fourier(fx, tabs1, tabs2, bf_row, layer):
    a = fx.shape[2] // SUBLANES
    kb = a // SUBLANES
    m_hi, m_lo, tc, ts = tabs1
    cs_hi, cs_lo = tabs2
    bps = math.gcd(DFT_BLOCKS_PER_STEP, kb)
    n1 = B_BLOCKS // bps
    in_blk = lambda s: jnp.minimum(s, n1 - 1)
    out_blk = lambda s: jnp.maximum(s - n1, 0)
    tw = pl.BlockSpec((bps, a, SUBLANES), lambda s: (in_blk(s), 0, 0))
    return pl.pallas_call(
        functools.partial(_dft_kernel, layer=layer),
        grid=(n1 + kb // bps,),
        in_specs=[pl.BlockSpec((4, bps, a * SUBLANES, LANES), lambda s: (0, in_blk(s), 0, 0)),
                  _const_spec(m_hi.shape), _const_spec(m_lo.shape), tw, tw,
                  _const_spec(cs_hi.shape), _const_spec(cs_lo.shape),
                  _layer_spec(bf_row.shape, layer)],
        out_specs=pl.BlockSpec((F_W // LANES, bps, DFT_B * SUBLANES, LANES),
                               lambda s: (0, out_blk(s), 0, 0)),
        out_shape=jax.ShapeDtypeStruct((F_W // LANES, kb, DFT_B * SUBLANES, LANES), F32),
        scratch_shapes=[pltpu.VMEM((4, kb, DFT_B * SUBLANES, LANES), F32)],
        compiler_params=_params("arbitrary"),
        name="dft",
    )(fx, m_hi, m_lo, tc, ts, cs_hi, cs_lo, bf_row)


def _outffn(x, oa, of, sg, mods, weights, layer, tm, n_sub):
    n = x.shape[0]
    a = n // DFT_B
    row = lambda w: pl.BlockSpec((tm, w), lambda i: (i, 0))
    of_spec = pl.BlockSpec((F_W // LANES, a // SUBLANES, tm // a * SUBLANES, LANES),
                           lambda i: (0, 0, i, 0))
    consts = [mods] + list(weights)
    return pl.pallas_call(
        functools.partial(_outffn_kernel, n_sub=n_sub, layer=layer),
        grid=(n // tm,),
        in_specs=[row(D_MODEL), row(ATTN_W), of_spec, row(C_W)]
        + [_layer_spec(c.shape, layer) for c in consts],
        out_specs=row(D_MODEL),
        out_shape=jax.ShapeDtypeStruct((n, D_MODEL), F32),
        compiler_params=_params("arbitrary"),
        name="outffn",
    )(x, oa, of, sg, *consts)


def _context(xc, mods, w_in, shared, sgu, sink_tab, ctx_tabs, bf_row, weights):
    n_ctx = xc.shape[0]
    args = ([(xc, None), (mods, 0), (mods, 1), (w_in, 0), (w_in, 1)] + [(t, None) for t in shared]
            + [(t, 0) for t in sgu] + [(sink_tab, 0)] + [(t, None) for t in ctx_tabs]
            + [(bf_row, 0)] + [(t, 0) for t in weights])
    outs = [jax.ShapeDtypeStruct((KV_HEADS, n_ctx, w), BF16) for w in (LANES, 2 * LANES)] * 2
    return pl.pallas_call(
        _ctx_kernel,
        grid=(1,),
        in_specs=[_const_spec(t.shape) if l is None else _layer_spec(t.shape, l) for t, l in args],
        out_specs=[pl.BlockSpec(o.shape, lambda i: (0, 0, 0)) for o in outs],
        out_shape=outs,
        compiler_params=_params("arbitrary"),
        name="context",
    )(*[t for t, _ in args])


def _block_diag_wf(w_f):
    bd = jnp.zeros((w_f.shape[0], F_W, F_W), F32)
    for g in range(F_GROUPS):
        bd = bd.at[:, g * F_DIM:(g + 1) * F_DIM, g * F_DIM:(g + 1) * F_DIM].set(w_f[:, g])
    return bd


def _forward(x, c, ctx, c_ctx, w_ada, b_ada, w_in, w_out, attn_sink, w_fourier, b_fourier,
             sgu_ln_g, sgu_ln_b, w_spatial, b_spatial, ln1_g, ln1_b, w_ffn_in, w_ffn_out,
             ln2_g, ln2_b, tm_in=1024, n_sub_in=4, tq=1024, tm_out=1024, n_sub=4):
    n = x.shape[1]
    n_ctx = ctx.shape[1]
    depth = w_in.shape[0]
    assert x.shape[0] == 1 and n % max(tm_in, tq, tm_out) == 0 and n >= 3 * BLOCK
    assert (n // DFT_B) % SUBLANES == 0 and tm_out % (n // DFT_B) == 0 and n_ctx % CHUNK == 0
    xs = x[0]
    cvt = jnp.concatenate([c[0][:, None], c_ctx[:, None],
                           jnp.zeros((D_MODEL, SUBLANES - N_COND), F32)], axis=1)
    mods, w_in_b = _prep(cvt, w_ada, b_ada, w_in, _channel_dft_tables(), _block_diag_wf(w_fourier))
    w_out_b = jnp.concatenate([w_out[:, h * HEAD_DIM:(h + 1) * HEAD_DIM, :] for h in HEAD_ORDER]
                              + [w_out[:, ATTN_W:, :]], axis=1).astype(BF16)
    shared = (_group_avg_table(),)
    rope = _rope_tables(n)
    tabs1 = _stage1_tables(n // DFT_B)
    tabs2 = _stage2_tables()
    ctx_tabs = _ctx_dft_tables(n_ctx)
    sgu = (sgu_ln_g.reshape(depth, C_W), sgu_ln_b.reshape(depth, C_W),
           jnp.concatenate([w_spatial[:, h] for h in range(C_HEADS)], axis=2).astype(BF16),
           jnp.repeat(jnp.swapaxes(b_spatial, 1, 2), C_DIM, axis=2))
    sink_tab = jnp.broadcast_to((attn_sink * math.log2(math.e))[:, :, None], (depth, ATTN_HEADS, LANES))
    bf_row = b_fourier.reshape(depth, F_W)

    ctx_kv = None
    for l in range(depth):
        q, kd, vd, fx, sg, w1_b, w2_b = _inproj(xs, mods, w_in_b, l, rope, shared, sgu,
                                                w_ffn_in, w_ffn_out, tm_in, n_sub_in)
        weights = (w_out_b, ln1_g, ln1_b, w1_b, w2_b, ln2_g, ln2_b)
        if ctx_kv is None:
            ctx_kv = _context(ctx[0], mods, w_in_b, shared, sgu, sink_tab, ctx_tabs, bf_row, weights)
        oa = _attention(q, kd, vd, ctx_kv[2 * l], ctx_kv[2 * l + 1], sink_tab, l, tq)
        of = _fourier(fx, tabs1, tabs2, bf_row, l)
        xs = _outffn(xs, oa, of, sg, mods, weights, l, tm_out, n_sub)
    return xs[None]


def kernel(x, c, ctx, c_ctx, w_ada, b_ada, w_in, w_out, attn_sink, w_fourier, b_fourier, sgu_ln_g,
           sgu_ln_b, w_spatial, b_spatial, ln1_g, ln1_b, w_ffn_in, w_ffn_out, ln2_g, ln2_b):
    return _forward(x, c, ctx, c_ctx, w_ada, b_ada, w_in, w_out, attn_sink, w_fourier, b_fourier,
                    sgu_ln_g, sgu_ln_b, w_spatial, b_spatial, ln1_g, ln1_b, w_ffn_in, w_ffn_out,
                    ln2_g, ln2_b)
```

```python
import functools
import math

import numpy as np
import jax
import jax.numpy as jnp
from jax import lax
from jax.experimental import pallas as pl
from jax.experimental.pallas import tpu as pltpu

F32 = jnp.float32
BF16 = jnp.bfloat16

D_MODEL = 1024
DEPTH = 2
GRID_W = 64
HEAD_DIM = 64
ATTN_HEADS = 8
KV_HEADS = 2
WINDOW = 128
BLOCK = 128
ROPE_BASE = 10000.0
AXIS_DIM = HEAD_DIM // 2
F_GROUPS = 4
F_DIM = 64
C_HEADS = 4
C_DIM = 64
CHUNK = 128
D_FF = 2816
ATTN_W = ATTN_HEADS * HEAD_DIM
KV_W = KV_HEADS * HEAD_DIM
F_W = F_GROUPS * F_DIM
C_W = C_HEADS * C_DIM
COL_K = ATTN_W
COL_V = COL_K + KV_W
COL_F = COL_V + KV_W
COL_U = COL_F + 2 * F_W
COL_G = COL_U + C_W
D_PROJ = COL_G + C_W
N_COND = 2
GQA_GROUP = ATTN_HEADS // KV_HEADS
HEAD_ORDER = tuple(kv * GQA_GROUP + g for g in range(GQA_GROUP) for kv in range(KV_HEADS))
Q_SCALE = HEAD_DIM ** -0.5 * math.log2(math.e)
ALPHA = (2 * DEPTH) ** 0.25
LN_EPS = 1e-6
NEG_BIG = -1e30

LANES = 128
SUBLANES = 8
MXU_TILE = 256
FFN_CHUNKS = (0, 4 * MXU_TILE, 8 * MXU_TILE, D_FF)
PREP_STEPS = 4
DFT_B = 128
B_BLOCKS = DFT_B // SUBLANES
DFT_BLOCKS_PER_STEP = 2
VMEM_LIMIT = 60 * 1024 * 1024


def _dot(a, b):
    return jnp.dot(a, b, preferred_element_type=F32)


def _split(a):
    hi = a.astype(BF16)
    lo = (a - hi.astype(F32)).astype(BF16)
    return hi, lo


def _dot3(a_cat, b_hi, b_lo):
    rows = a_cat.shape[0] // 2
    both = _dot(a_cat, b_hi)
    return both[:rows] + both[rows:] + _dot(a_cat[:rows], b_lo)


def _ln(x):
    mu = jnp.mean(x, axis=-1, keepdims=True)
    xc = x - mu
    var = jnp.mean(xc * xc, axis=-1, keepdims=True)
    return xc * lax.rsqrt(var + LN_EPS)


def _gelu(x):
    return 0.5 * x * (1.0 + jnp.tanh(math.sqrt(2.0 / math.pi) * (x + 0.044715 * (x * x * x))))


def _silu(x):
    return x / (1.0 + jnp.exp(-x))


def _mod_rows(mod, row):
    return [mod[row:row + 1, i * D_MODEL:(i + 1) * D_MODEL] for i in range(6)]


def _rope(t, tabs):
    cos, sin_a, sin_b = tabs
    return (t * cos + pltpu.roll(t, LANES - AXIS_DIM // 2, 1) * sin_a
            + pltpu.roll(t, AXIS_DIM // 2, 1) * sin_b)


def _rope_tile(rt_ref, ct_ref, first_grid_row, rows):
    tabs = []
    for t in range(3):
        groups = [rt_ref[t, pl.ds(first_grid_row + r, 1), :] + ct_ref[t]
                  for r in range(rows // GRID_W)]
        tabs.append(jnp.concatenate(groups, axis=0) if len(groups) > 1 else groups[0])
    return tabs


def _project(h, w_in_ref, lo, hi):
    return _dot(h, w_in_ref[:, lo:hi])


def _spatial_gate_stages(u, g, avg_ref, lng_ref, lnb_ref, wscat_ref, bs_ref):
    rows = u.shape[0]
    ug = _gelu(u)
    vg = _gelu(g)
    avg = avg_ref[...]
    mu = _dot(jnp.concatenate(_split(vg), axis=1), avg)
    yield None
    vc = vg - mu
    var = _dot(jnp.concatenate(_split(vc * vc), axis=1), avg)
    yield None
    vn = vc * lax.rsqrt(var + LN_EPS) * lng_ref[...] + lnb_ref[...]
    lane = lax.broadcasted_iota(jnp.int32, (1, C_W), 1)
    vnb = vn.astype(BF16)
    zero = jnp.zeros_like(vnb)
    outs = []
    for c in range(rows // CHUNK):
        blk = vnb[c * CHUNK:(c + 1) * CHUNK]
        rhs = jnp.concatenate(
            [jnp.where((lane >= h * C_DIM) & (lane < (h + 1) * C_DIM), blk, zero[:CHUNK])
             for h in range(C_HEADS)], axis=0)
        vs = _dot(wscat_ref[...], rhs) + bs_ref[...]
        outs.append(ug[c * CHUNK:(c + 1) * CHUNK] * vs)
    yield jnp.concatenate(outs, axis=0) if len(outs) > 1 else outs[0]


def _spatial_gate(*args):
    return list(_spatial_gate_stages(*args))[-1]


def _low_lanes():
    return lax.broadcasted_iota(jnp.int32, (1, LANES), 1) < HEAD_DIM


def _split_keys(k):
    low = _low_lanes()
    z = jnp.zeros_like(k)
    return jnp.where(low, k, z), jnp.where(low, z, k)


def _split_values(v):
    top, bot = _split_keys(v)
    low_ones = jnp.where(_low_lanes(), 1.0, 0.0)
    return (jnp.concatenate([top, jnp.broadcast_to(low_ones, v.shape).astype(BF16)], axis=1),
            jnp.concatenate([bot, jnp.broadcast_to(1.0 - low_ones, v.shape).astype(BF16)], axis=1))


def _stack_groups(q):
    return jnp.concatenate([q[:, g * LANES:(g + 1) * LANES] for g in range(ATTN_W // LANES)], axis=0)


def _sink_cols(sink_ref, rows):
    return [jnp.concatenate([jnp.broadcast_to(sink_ref[4 * kh + g:4 * kh + g + 1, 0:1], (rows, 1))
                             for g in range(4)], axis=0) for kh in range(KV_HEADS)]


def _pair_softmax(s, sinks):
    half = s.shape[1] // 2
    es, ts = [], []
    for kh in range(KV_HEADS):
        sh = s[:, kh * half:(kh + 1) * half]
        m = jnp.max(sh, axis=1, keepdims=True)
        es.append(jnp.exp2(sh - m).astype(BF16))
        ts.append(jnp.exp2(sinks[kh] - m))
    return jnp.concatenate(es, axis=1), ts


def _pair_normalize(o, ts):
    return o[:, :LANES] / (o[:, LANES:] + jnp.where(_low_lanes(), ts[0], ts[1]))


def _out_ffn_stages(x, o_cat, mod_row, w_out_ref, ln1g, ln1b, w1_ref, w2_ref, ln2g, ln2b):
    g_m, sh_f, sc_f, g_f = mod_row[2], mod_row[3], mod_row[4], mod_row[5]
    y = _dot(o_cat, w_out_ref[...])
    yield None
    x1 = _ln(ALPHA * x + g_m * y) * ln1g + ln1b
    h2 = (_ln(x1) * (1.0 + sc_f) + sh_f).astype(BF16)
    yield None
    y2 = None
    for lo, hi in zip(FFN_CHUNKS[:-1], FFN_CHUNKS[1:]):
        a = _dot(h2, w1_ref[:, lo:hi])
        b = _dot(h2, w1_ref[:, D_FF + lo:D_FF + hi])
        t = (_silu(a) * b).astype(BF16)
        part = _dot(t, w2_ref[lo:hi, :])
        y2 = part if y2 is None else y2 + part
        yield None
    yield _ln(ALPHA * x1 + g_f * y2) * ln2g + ln2b


OUT_FFN_STAGES = len(FFN_CHUNKS) + 2


def _run_skewed(gens, n_stages, skew):
    results = [None] * len(gens)
    for t in range(n_stages + skew * (len(gens) - 1)):
        for r, gen in enumerate(gens):
            if 0 <= t - r * skew < n_stages:
                results[r] = next(gen)
    return results


def _out_ffn(*args):
    return _run_skewed([_out_ffn_stages(*args)], OUT_FFN_STAGES, 0)[0]


def _layer_rows(layer, *refs):
    return [r.at[pl.ds(layer, 1)] for r in refs]


def _ada_kernel(cvt_ref, w_ref, b_ref, o_ref):
    s = _silu(cvt_ref[...])
    w = w_ref[...]
    bias = b_ref[pl.ds(pl.program_id(0), 1), :]
    o_ref[...] = jnp.zeros_like(o_ref)
    for r in range(N_COND):
        o_ref[r:r + 1, :] = jnp.sum(w * s[:, r:r + 1], axis=0, keepdims=True) + bias


def _wprep_kernel(w_in_ref, bdc_ref, wf_ref, o_ref):
    hp = functools.partial(jnp.dot, precision=lax.Precision.HIGHEST, preferred_element_type=F32)
    wf = wf_ref[...]
    fold = jnp.concatenate([hp(bdc_ref[:, :F_W], wf), hp(bdc_ref[:, F_W:], wf)], axis=1)
    for j, head in enumerate(HEAD_ORDER):
        o_ref[:, j * HEAD_DIM:(j + 1) * HEAD_DIM] = (
            w_in_ref[:, head * HEAD_DIM:(head + 1) * HEAD_DIM] * Q_SCALE).astype(BF16)
    o_ref[:, COL_K:COL_F] = w_in_ref[:, COL_K:COL_F].astype(BF16)
    o_ref[:, COL_F:COL_U] = hp(w_in_ref[:, COL_F:COL_F + F_W], fold).astype(BF16)
    o_ref[:, COL_U:] = w_in_ref[:, COL_F + F_W:].astype(BF16)


def _prep_kernel(cvt_ref, w_ada_ref, b_ada_ref, w_in_ref, bdc_ref, wf_ref, mods_ref, wcat_ref):
    _ada_kernel(cvt_ref, w_ada_ref, b_ada_ref, mods_ref)

    @pl.when(pl.program_id(1) == 0)
    def _():
        _wprep_kernel(w_in_ref, bdc_ref, wf_ref, wcat_ref)


def _cast_stages(pairs, n_stages):
    for s in range(n_stages):
        for src, dst in pairs:
            step = -(-src.shape[0] // (n_stages * 2 * SUBLANES)) * 2 * SUBLANES
            lo, hi = min(s * step, src.shape[0]), min((s + 1) * step, src.shape[0])
            if hi > lo:
                dst[lo:hi, :] = src[lo:hi, :].astype(BF16)
        yield


def _inproj_stages(x, sh, sc, w_ref, tabs, avg_ref, lng_ref, lnb_ref, wscat_ref, bs_ref,
                   q_ref, kd_ref, vd_ref, fx_ref, sg_ref, row0):
    rows = x.shape[0]
    rs = slice(row0, row0 + rows)
    h = (_ln(x) * (1.0 + sc) + sh).astype(BF16)
    yield
    for p in range(ATTN_W // MXU_TILE):
        qq = _project(h, w_ref, p * MXU_TILE, (p + 1) * MXU_TILE)
        for j in range(MXU_TILE // LANES):
            c0 = p * MXU_TILE + j * LANES
            q_ref[rs, c0:c0 + LANES] = _rope(qq[:, j * LANES:(j + 1) * LANES], tabs).astype(BF16)
    kv = _project(h, w_ref, COL_K, COL_F)
    kd_ref[rs, :] = _rope(kv[:, :KV_W], tabs).astype(BF16)
    vd_ref[rs, :] = kv[:, KV_W:].astype(BF16)
    yield
    f = _project(h, w_ref, COL_F, COL_U)
    for al in range(rows // DFT_B):
        for bb in range(B_BLOCKS):
            r0 = al * DFT_B + bb * SUBLANES
            a0 = (row0 // DFT_B + al) * SUBLANES
            for p in range(4):
                fx_ref[p, bb, a0:a0 + SUBLANES, :] = f[r0:r0 + SUBLANES, p * LANES:(p + 1) * LANES]
    u = _project(h, w_ref, COL_U, COL_G)
    g = _project(h, w_ref, COL_G, D_PROJ)
    gate = _spatial_gate_stages(u, g, avg_ref, lng_ref, lnb_ref, wscat_ref, bs_ref)
    next(gate)
    yield
    next(gate)
    yield
    sg_ref[rs, :] = next(gate).astype(BF16)
    yield


INPROJ_STAGES = 5


def _inproj_kernel(x_ref, mod_ref, w_ref, rt_ref, ct_ref, avg_ref, lng_ref, lnb_ref, wscat_ref,
                   bs_ref, w1_ref, w2_ref, q_ref, kd_ref, vd_ref, fx_ref, sg_ref, w1b_ref, w2b_ref,
                   *, n_sub, layer):
    lng_ref, lnb_ref = _layer_rows(layer, lng_ref, lnb_ref)
    tm = x_ref.shape[0]
    sub = tm // n_sub
    sh, sc = _mod_rows(mod_ref[...], 0)[:2]
    gens = []
    for r in range(n_sub):
        grid_row = pl.program_id(0) * (tm // GRID_W) + r * (sub // GRID_W)
        gens.append(_inproj_stages(
            x_ref[r * sub:(r + 1) * sub, :], sh, sc, w_ref, _rope_tile(rt_ref, ct_ref, grid_row, sub),
            avg_ref, lng_ref, lnb_ref, wscat_ref, bs_ref, q_ref, kd_ref, vd_ref, fx_ref, sg_ref,
            r * sub))
    gens.append(_cast_stages([(w1_ref, w1b_ref), (w2_ref, w2b_ref)], INPROJ_STAGES))
    _run_skewed(gens, INPROJ_STAGES, 1)


def _attn_block_stages(q_ref, k_ref, v_ref, kc_ref, vc_ref, sinks, o_ref, blk, row0, n_tokens):
    span = 3 * BLOCK
    n_ctx = kc_ref.shape[1]
    half = span + n_ctx
    start = pl.multiple_of(jnp.clip((blk - 1) * BLOCK, 0, n_tokens - span), BLOCK)
    k_top, k_bot = _split_keys(k_ref[pl.ds(start, span), :])
    v_top, v_bot = _split_values(v_ref[pl.ds(start, span), :])
    kk = jnp.concatenate([k_top, kc_ref[0], k_bot, kc_ref[1]], axis=0)
    vv = jnp.concatenate([v_top, vc_ref[0], v_bot, vc_ref[1]], axis=0)
    q4 = _stack_groups(q_ref[row0:row0 + BLOCK, :])
    yield
    s = lax.dot_general(q4, kk, (((1,), (1,)), ((), ())), preferred_element_type=F32)
    yield
    qpos = blk * BLOCK + lax.broadcasted_iota(jnp.int32, (BLOCK, span), 0)
    kpos = start + lax.broadcasted_iota(jnp.int32, (BLOCK, span), 1)
    ok = jnp.concatenate([jnp.abs(kpos - qpos) <= WINDOW] * 4, axis=0)
    s = jnp.concatenate([jnp.where(ok, s[:, :span], NEG_BIG), s[:, span:half],
                         jnp.where(ok, s[:, half:half + span], NEG_BIG), s[:, half + span:]], axis=1)
    e, ts = _pair_softmax(s, sinks)
    yield
    o = _dot(e, vv)
    yield
    res = _pair_normalize(o, ts).astype(BF16)
    for g in range(ATTN_W // LANES):
        o_ref[row0:row0 + BLOCK, g * LANES:(g + 1) * LANES] = res[g * BLOCK:(g + 1) * BLOCK]
    yield


ATTN_STAGES = 5


def _attn_kernel(q_ref, k_ref, v_ref, kc_ref, vc_ref, sink_ref, o_ref, *, n_tokens):
    n_blocks = q_ref.shape[0] // BLOCK
    sinks = _sink_cols(sink_ref, BLOCK)
    gens = [_attn_block_stages(q_ref, k_ref, v_ref, kc_ref, vc_ref, sinks, o_ref,
                               pl.program_id(0) * n_blocks + r, r * BLOCK, n_tokens)
            for r in range(n_blocks)]
    _run_skewed(gens, ATTN_STAGES, 1)


def _complex_rows(parts):
    return _split(jnp.concatenate([jnp.concatenate(parts[0:2], axis=1),
                                   jnp.concatenate(parts[2:4], axis=1)], axis=0))


def _dft_stage1_stages(x_ref, m_ref, tc_ref, ts_ref, y_scr, j, bl, row0):
    a = m_ref.shape[0] // 4
    x_hi, x_lo = _complex_rows([x_ref[i, j, pl.ds(bl, a, stride=SUBLANES), :] for i in range(4)])
    yield
    y = _dot3(m_ref[...], x_hi, x_lo)
    yield
    yr, yi = y[:a], y[a:]
    tc, ts = tc_ref[j, :, bl:bl + 1], ts_ref[j, :, bl:bl + 1]
    wr = yr * tc + yi * ts
    wi = yi * tc - yr * ts
    parts = (wr[:, :LANES], wr[:, LANES:], wi[:, :LANES], wi[:, LANES:])
    for kb in range(a // SUBLANES):
        for i in range(4):
            y_scr[i, kb, pl.ds(row0, SUBLANES), :] = parts[i][kb * SUBLANES:(kb + 1) * SUBLANES]
    yield


def _dft_stage2_stages(y_scr, kb, cs_ref, bf_ref, o_ref, j, kl):
    r_hi, r_lo = _complex_rows([y_scr[i, kb, pl.ds(kl, DFT_B, stride=SUBLANES), :] for i in range(4)])
    yield
    out = _dot3(cs_ref[...], r_hi, r_lo) + bf_ref[...]
    yield
    for hf in range(F_W // LANES):
        o_ref[hf, j, pl.ds(kl, DFT_B, stride=SUBLANES), :] = out[:, hf * LANES:(hf + 1) * LANES]
    yield


DFT_STAGES = 3


def _dft_kernel(x_ref, m_ref, tc_ref, ts_ref, cs_ref, bf_ref, o_ref, y_scr, *, layer):
    bf_ref, = _layer_rows(layer, bf_ref)
    step = pl.program_id(0)
    bps = x_ref.shape[1]
    n1 = B_BLOCKS // bps

    @pl.when(step < n1)
    def _():
        gens = [_dft_stage1_stages(
            x_ref, m_ref, tc_ref, ts_ref, y_scr, j, bl,
            pl.multiple_of(((step * bps + j) * SUBLANES + bl) * SUBLANES, SUBLANES))
            for j in range(bps) for bl in range(SUBLANES)]
        _run_skewed(gens, DFT_STAGES, 1)

    @pl.when(step >= n1)
    def _():
        gens = [_dft_stage2_stages(y_scr, (step - n1) * bps + j, cs_ref, bf_ref, o_ref, j, kl)
                for j in range(bps) for kl in range(SUBLANES)]
        _run_skewed(gens, DFT_STAGES, 1)


def _outffn_kernel(x_ref, oa_ref, of_ref, sg_ref, mod_ref, w_out_ref, ln1g_ref, ln1b_ref,
                   w1_ref, w2_ref, ln2g_ref, ln2b_ref, o_ref, *, n_sub, layer):
    ln1g_ref, ln1b_ref, ln2g_ref, ln2b_ref = _layer_rows(layer, ln1g_ref, ln1b_ref, ln2g_ref, ln2b_ref)
    n_kb, k2_rows = of_ref.shape[1], of_ref.shape[2]
    halves = []
    for hf in range(F_W // LANES):
        pieces = [of_ref[hf, kb, k2 * SUBLANES:(k2 + 1) * SUBLANES, :]
                  for k2 in range(k2_rows // SUBLANES) for kb in range(n_kb)]
        halves.append(jnp.concatenate(pieces, axis=0).astype(BF16))
    o_cat = jnp.concatenate([oa_ref[...]] + halves + [sg_ref[...]], axis=1)
    tm = x_ref.shape[0]
    sub = tm // n_sub
    mod_row = _mod_rows(mod_ref[...], 0)
    gens = [_out_ffn_stages(x_ref[r * sub:(r + 1) * sub, :], o_cat[r * sub:(r + 1) * sub], mod_row,
                            w_out_ref, ln1g_ref[...], ln1b_ref[...], w1_ref, w2_ref,
                            ln2g_ref[...], ln2b_ref[...]) for r in range(n_sub)]
    for r, res in enumerate(_run_skewed(gens, OUT_FFN_STAGES, 1)):
        o_ref[r * sub:(r + 1) * sub, :] = res


def _ctx_kernel(xc_ref, mod0_ref, mod1_ref, w_in0_ref, w_in1_ref, avg_ref,
                lng_ref, lnb_ref, wscat_ref, bs_ref, sink_ref, cs_ref, bf_ref,
                w_out_ref, ln1g_ref, ln1b_ref, w1_ref, w2_ref, ln2g_ref, ln2b_ref,
                kc0_ref, vc0_ref, kc1_ref, vc1_ref):
    lng_ref, lnb_ref, bf_ref, ln1g_ref, ln1b_ref, ln2g_ref, ln2b_ref = _layer_rows(
        0, lng_ref, lnb_ref, bf_ref, ln1g_ref, ln1b_ref, ln2g_ref, ln2b_ref)
    xc = xc_ref[...]
    n_ctx = xc.shape[0]
    m0 = _mod_rows(mod0_ref[...], 1)
    h = (_ln(xc) * (1.0 + m0[1]) + m0[0]).astype(BF16)
    q4 = _stack_groups(_project(h, w_in0_ref, 0, COL_K).astype(BF16))
    k_parts = _split_keys(_project(h, w_in0_ref, COL_K, COL_V).astype(BF16))
    v_parts = _split_values(_project(h, w_in0_ref, COL_V, COL_F).astype(BF16))
    kc0_ref[0], kc0_ref[1] = k_parts
    vc0_ref[0], vc0_ref[1] = v_parts
    kk = jnp.concatenate(k_parts, axis=0)
    vv = jnp.concatenate(v_parts, axis=0)
    s = lax.dot_general(q4, kk, (((1,), (1,)), ((), ())), preferred_element_type=F32)
    e, ts = _pair_softmax(s, _sink_cols(sink_ref, n_ctx))
    o4 = _pair_normalize(_dot(e, vv), ts)
    pairs = [o4[g * n_ctx:(g + 1) * n_ctx] for g in range(ATTN_W // LANES)]
    f = _project(h, w_in0_ref, COL_F, COL_U)
    y_hi, y_lo = _split(jnp.concatenate([f[:, :F_W], f[:, F_W:]], axis=0))
    o_f = _dot3(cs_ref[...], y_hi, y_lo) + bf_ref[...]
    u = _project(h, w_in0_ref, COL_U, COL_G)
    g = _project(h, w_in0_ref, COL_G, D_PROJ)
    sg = _spatial_gate(u, g, avg_ref, lng_ref, lnb_ref, wscat_ref, bs_ref)
    o_cat = jnp.concatenate([p.astype(BF16) for p in pairs] + [o_f.astype(BF16), sg.astype(BF16)],
                            axis=1)
    xc1 = _out_ffn(xc, o_cat, m0, w_out_ref, ln1g_ref[...], ln1b_ref[...], w1_ref, w2_ref,
                   ln2g_ref[...], ln2b_ref[...])
    m1 = _mod_rows(mod1_ref[...], 1)
    h1 = (_ln(xc1) * (1.0 + m1[1]) + m1[0]).astype(BF16)
    kc1_ref[0], kc1_ref[1] = _split_keys(_project(h1, w_in1_ref, COL_K, COL_V).astype(BF16))
    vc1_ref[0], vc1_ref[1] = _split_values(_project(h1, w_in1_ref, COL_V, COL_F).astype(BF16))


def _hi_lo(a):
    a32 = jnp.asarray(np.asarray(a, np.float64), F32)
    hi = a32.astype(BF16)
    return jnp.concatenate([hi, (a32 - hi.astype(F32)).astype(BF16)], axis=0)


def _dft_cos_sin(n):
    idx = np.arange(n, dtype=np.int64)
    ang = 2.0 * np.pi * ((idx[:, None] * idx[None, :]) % n) / n
    return np.cos(ang), np.sin(ang)


def _channel_dft_tables():
    c, s = _dft_cos_sin(F_DIM)
    bd = np.zeros((F_W, 2 * F_W))
    for g in range(F_GROUPS):
        sl = slice(g * F_DIM, (g + 1) * F_DIM)
        bd[sl, sl] = c / math.sqrt(F_DIM)
        bd[sl, F_W + g * F_DIM:F_W + (g + 1) * F_DIM] = -s / math.sqrt(F_DIM)
    return jnp.asarray(bd, F32)


def _group_avg_table():
    a = np.zeros((C_W, C_W))
    for h in range(C_HEADS):
        a[h * C_DIM:(h + 1) * C_DIM, h * C_DIM:(h + 1) * C_DIM] = 1.0 / C_DIM
    return jnp.asarray(np.concatenate([a, a], axis=0), BF16)


def _rope_tables(n):
    freqs = jnp.asarray(ROPE_BASE, F32) ** (-jnp.arange(0, AXIS_DIM, 2, dtype=F32) / AXIS_DIM)
    reps = LANES // HEAD_DIM

    def tables(pos, row_axis):
        ang = pos[:, None] * freqs
        c, s = jnp.cos(ang), jnp.sin(ang)
        z = jnp.zeros_like(s)
        pad = [z, z]
        pick = (lambda t: t + pad) if row_axis else (lambda t: pad + t)
        return jnp.stack([jnp.tile(jnp.concatenate(pick(t), axis=1), (1, reps))
                          for t in ([c, c], [-s, z], [z, s])])

    rt = tables(jnp.arange(n // GRID_W, dtype=F32), True)
    ct = tables(jnp.arange(GRID_W, dtype=F32), False)
    return rt, ct


def _stage1_tables(a):
    n = a * DFT_B
    c, s = _dft_cos_sin(a)
    m = _hi_lo(np.block([[c, s], [-s, c]]) / math.sqrt(a))
    k1 = np.arange(a, dtype=np.int64)[None, :, None]
    b = (np.arange(B_BLOCKS, dtype=np.int64)[:, None, None] * SUBLANES
         + np.arange(SUBLANES, dtype=np.int64)[None, None, :])
    ang = 2.0 * np.pi * ((k1 * b) % n) / n
    return m, jnp.asarray(np.cos(ang), F32), jnp.asarray(np.sin(ang), F32)


def _stage2_tables():
    c, s = _dft_cos_sin(DFT_B)
    return _hi_lo(np.concatenate([c, s], axis=1) / math.sqrt(DFT_B))


def _ctx_dft_tables(n_ctx):
    c, s = _dft_cos_sin(n_ctx)
    return _hi_lo(np.concatenate([c, s], axis=1) / math.sqrt(n_ctx))


def _const_spec(shape):
    nd = len(shape)
    return pl.BlockSpec(shape, lambda *_: (0,) * nd, pipeline_mode=pl.Buffered(1))


def _layer_spec(shape, layer):
    if len(shape) == 2:
        return _const_spec(shape)
    nd = len(shape) - 1
    return pl.BlockSpec((None,) + tuple(shape[1:]), lambda *_: (layer,) + (0,) * nd,
                        pipeline_mode=pl.Buffered(1))


def _params(*sem):
    return pltpu.CompilerParams(dimension_semantics=sem, vmem_limit_bytes=VMEM_LIMIT)


def _prep(cvt, w_ada, b_ada, w_in, bdc, wf_bd):
    depth = w_in.shape[0]
    tn = 6 * D_MODEL // PREP_STEPS
    per_layer = lambda shape: pl.BlockSpec((None,) + tuple(shape[1:]), lambda l, j: (l, 0, 0))
    return pl.pallas_call(
        _prep_kernel,
        grid=(depth, PREP_STEPS),
        in_specs=[pl.BlockSpec((D_MODEL, SUBLANES), lambda l, j: (0, 0)),
                  pl.BlockSpec((None, D_MODEL, tn), lambda l, j: (l, 0, j)),
                  pl.BlockSpec((depth, tn), lambda l, j: (0, j)),
                  per_layer(w_in.shape), pl.BlockSpec(bdc.shape, lambda l, j: (0, 0)),
                  per_layer(wf_bd.shape)],
        out_specs=[pl.BlockSpec((None, SUBLANES, tn), lambda l, j: (l, 0, j)),
                   per_layer((depth, D_MODEL, D_PROJ))],
        out_shape=[jax.ShapeDtypeStruct((depth, SUBLANES, 6 * D_MODEL), F32),
                   jax.ShapeDtypeStruct((depth, D_MODEL, D_PROJ), BF16)],
        compiler_params=_params("arbitrary", "arbitrary"),
        name="prep",
    )(cvt, w_ada, b_ada, w_in, bdc, wf_bd)


def _inproj(x, mods, w_in, layer, rope, shared, sgu, w1, w2, tm, n_sub):
    n = x.shape[0]
    a = n // DFT_B
    steps = n // tm
    row = lambda w: pl.BlockSpec((tm, w), lambda i: (i, 0))
    chunk = lambda w: pl.BlockSpec((None, w.shape[1] // steps, w.shape[2]), lambda i: (layer, i, 0))
    consts = list(rope) + list(shared)
    fx_rows = tm // DFT_B * SUBLANES
    q, kd, vd, fx, sg, w1b, w2b = pl.pallas_call(
        functools.partial(_inproj_kernel, n_sub=n_sub, layer=layer),
        grid=(steps,),
        in_specs=[row(D_MODEL), _layer_spec(mods.shape, layer), _layer_spec(w_in.shape, layer)]
        + [_const_spec(c.shape) for c in consts] + [_layer_spec(c.shape, layer) for c in sgu]
        + [chunk(w1), chunk(w2)],
        out_specs=[row(ATTN_W), row(KV_W), row(KV_W),
                   pl.BlockSpec((4, B_BLOCKS, fx_rows, LANES), lambda i: (0, 0, i, 0)), row(C_W)]
        + [pl.BlockSpec((w.shape[1] // steps, w.shape[2]), lambda i: (i, 0)) for w in (w1, w2)],
        out_shape=[jax.ShapeDtypeStruct((n, ATTN_W), BF16), jax.ShapeDtypeStruct((n, KV_W), BF16),
                   jax.ShapeDtypeStruct((n, KV_W), BF16),
                   jax.ShapeDtypeStruct((4, B_BLOCKS, a * SUBLANES, LANES), F32),
                   jax.ShapeDtypeStruct((n, C_W), BF16)]
        + [jax.ShapeDtypeStruct(w.shape[1:], BF16) for w in (w1, w2)],
        compiler_params=_params("arbitrary"),
        name="inproj",
    )(x, mods, w_in, *consts, *sgu, w1, w2)
    return q, kd, vd, fx, sg, w1b, w2b


def _attention(q, kd, vd, kc, vc, sink_tab, layer, tq):
    n = q.shape[0]
    return pl.pallas_call(
        functools.partial(_attn_kernel, n_tokens=n),
        grid=(n // tq,),
        in_specs=[pl.BlockSpec((tq, ATTN_W), lambda i: (i, 0)), _const_spec(kd.shape),
                  _const_spec(vd.shape), _const_spec(kc.shape), _const_spec(vc.shape),
                  _layer_spec(sink_tab.shape, layer)],
        out_specs=pl.BlockSpec((tq, ATTN_W), lambda i: (i, 0)),
        out_shape=jax.ShapeDtypeStruct((n, ATTN_W), BF16),
        compiler_params=_params("arbitrary"),
        name="attention",
    )(q, kd, vd, kc, vc, sink_tab)


def _fourier(fx, tabs1, tabs2, bf_row, layer):
    a = fx.shape[2] // SUBLANES
    kb = a // SUBLANES
    m, tc, ts = tabs1
    bps = math.gcd(DFT_BLOCKS_PER_STEP, kb)
    n1 = B_BLOCKS // bps
    in_blk = lambda s: jnp.minimum(s, n1 - 1)
    out_blk = lambda s: jnp.maximum(s - n1, 0)
    tw = pl.BlockSpec((bps, a, SUBLANES), lambda s: (in_blk(s), 0, 0))
    return pl.pallas_call(
        functools.partial(_dft_kernel, layer=layer),
        grid=(n1 + kb // bps,),
        in_specs=[pl.BlockSpec((4, bps, a * SUBLANES, LANES), lambda s: (0, in_blk(s), 0, 0)),
                  _const_spec(m.shape), tw, tw, _const_spec(tabs2.shape),
                  _layer_spec(bf_row.shape, layer)],
        out_specs=pl.BlockSpec((F_W // LANES, bps, DFT_B * SUBLANES, LANES),
                               lambda s: (0, out_blk(s), 0, 0)),
        out_shape=jax.ShapeDtypeStruct((F_W // LANES, kb, DFT_B * SUBLANES, LANES), F32),
        scratch_shapes=[pltpu.VMEM((4, kb, DFT_B * SUBLANES, LANES), F32)],
        compiler_params=_params("arbitrary"),
        name="dft",
    )(fx, m, tc, ts, tabs2, bf_row)


def _outffn(x, oa, of, sg, mods, weights, layer, tm, n_sub):
    n = x.shape[0]
    a = n // DFT_B
    row = lambda w: pl.BlockSpec((tm, w), lambda i: (i, 0))
    of_spec = pl.BlockSpec((F_W // LANES, a // SUBLANES, tm // a * SUBLANES, LANES),
                           lambda i: (0, 0, i, 0))
    consts = [mods] + list(weights)
    return pl.pallas_call(
        functools.partial(_outffn_kernel, n_sub=n_sub, layer=layer),
        grid=(n // tm,),
        in_specs=[row(D_MODEL), row(ATTN_W), of_spec, row(C_W)]
        + [_layer_spec(c.shape, layer) for c in consts],
        out_specs=row(D_MODEL),
        out_shape=jax.ShapeDtypeStruct((n, D_MODEL), F32),
        compiler_params=_params("arbitrary"),
        name="outffn",
    )(x, oa, of, sg, *consts)


def _context(xc, mods, w_in, shared, sgu, sink_tab, ctx_tabs, bf_row, weights):
    n_ctx = xc.shape[0]
    args = ([(xc, None), (mods, 0), (mods, 1), (w_in, 0), (w_in, 1)] + [(t, None) for t in shared]
            + [(t, 0) for t in sgu] + [(sink_tab, 0), (ctx_tabs, None)]
            + [(bf_row, 0)] + [(t, 0) for t in weights])
    outs = [jax.ShapeDtypeStruct((KV_HEADS, n_ctx, w), BF16) for w in (LANES, 2 * LANES)] * 2
    return pl.pallas_call(
        _ctx_kernel,
        grid=(1,),
        in_specs=[_const_spec(t.shape) if l is None else _layer_spec(t.shape, l) for t, l in args],
        out_specs=[pl.BlockSpec(o.shape, lambda i: (0, 0, 0)) for o in outs],
        out_shape=outs,
        compiler_params=_params("arbitrary"),
        name="context",
    )(*[t for t, _ in args])


def _block_diag_wf(w_f):
    bd = jnp.zeros((w_f.shape[0], F_W, F_W), F32)
    for g in range(F_GROUPS):
        bd = bd.at[:, g * F_DIM:(g + 1) * F_DIM, g * F_DIM:(g + 1) * F_DIM].set(w_f[:, g])
    return bd


def _forward(x, c, ctx, c_ctx, w_ada, b_ada, w_in, w_out, attn_sink, w_fourier, b_fourier,
             sgu_ln_g, sgu_ln_b, w_spatial, b_spatial, ln1_g, ln1_b, w_ffn_in, w_ffn_out,
             ln2_g, ln2_b, tm_in=2048, n_sub_in=8, tq=2048, tm_out=1024, n_sub=4):
    n = x.shape[1]
    n_ctx = ctx.shape[1]
    depth = w_in.shape[0]
    assert x.shape[0] == 1 and n % max(tm_in, tq, tm_out) == 0 and n >= 3 * BLOCK
    assert (n // DFT_B) % SUBLANES == 0 and tm_out % (n // DFT_B) == 0 and n_ctx % CHUNK == 0
    xs = x[0]
    cvt = jnp.concatenate([c[0][:, None], c_ctx[:, None],
                           jnp.zeros((D_MODEL, SUBLANES - N_COND), F32)], axis=1)
    mods, w_in_b = _prep(cvt, w_ada, b_ada, w_in, _channel_dft_tables(), _block_diag_wf(w_fourier))
    w_out_b = jnp.concatenate([w_out[:, h * HEAD_DIM:(h + 1) * HEAD_DIM, :] for h in HEAD_ORDER]
                              + [w_out[:, ATTN_W:, :]], axis=1).astype(BF16)
    shared = (_group_avg_table(),)
    rope = _rope_tables(n)
    tabs1 = _stage1_tables(n // DFT_B)
    tabs2 = _stage2_tables()
    ctx_tabs = _ctx_dft_tables(n_ctx)
    sgu = (sgu_ln_g.reshape(depth, C_W), sgu_ln_b.reshape(depth, C_W),
           jnp.concatenate([w_spatial[:, h] for h in range(C_HEADS)], axis=2).astype(BF16),
           jnp.repeat(jnp.swapaxes(b_spatial, 1, 2), C_DIM, axis=2))
    sink_tab = jnp.broadcast_to((attn_sink * math.log2(math.e))[:, :, None], (depth, ATTN_HEADS, LANES))
    bf_row = b_fourier.reshape(depth, F_W)

    ctx_kv = None
    for l in range(depth):
        q, kd, vd, fx, sg, w1_b, w2_b = _inproj(xs, mods, w_in_b, l, rope, shared, sgu,
                                                w_ffn_in, w_ffn_out, tm_in, n_sub_in)
        weights = (w_out_b, ln1_g, ln1_b, w1_b, w2_b, ln2_g, ln2_b)
        if ctx_kv is None:
            ctx_kv = _context(ctx[0], mods, w_in_b, shared, sgu, sink_tab, ctx_tabs, bf_row, weights)
        oa = _attention(q, kd, vd, ctx_kv[2 * l], ctx_kv[2 * l + 1], sink_tab, l, tq)
        of = _fourier(fx, tabs1, tabs2, bf_row, l)
        xs = _outffn(xs, oa, of, sg, mods, weights, l, tm_out, n_sub)
    return xs[None]


def kernel(x, c, ctx, c_ctx, w_ada, b_ada, w_in, w_out, attn_sink, w_fourier, b_fourier, sgu_ln_g,
           sgu_ln_b, w_spatial, b_spatial, ln1_g, ln1_b, w_ffn_in, w_ffn_out, ln2_g, ln2_b):
    return _forward(x, c, ctx, c_ctx, w_ada, b_ada, w_in, w_out, attn_sink, w_fourier, b_fourier,
                    sgu_ln_g, sgu_ln_b, w_spatial, b_spatial, ln1_g, ln1_b, w_ffn_in, w_ffn_out,
                    ln2_g, ln2_b)
```

```python
import functools
import math

import numpy as np
import jax
import jax.numpy as jnp
from jax import lax
from jax.experimental import pallas as pl
from jax.experimental.pallas import tpu as pltpu

F32 = jnp.float32
BF16 = jnp.bfloat16

D_MODEL = 1024
DEPTH = 2
GRID_W = 64
HEAD_DIM = 64
ATTN_HEADS = 8
KV_HEADS = 2
WINDOW = 128
BLOCK = 128
ROPE_BASE = 10000.0
AXIS_DIM = HEAD_DIM // 2
F_GROUPS = 4
F_DIM = 64
C_HEADS = 4
C_DIM = 64
CHUNK = 128
D_FF = 2816
ATTN_W = ATTN_HEADS * HEAD_DIM
KV_W = KV_HEADS * HEAD_DIM
F_W = F_GROUPS * F_DIM
C_W = C_HEADS * C_DIM
COL_K = ATTN_W
COL_V = COL_K + KV_W
COL_F = COL_V + KV_W
COL_U = COL_F + 2 * F_W
COL_G = COL_U + C_W
D_PROJ = COL_G + C_W
N_COND = 2
GQA_GROUP = ATTN_HEADS // KV_HEADS
HEAD_ORDER = tuple(kv * GQA_GROUP + g for g in range(GQA_GROUP) for kv in range(KV_HEADS))
Q_SCALE = HEAD_DIM ** -0.5 * math.log2(math.e)
ALPHA = (2 * DEPTH) ** 0.25
LN_EPS = 1e-6
NEG_BIG = -1e30

LANES = 128
SUBLANES = 8
MXU_TILE = 256
FFN_CHUNKS = (0, 4 * MXU_TILE, 8 * MXU_TILE, D_FF)
PREP_STEPS = 4
DFT_B = 128
B_BLOCKS = DFT_B // SUBLANES
DFT_BLOCKS_PER_STEP = 2
VMEM_LIMIT = 56 * 1024 * 1024


def _dot(a, b):
    return jnp.dot(a, b, preferred_element_type=F32)


def _split(a):
    hi = a.astype(BF16)
    lo = (a - hi.astype(F32)).astype(BF16)
    return hi, lo


def _dot3(a_cat, b_hi, b_lo):
    rows = a_cat.shape[0] // 2
    both = _dot(a_cat, b_hi)
    return both[:rows] + both[rows:] + _dot(a_cat[:rows], b_lo)


def _ln(x):
    mu = jnp.mean(x, axis=-1, keepdims=True)
    xc = x - mu
    var = jnp.mean(xc * xc, axis=-1, keepdims=True)
    return xc * lax.rsqrt(var + LN_EPS)


def _gelu(x):
    return 0.5 * x * (1.0 + jnp.tanh(math.sqrt(2.0 / math.pi) * (x + 0.044715 * (x * x * x))))


def _silu(x):
    return x / (1.0 + jnp.exp(-x))


def _mod_rows(mod, row):
    return [mod[row:row + 1, i * D_MODEL:(i + 1) * D_MODEL] for i in range(6)]


def _rope(t, tabs):
    cos, sin_a, sin_b = tabs
    return (t * cos + pltpu.roll(t, LANES - AXIS_DIM // 2, 1) * sin_a
            + pltpu.roll(t, AXIS_DIM // 2, 1) * sin_b)


def _rope_tile(rt_ref, ct_ref, first_grid_row, rows):
    tabs = []
    for t in range(3):
        groups = [rt_ref[t, pl.ds(first_grid_row + r, 1), :] + ct_ref[t]
                  for r in range(rows // GRID_W)]
        tabs.append(jnp.concatenate(groups, axis=0) if len(groups) > 1 else groups[0])
    return tabs


def _project(h, w_in_ref, lo, hi):
    return _dot(h, w_in_ref[:, lo:hi])


def _spatial_gate_stages(u, g, avg_ref, lng_ref, lnb_ref, wscat_ref, bs_ref):
    rows = u.shape[0]
    ug = _gelu(u)
    vg = _gelu(g)
    avg = avg_ref[...]
    mu = _dot(jnp.concatenate(_split(vg), axis=1), avg)
    yield None
    vc = vg - mu
    var = _dot(jnp.concatenate(_split(vc * vc), axis=1), avg)
    yield None
    vn = vc * lax.rsqrt(var + LN_EPS) * lng_ref[...] + lnb_ref[...]
    lane = lax.broadcasted_iota(jnp.int32, (1, C_W), 1)
    vnb = vn.astype(BF16)
    zero = jnp.zeros_like(vnb)
    outs = []
    for c in range(rows // CHUNK):
        blk = vnb[c * CHUNK:(c + 1) * CHUNK]
        rhs = jnp.concatenate(
            [jnp.where((lane >= h * C_DIM) & (lane < (h + 1) * C_DIM), blk, zero[:CHUNK])
             for h in range(C_HEADS)], axis=0)
        vs = _dot(wscat_ref[...], rhs) + bs_ref[...]
        outs.append(ug[c * CHUNK:(c + 1) * CHUNK] * vs)
    yield jnp.concatenate(outs, axis=0) if len(outs) > 1 else outs[0]


def _spatial_gate(*args):
    return list(_spatial_gate_stages(*args))[-1]


def _low_lanes():
    return lax.broadcasted_iota(jnp.int32, (1, LANES), 1) < HEAD_DIM


def _split_keys(k):
    low = _low_lanes()
    z = jnp.zeros_like(k)
    return jnp.where(low, k, z), jnp.where(low, z, k)


def _split_values(v):
    top, bot = _split_keys(v)
    low_ones = jnp.where(_low_lanes(), 1.0, 0.0)
    return (jnp.concatenate([top, jnp.broadcast_to(low_ones, v.shape).astype(BF16)], axis=1),
            jnp.concatenate([bot, jnp.broadcast_to(1.0 - low_ones, v.shape).astype(BF16)], axis=1))


def _stack_groups(q):
    return jnp.concatenate([q[:, g * LANES:(g + 1) * LANES] for g in range(ATTN_W // LANES)], axis=0)


def _sink_cols(sink_ref, rows):
    return [jnp.concatenate([jnp.broadcast_to(sink_ref[4 * kh + g:4 * kh + g + 1, 0:1], (rows, 1))
                             for g in range(4)], axis=0) for kh in range(KV_HEADS)]


def _pair_softmax(s, sinks):
    half = s.shape[1] // 2
    es, ts = [], []
    for kh in range(KV_HEADS):
        sh = s[:, kh * half:(kh + 1) * half]
        m = jnp.max(sh, axis=1, keepdims=True)
        es.append(jnp.exp2(sh - m).astype(BF16))
        ts.append(jnp.exp2(sinks[kh] - m))
    return jnp.concatenate(es, axis=1), ts


def _pair_normalize(o, ts):
    return o[:, :LANES] / (o[:, LANES:] + jnp.where(_low_lanes(), ts[0], ts[1]))


def _out_ffn_stages(x, o_cat, mod_row, w_out_ref, ln1g, ln1b, w1_ref, w2_ref, ln2g, ln2b):
    g_m, sh_f, sc_f, g_f = mod_row[2], mod_row[3], mod_row[4], mod_row[5]
    y = _dot(o_cat, w_out_ref[...])
    yield None
    x1 = _ln(ALPHA * x + g_m * y) * ln1g + ln1b
    h2 = (_ln(x1) * (1.0 + sc_f) + sh_f).astype(BF16)
    yield None
    y2 = None
    for lo, hi in zip(FFN_CHUNKS[:-1], FFN_CHUNKS[1:]):
        a = _dot(h2, w1_ref[:, lo:hi])
        b = _dot(h2, w1_ref[:, D_FF + lo:D_FF + hi])
        t = (_silu(a) * b).astype(BF16)
        part = _dot(t, w2_ref[lo:hi, :])
        y2 = part if y2 is None else y2 + part
        yield None
    yield _ln(ALPHA * x1 + g_f * y2) * ln2g + ln2b


OUT_FFN_STAGES = len(FFN_CHUNKS) + 2


def _run_skewed(gens, n_stages, skew):
    results = [None] * len(gens)
    for t in range(n_stages + skew * (len(gens) - 1)):
        for r, gen in enumerate(gens):
            if 0 <= t - r * skew < n_stages:
                results[r] = next(gen)
    return results


def _out_ffn(*args):
    return _run_skewed([_out_ffn_stages(*args)], OUT_FFN_STAGES, 0)[0]


def _layer_rows(layer, *refs):
    return [r.at[pl.ds(layer, 1)] for r in refs]


def _ada_kernel(cvt_ref, w_ref, b_ref, o_ref):
    s = _silu(cvt_ref[...])
    w = w_ref[...]
    bias = b_ref[pl.ds(pl.program_id(0), 1), :]
    o_ref[...] = jnp.zeros_like(o_ref)
    for r in range(N_COND):
        o_ref[r:r + 1, :] = jnp.sum(w * s[:, r:r + 1], axis=0, keepdims=True) + bias


def _wprep_kernel(w_in_ref, bdc_ref, wf_ref, o_ref):
    hp = functools.partial(jnp.dot, precision=lax.Precision.HIGHEST, preferred_element_type=F32)
    wf = wf_ref[...]
    fold = jnp.concatenate([hp(bdc_ref[:, :F_W], wf), hp(bdc_ref[:, F_W:], wf)], axis=1)
    for j, head in enumerate(HEAD_ORDER):
        o_ref[:, j * HEAD_DIM:(j + 1) * HEAD_DIM] = (
            w_in_ref[:, head * HEAD_DIM:(head + 1) * HEAD_DIM] * Q_SCALE).astype(BF16)
    o_ref[:, COL_K:COL_F] = w_in_ref[:, COL_K:COL_F].astype(BF16)
    o_ref[:, COL_F:COL_U] = hp(w_in_ref[:, COL_F:COL_F + F_W], fold).astype(BF16)
    o_ref[:, COL_U:] = w_in_ref[:, COL_F + F_W:].astype(BF16)


def _prep_kernel(cvt_ref, w_ada_ref, b_ada_ref, w_in_ref, bdc_ref, wf_ref, mods_ref, wcat_ref):
    _ada_kernel(cvt_ref, w_ada_ref, b_ada_ref, mods_ref)

    @pl.when(pl.program_id(1) == 0)
    def _():
        _wprep_kernel(w_in_ref, bdc_ref, wf_ref, wcat_ref)


def _cast_stages(pairs, n_stages):
    for s in range(n_stages):
        for src, dst in pairs:
            step = -(-src.shape[0] // (n_stages * 2 * SUBLANES)) * 2 * SUBLANES
            lo, hi = min(s * step, src.shape[0]), min((s + 1) * step, src.shape[0])
            if hi > lo:
                dst[lo:hi, :] = src[lo:hi, :].astype(BF16)
        yield


def _inproj_stages(x, sh, sc, w_ref, tabs, avg_ref, lng_ref, lnb_ref, wscat_ref, bs_ref,
                   q_ref, kd_ref, vd_ref, fx_ref, sg_ref, row0):
    rows = x.shape[0]
    rs = slice(row0, row0 + rows)
    h = (_ln(x) * (1.0 + sc) + sh).astype(BF16)
    yield
    for p in range(ATTN_W // MXU_TILE):
        qq = _project(h, w_ref, p * MXU_TILE, (p + 1) * MXU_TILE)
        for j in range(MXU_TILE // LANES):
            c0 = p * MXU_TILE + j * LANES
            q_ref[rs, c0:c0 + LANES] = _rope(qq[:, j * LANES:(j + 1) * LANES], tabs).astype(BF16)
    kv = _project(h, w_ref, COL_K, COL_F)
    kd_ref[rs, :] = _rope(kv[:, :KV_W], tabs).astype(BF16)
    vd_ref[rs, :] = kv[:, KV_W:].astype(BF16)
    yield
    f = _project(h, w_ref, COL_F, COL_U)
    for al in range(rows // DFT_B):
        for bb in range(B_BLOCKS):
            r0 = al * DFT_B + bb * SUBLANES
            a0 = (row0 // DFT_B + al) * SUBLANES
            for p in range(4):
                fx_ref[p, bb, a0:a0 + SUBLANES, :] = f[r0:r0 + SUBLANES, p * LANES:(p + 1) * LANES]
    u = _project(h, w_ref, COL_U, COL_G)
    g = _project(h, w_ref, COL_G, D_PROJ)
    gate = _spatial_gate_stages(u, g, avg_ref, lng_ref, lnb_ref, wscat_ref, bs_ref)
    next(gate)
    yield
    next(gate)
    yield
    sg_ref[rs, :] = next(gate).astype(BF16)
    yield


INPROJ_STAGES = 5


def _with_cast_refs(refs, n_out):
    if len(refs) == n_out:
        return refs, []
    return refs[2:2 + n_out], [(refs[0], refs[-2]), (refs[1], refs[-1])]


def _inproj_kernel(x_ref, mod_ref, w_ref, rt_ref, ct_ref, avg_ref, lng_ref, lnb_ref, wscat_ref,
                   bs_ref, *refs, n_sub, layer):
    (q_ref, kd_ref, vd_ref, fx_ref, sg_ref), cast = _with_cast_refs(refs, 5)
    lng_ref, lnb_ref = _layer_rows(layer, lng_ref, lnb_ref)
    tm = x_ref.shape[0]
    sub = tm // n_sub
    sh, sc = _mod_rows(mod_ref[...], 0)[:2]
    gens = []
    for r in range(n_sub):
        grid_row = pl.program_id(0) * (tm // GRID_W) + r * (sub // GRID_W)
        gens.append(_inproj_stages(
            x_ref[r * sub:(r + 1) * sub, :], sh, sc, w_ref, _rope_tile(rt_ref, ct_ref, grid_row, sub),
            avg_ref, lng_ref, lnb_ref, wscat_ref, bs_ref, q_ref, kd_ref, vd_ref, fx_ref, sg_ref,
            r * sub))
    if cast:
        gens.append(_cast_stages(cast, INPROJ_STAGES))
    _run_skewed(gens, INPROJ_STAGES, 1)


def _attn_block_stages(q_ref, k_ref, v_ref, kc_ref, vc_ref, sinks, o_ref, blk, row0, n_tokens):
    span = 3 * BLOCK
    n_ctx = kc_ref.shape[1]
    half = span + n_ctx
    start = pl.multiple_of(jnp.clip((blk - 1) * BLOCK, 0, n_tokens - span), BLOCK)
    k_top, k_bot = _split_keys(k_ref[pl.ds(start, span), :])
    v_top, v_bot = _split_values(v_ref[pl.ds(start, span), :])
    kk = jnp.concatenate([k_top, kc_ref[0], k_bot, kc_ref[1]], axis=0)
    vv = jnp.concatenate([v_top, vc_ref[0], v_bot, vc_ref[1]], axis=0)
    q4 = _stack_groups(q_ref[row0:row0 + BLOCK, :])
    yield
    s = lax.dot_general(q4, kk, (((1,), (1,)), ((), ())), preferred_element_type=F32)
    yield
    qpos = blk * BLOCK + lax.broadcasted_iota(jnp.int32, (BLOCK, span), 0)
    kpos = start + lax.broadcasted_iota(jnp.int32, (BLOCK, span), 1)
    ok = jnp.concatenate([jnp.abs(kpos - qpos) <= WINDOW] * 4, axis=0)
    s = jnp.concatenate([jnp.where(ok, s[:, :span], NEG_BIG), s[:, span:half],
                         jnp.where(ok, s[:, half:half + span], NEG_BIG), s[:, half + span:]], axis=1)
    e, ts = _pair_softmax(s, sinks)
    yield
    o = _dot(e, vv)
    yield
    res = _pair_normalize(o, ts).astype(BF16)
    for g in range(ATTN_W // LANES):
        o_ref[row0:row0 + BLOCK, g * LANES:(g + 1) * LANES] = res[g * BLOCK:(g + 1) * BLOCK]
    yield


ATTN_STAGES = 5


def _attn_kernel(q_ref, k_ref, v_ref, kc_ref, vc_ref, sink_ref, *refs, n_tokens):
    (o_ref,), cast = _with_cast_refs(refs, 1)
    n_blocks = q_ref.shape[0] // BLOCK
    sinks = _sink_cols(sink_ref, BLOCK)
    gens = [_attn_block_stages(q_ref, k_ref, v_ref, kc_ref, vc_ref, sinks, o_ref,
                               pl.program_id(0) * n_blocks + r, r * BLOCK, n_tokens)
            for r in range(n_blocks)]
    if cast:
        gens.append(_cast_stages(cast, ATTN_STAGES))
    _run_skewed(gens, ATTN_STAGES, 1)


def _complex_rows(parts):
    return _split(jnp.concatenate([jnp.concatenate(parts[0:2], axis=1),
                                   jnp.concatenate(parts[2:4], axis=1)], axis=0))


def _dft_stage1_stages(x_ref, m_ref, tc_ref, ts_ref, y_scr, j, bl, row0):
    a = m_ref.shape[0] // 4
    x_hi, x_lo = _complex_rows([x_ref[i, j, pl.ds(bl, a, stride=SUBLANES), :] for i in range(4)])
    yield
    y = _dot3(m_ref[...], x_hi, x_lo)
    yield
    yr, yi = y[:a], y[a:]
    tc, ts = tc_ref[j, :, bl:bl + 1], ts_ref[j, :, bl:bl + 1]
    wr = yr * tc + yi * ts
    wi = yi * tc - yr * ts
    parts = (wr[:, :LANES], wr[:, LANES:], wi[:, :LANES], wi[:, LANES:])
    for kb in range(a // SUBLANES):
        for i in range(4):
            y_scr[i, kb, pl.ds(row0, SUBLANES), :] = parts[i][kb * SUBLANES:(kb + 1) * SUBLANES]
    yield


def _dft_stage2_stages(y_scr, kb, cs_ref, bf_ref, o_ref, j, kl):
    r_hi, r_lo = _complex_rows([y_scr[i, kb, pl.ds(kl, DFT_B, stride=SUBLANES), :] for i in range(4)])
    yield
    out = _dot3(cs_ref[...], r_hi, r_lo) + bf_ref[...]
    yield
    for hf in range(F_W // LANES):
        o_ref[hf, j, pl.ds(kl, DFT_B, stride=SUBLANES), :] = out[:, hf * LANES:(hf + 1) * LANES]
    yield


DFT_STAGES = 3


def _dft_kernel(x_ref, m_ref, tc_ref, ts_ref, cs_ref, bf_ref, o_ref, y_scr, *, layer):
    bf_ref, = _layer_rows(layer, bf_ref)
    step = pl.program_id(0)
    bps = x_ref.shape[1]
    n1 = B_BLOCKS // bps

    @pl.when(step < n1)
    def _():
        gens = [_dft_stage1_stages(
            x_ref, m_ref, tc_ref, ts_ref, y_scr, j, bl,
            pl.multiple_of(((step * bps + j) * SUBLANES + bl) * SUBLANES, SUBLANES))
            for j in range(bps) for bl in range(SUBLANES)]
        _run_skewed(gens, DFT_STAGES, 1)

    @pl.when(step >= n1)
    def _():
        gens = [_dft_stage2_stages(y_scr, (step - n1) * bps + j, cs_ref, bf_ref, o_ref, j, kl)
                for j in range(bps) for kl in range(SUBLANES)]
        _run_skewed(gens, DFT_STAGES, 1)


def _outffn_kernel(x_ref, oa_ref, of_ref, sg_ref, mod_ref, w_out_ref, ln1g_ref, ln1b_ref,
                   w1_ref, w2_ref, ln2g_ref, ln2b_ref, o_ref, *, n_sub, layer):
    ln1g_ref, ln1b_ref, ln2g_ref, ln2b_ref = _layer_rows(layer, ln1g_ref, ln1b_ref, ln2g_ref, ln2b_ref)
    n_kb, k2_rows = of_ref.shape[1], of_ref.shape[2]
    halves = []
    for hf in range(F_W // LANES):
        pieces = [of_ref[hf, kb, k2 * SUBLANES:(k2 + 1) * SUBLANES, :]
                  for k2 in range(k2_rows // SUBLANES) for kb in range(n_kb)]
        halves.append(jnp.concatenate(pieces, axis=0).astype(BF16))
    o_cat = jnp.concatenate([oa_ref[...]] + halves + [sg_ref[...]], axis=1)
    tm = x_ref.shape[0]
    sub = tm // n_sub
    mod_row = _mod_rows(mod_ref[...], 0)
    gens = [_out_ffn_stages(x_ref[r * sub:(r + 1) * sub, :], o_cat[r * sub:(r + 1) * sub], mod_row,
                            w_out_ref, ln1g_ref[...], ln1b_ref[...], w1_ref, w2_ref,
                            ln2g_ref[...], ln2b_ref[...]) for r in range(n_sub)]
    for r, res in enumerate(_run_skewed(gens, OUT_FFN_STAGES, 1)):
        o_ref[r * sub:(r + 1) * sub, :] = res


def _ctx_kernel(xc_ref, mod0_ref, mod1_ref, w_in0_ref, w_in1_ref, avg_ref,
                lng_ref, lnb_ref, wscat_ref, bs_ref, sink_ref, cs_ref, bf_ref,
                w_out_ref, ln1g_ref, ln1b_ref, w1_ref, w2_ref, ln2g_ref, ln2b_ref,
                kc0_ref, vc0_ref, kc1_ref, vc1_ref):
    lng_ref, lnb_ref, bf_ref, ln1g_ref, ln1b_ref, ln2g_ref, ln2b_ref = _layer_rows(
        0, lng_ref, lnb_ref, bf_ref, ln1g_ref, ln1b_ref, ln2g_ref, ln2b_ref)
    xc = xc_ref[...]
    n_ctx = xc.shape[0]
    m0 = _mod_rows(mod0_ref[...], 1)
    h = (_ln(xc) * (1.0 + m0[1]) + m0[0]).astype(BF16)
    q4 = _stack_groups(_project(h, w_in0_ref, 0, COL_K).astype(BF16))
    k_parts = _split_keys(_project(h, w_in0_ref, COL_K, COL_V).astype(BF16))
    v_parts = _split_values(_project(h, w_in0_ref, COL_V, COL_F).astype(BF16))
    kc0_ref[0], kc0_ref[1] = k_parts
    vc0_ref[0], vc0_ref[1] = v_parts
    kk = jnp.concatenate(k_parts, axis=0)
    vv = jnp.concatenate(v_parts, axis=0)
    s = lax.dot_general(q4, kk, (((1,), (1,)), ((), ())), preferred_element_type=F32)
    e, ts = _pair_softmax(s, _sink_cols(sink_ref, n_ctx))
    o4 = _pair_normalize(_dot(e, vv), ts)
    pairs = [o4[g * n_ctx:(g + 1) * n_ctx] for g in range(ATTN_W // LANES)]
    f = _project(h, w_in0_ref, COL_F, COL_U)
    y_hi, y_lo = _split(jnp.concatenate([f[:, :F_W], f[:, F_W:]], axis=0))
    o_f = _dot3(cs_ref[...], y_hi, y_lo) + bf_ref[...]
    u = _project(h, w_in0_ref, COL_U, COL_G)
    g = _project(h, w_in0_ref, COL_G, D_PROJ)
    sg = _spatial_gate(u, g, avg_ref, lng_ref, lnb_ref, wscat_ref, bs_ref)
    o_cat = jnp.concatenate([p.astype(BF16) for p in pairs] + [o_f.astype(BF16), sg.astype(BF16)],
                            axis=1)
    xc1 = _out_ffn(xc, o_cat, m0, w_out_ref, ln1g_ref[...], ln1b_ref[...], w1_ref, w2_ref,
                   ln2g_ref[...], ln2b_ref[...])
    m1 = _mod_rows(mod1_ref[...], 1)
    h1 = (_ln(xc1) * (1.0 + m1[1]) + m1[0]).astype(BF16)
    kc1_ref[0], kc1_ref[1] = _split_keys(_project(h1, w_in1_ref, COL_K, COL_V).astype(BF16))
    vc1_ref[0], vc1_ref[1] = _split_values(_project(h1, w_in1_ref, COL_V, COL_F).astype(BF16))


def _hi_lo(a):
    a32 = jnp.asarray(np.asarray(a, np.float64), F32)
    hi = a32.astype(BF16)
    return jnp.concatenate([hi, (a32 - hi.astype(F32)).astype(BF16)], axis=0)


def _dft_cos_sin(n):
    idx = np.arange(n, dtype=np.int64)
    ang = 2.0 * np.pi * ((idx[:, None] * idx[None, :]) % n) / n
    return np.cos(ang), np.sin(ang)


def _channel_dft_tables():
    c, s = _dft_cos_sin(F_DIM)
    bd = np.zeros((F_W, 2 * F_W))
    for g in range(F_GROUPS):
        sl = slice(g * F_DIM, (g + 1) * F_DIM)
        bd[sl, sl] = c / math.sqrt(F_DIM)
        bd[sl, F_W + g * F_DIM:F_W + (g + 1) * F_DIM] = -s / math.sqrt(F_DIM)
    return jnp.asarray(bd, F32)


def _group_avg_table():
    a = np.zeros((C_W, C_W))
    for h in range(C_HEADS):
        a[h * C_DIM:(h + 1) * C_DIM, h * C_DIM:(h + 1) * C_DIM] = 1.0 / C_DIM
    return jnp.asarray(np.concatenate([a, a], axis=0), BF16)


def _rope_tables(n):
    freqs = jnp.asarray(ROPE_BASE, F32) ** (-jnp.arange(0, AXIS_DIM, 2, dtype=F32) / AXIS_DIM)
    reps = LANES // HEAD_DIM

    def tables(pos, row_axis):
        ang = pos[:, None] * freqs
        c, s = jnp.cos(ang), jnp.sin(ang)
        z = jnp.zeros_like(s)
        pad = [z, z]
        pick = (lambda t: t + pad) if row_axis else (lambda t: pad + t)
        return jnp.stack([jnp.tile(jnp.concatenate(pick(t), axis=1), (1, reps))
                          for t in ([c, c], [-s, z], [z, s])])

    rt = tables(jnp.arange(n // GRID_W, dtype=F32), True)
    ct = tables(jnp.arange(GRID_W, dtype=F32), False)
    return rt, ct


def _stage1_tables(a):
    n = a * DFT_B
    c, s = _dft_cos_sin(a)
    m = _hi_lo(np.block([[c, s], [-s, c]]) / math.sqrt(a))
    k1 = np.arange(a, dtype=np.int64)[None, :, None]
    b = (np.arange(B_BLOCKS, dtype=np.int64)[:, None, None] * SUBLANES
         + np.arange(SUBLANES, dtype=np.int64)[None, None, :])
    ang = 2.0 * np.pi * ((k1 * b) % n) / n
    return m, jnp.asarray(np.cos(ang), F32), jnp.asarray(np.sin(ang), F32)


def _stage2_tables():
    c, s = _dft_cos_sin(DFT_B)
    return _hi_lo(np.concatenate([c, s], axis=1) / math.sqrt(DFT_B))


def _ctx_dft_tables(n_ctx):
    c, s = _dft_cos_sin(n_ctx)
    return _hi_lo(np.concatenate([c, s], axis=1) / math.sqrt(n_ctx))


def _const_spec(shape):
    nd = len(shape)
    return pl.BlockSpec(shape, lambda *_: (0,) * nd, pipeline_mode=pl.Buffered(1))


def _layer_spec(shape, layer):
    if len(shape) == 2:
        return _const_spec(shape)
    nd = len(shape) - 1
    return pl.BlockSpec((None,) + tuple(shape[1:]), lambda *_: (layer,) + (0,) * nd,
                        pipeline_mode=pl.Buffered(1))


def _params(*sem):
    return pltpu.CompilerParams(dimension_semantics=sem, vmem_limit_bytes=VMEM_LIMIT)


def _prep(cvt, w_ada, b_ada, w_in, bdc, wf_bd):
    depth = w_in.shape[0]
    tn = 6 * D_MODEL // PREP_STEPS
    per_layer = lambda shape: pl.BlockSpec((None,) + tuple(shape[1:]), lambda l, j: (l, 0, 0))
    return pl.pallas_call(
        _prep_kernel,
        grid=(depth, PREP_STEPS),
        in_specs=[pl.BlockSpec((D_MODEL, SUBLANES), lambda l, j: (0, 0)),
                  pl.BlockSpec((None, D_MODEL, tn), lambda l, j: (l, 0, j)),
                  pl.BlockSpec((depth, tn), lambda l, j: (0, j)),
                  per_layer(w_in.shape), pl.BlockSpec(bdc.shape, lambda l, j: (0, 0)),
                  per_layer(wf_bd.shape)],
        out_specs=[pl.BlockSpec((None, SUBLANES, tn), lambda l, j: (l, 0, j)),
                   per_layer((depth, D_MODEL, D_PROJ))],
        out_shape=[jax.ShapeDtypeStruct((depth, SUBLANES, 6 * D_MODEL), F32),
                   jax.ShapeDtypeStruct((depth, D_MODEL, D_PROJ), BF16)],
        compiler_params=_params("arbitrary", "arbitrary"),
        name="prep",
    )(cvt, w_ada, b_ada, w_in, bdc, wf_bd)


def _cast_plumbing(cast, steps):
    if cast is None:
        return [], [], [], []
    *ws, layer = cast
    rows = [w.shape[1] // steps for w in ws]
    return (ws,
            [pl.BlockSpec((None, r, w.shape[2]), lambda i: (layer, i, 0)) for w, r in zip(ws, rows)],
            [pl.BlockSpec((r, w.shape[2]), lambda i: (i, 0)) for w, r in zip(ws, rows)],
            [jax.ShapeDtypeStruct(w.shape[1:], BF16) for w in ws])


def _inproj(x, mods, w_in, layer, rope, shared, sgu, tm, n_sub, cast=None):
    n = x.shape[0]
    a = n // DFT_B
    steps = n // tm
    row = lambda w: pl.BlockSpec((tm, w), lambda i: (i, 0))
    consts = list(rope) + list(shared)
    fx_rows = tm // DFT_B * SUBLANES
    c_in, c_in_specs, c_out_specs, c_out_shapes = _cast_plumbing(cast, steps)
    return pl.pallas_call(
        functools.partial(_inproj_kernel, n_sub=n_sub, layer=layer),
        grid=(steps,),
        in_specs=[row(D_MODEL), _layer_spec(mods.shape, layer), _layer_spec(w_in.shape, layer)]
        + [_const_spec(c.shape) for c in consts] + [_layer_spec(c.shape, layer) for c in sgu]
        + c_in_specs,
        out_specs=[row(ATTN_W), row(KV_W), row(KV_W),
                   pl.BlockSpec((4, B_BLOCKS, fx_rows, LANES), lambda i: (0, 0, i, 0)), row(C_W)]
        + c_out_specs,
        out_shape=[jax.ShapeDtypeStruct((n, ATTN_W), BF16), jax.ShapeDtypeStruct((n, KV_W), BF16),
                   jax.ShapeDtypeStruct((n, KV_W), BF16),
                   jax.ShapeDtypeStruct((4, B_BLOCKS, a * SUBLANES, LANES), F32),
                   jax.ShapeDtypeStruct((n, C_W), BF16)]
        + c_out_shapes,
        compiler_params=_params("arbitrary"),
        name="inproj",
    )(x, mods, w_in, *consts, *sgu, *c_in)


def _attention(q, kd, vd, kc, vc, sink_tab, layer, tq, cast=None):
    n = q.shape[0]
    c_in, c_in_specs, c_out_specs, c_out_shapes = _cast_plumbing(cast, n // tq)
    return pl.pallas_call(
        functools.partial(_attn_kernel, n_tokens=n),
        grid=(n // tq,),
        in_specs=[pl.BlockSpec((tq, ATTN_W), lambda i: (i, 0)), _const_spec(kd.shape),
                  _const_spec(vd.shape), _const_spec(kc.shape), _const_spec(vc.shape),
                  _layer_spec(sink_tab.shape, layer)] + c_in_specs,
        out_specs=[pl.BlockSpec((tq, ATTN_W), lambda i: (i, 0))] + c_out_specs,
        out_shape=[jax.ShapeDtypeStruct((n, ATTN_W), BF16)] + c_out_shapes,
        compiler_params=_params("arbitrary"),
        name="attention",
    )(q, kd, vd, kc, vc, sink_tab, *c_in)


def _fourier(fx, tabs1, tabs2, bf_row, layer):
    a = fx.shape[2] // SUBLANES
    kb = a // SUBLANES
    m, tc, ts = tabs1
    bps = math.gcd(DFT_BLOCKS_PER_STEP, kb)
    n1 = B_BLOCKS // bps
    in_blk = lambda s: jnp.minimum(s, n1 - 1)
    out_blk = lambda s: jnp.maximum(s - n1, 0)
    tw = pl.BlockSpec((bps, a, SUBLANES), lambda s: (in_blk(s), 0, 0))
    return pl.pallas_call(
        functools.partial(_dft_kernel, layer=layer),
        grid=(n1 + kb // bps,),
        in_specs=[pl.BlockSpec((4, bps, a * SUBLANES, LANES), lambda s: (0, in_blk(s), 0, 0)),
                  _const_spec(m.shape), tw, tw, _const_spec(tabs2.shape),
                  _layer_spec(bf_row.shape, layer)],
        out_specs=pl.BlockSpec((F_W // LANES, bps, DFT_B * SUBLANES, LANES),
                               lambda s: (0, out_blk(s), 0, 0)),
        out_shape=jax.ShapeDtypeStruct((F_W // LANES, kb, DFT_B * SUBLANES, LANES), F32),
        scratch_shapes=[pltpu.VMEM((4, kb, DFT_B * SUBLANES, LANES), F32)],
        compiler_params=_params("arbitrary"),
        name="dft",
    )(fx, m, tc, ts, tabs2, bf_row)


def _outffn(x, oa, of, sg, mods, weights, layer, tm, n_sub):
    n = x.shape[0]
    a = n // DFT_B
    row = lambda w: pl.BlockSpec((tm, w), lambda i: (i, 0))
    of_spec = pl.BlockSpec((F_W // LANES, a // SUBLANES, tm // a * SUBLANES, LANES),
                           lambda i: (0, 0, i, 0))
    consts = [mods] + list(weights)
    return pl.pallas_call(
        functools.partial(_outffn_kernel, n_sub=n_sub, layer=layer),
        grid=(n // tm,),
        in_specs=[row(D_MODEL), row(ATTN_W), of_spec, row(C_W)]
        + [_layer_spec(c.shape, layer) for c in consts],
        out_specs=row(D_MODEL),
        out_shape=jax.ShapeDtypeStruct((n, D_MODEL), F32),
        compiler_params=_params("arbitrary"),
        name="outffn",
    )(x, oa, of, sg, *consts)


def _context(xc, mods, w_in, shared, sgu, sink_tab, ctx_tabs, bf_row, weights):
    n_ctx = xc.shape[0]
    args = ([(xc, None), (mods, 0), (mods, 1), (w_in, 0), (w_in, 1)] + [(t, None) for t in shared]
            + [(t, 0) for t in sgu] + [(sink_tab, 0), (ctx_tabs, None)]
            + [(bf_row, 0)] + [(t, 0) for t in weights])
    outs = [jax.ShapeDtypeStruct((KV_HEADS, n_ctx, w), BF16) for w in (LANES, 2 * LANES)] * 2
    return pl.pallas_call(
        _ctx_kernel,
        grid=(1,),
        in_specs=[_const_spec(t.shape) if l is None else _layer_spec(t.shape, l) for t, l in args],
        out_specs=[pl.BlockSpec(o.shape, lambda i: (0, 0, 0)) for o in outs],
        out_shape=outs,
        compiler_params=_params("arbitrary"),
        name="context",
    )(*[t for t, _ in args])


def _block_diag_wf(w_f):
    bd = jnp.zeros((w_f.shape[0], F_W, F_W), F32)
    for g in range(F_GROUPS):
        bd = bd.at[:, g * F_DIM:(g + 1) * F_DIM, g * F_DIM:(g + 1) * F_DIM].set(w_f[:, g])
    return bd


def _forward(x, c, ctx, c_ctx, w_ada, b_ada, w_in, w_out, attn_sink, w_fourier, b_fourier,
             sgu_ln_g, sgu_ln_b, w_spatial, b_spatial, ln1_g, ln1_b, w_ffn_in, w_ffn_out,
             ln2_g, ln2_b, tm_in=2048, n_sub_in=8, tm_in_cast=1024, n_sub_in_cast=4, tq=2048,
             tm_out=1024, n_sub=4):
    n = x.shape[1]
    n_ctx = ctx.shape[1]
    depth = w_in.shape[0]
    assert x.shape[0] == 1 and n % max(tm_in, tq, tm_out) == 0 and n >= 3 * BLOCK
    assert (n // DFT_B) % SUBLANES == 0 and tm_out % (n // DFT_B) == 0 and n_ctx % CHUNK == 0
    xs = x[0]
    cvt = jnp.concatenate([c[0][:, None], c_ctx[:, None],
                           jnp.zeros((D_MODEL, SUBLANES - N_COND), F32)], axis=1)
    mods, w_in_b = _prep(cvt, w_ada, b_ada, w_in, _channel_dft_tables(), _block_diag_wf(w_fourier))
    w_out_b = jnp.concatenate([w_out[:, h * HEAD_DIM:(h + 1) * HEAD_DIM, :] for h in HEAD_ORDER]
                              + [w_out[:, ATTN_W:, :]], axis=1).astype(BF16)
    shared = (_group_avg_table(),)
    rope = _rope_tables(n)
    tabs1 = _stage1_tables(n // DFT_B)
    tabs2 = _stage2_tables()
    ctx_tabs = _ctx_dft_tables(n_ctx)
    sgu = (sgu_ln_g.reshape(depth, C_W), sgu_ln_b.reshape(depth, C_W),
           jnp.concatenate([w_spatial[:, h] for h in range(C_HEADS)], axis=2).astype(BF16),
           jnp.repeat(jnp.swapaxes(b_spatial, 1, 2), C_DIM, axis=2))
    sink_tab = jnp.broadcast_to((attn_sink * math.log2(math.e))[:, :, None], (depth, ATTN_HEADS, LANES))
    bf_row = b_fourier.reshape(depth, F_W)

    ffn_b = None
    for l in range(depth):
        if l == 0:
            q, kd, vd, fx, sg, *ffn_b = _inproj(xs, mods, w_in_b, l, rope, shared, sgu, tm_in_cast,
                                                n_sub_in_cast, cast=(w_ffn_in, w_ffn_out, l))
        else:
            q, kd, vd, fx, sg = _inproj(xs, mods, w_in_b, l, rope, shared, sgu, tm_in, n_sub_in)
        weights = (w_out_b, ln1_g, ln1_b, *ffn_b, ln2_g, ln2_b)
        if l == 0:
            ctx_kv = _context(ctx[0], mods, w_in_b, shared, sgu, sink_tab, ctx_tabs, bf_row, weights)
        nxt = (w_ffn_in, w_ffn_out, l + 1) if l + 1 < depth else None
        oa, *ffn_b = _attention(q, kd, vd, ctx_kv[2 * l], ctx_kv[2 * l + 1], sink_tab, l, tq, cast=nxt)
        of = _fourier(fx, tabs1, tabs2, bf_row, l)
        xs = _outffn(xs, oa, of, sg, mods, weights, l, tm_out, n_sub)
    return xs[None]


def kernel(x, c, ctx, c_ctx, w_ada, b_ada, w_in, w_out, attn_sink, w_fourier, b_fourier, sgu_ln_g,
           sgu_ln_b, w_spatial, b_spatial, ln1_g, ln1_b, w_ffn_in, w_ffn_out, ln2_g, ln2_b):
    return _forward(x, c, ctx, c_ctx, w_ada, b_ada, w_in, w_out, attn_sink, w_fourier, b_fourier,
                    sgu_ln_g, sgu_ln_b, w_spatial, b_spatial, ln1_g, ln1_b, w_ffn_in, w_ffn_out,
                    ln2_g, ln2_b)
```

```python
import functools
import math

import numpy as np
import jax
import jax.numpy as jnp
from jax import lax
from jax.experimental import pallas as pl
from jax.experimental.pallas import tpu as pltpu

F32 = jnp.float32
BF16 = jnp.bfloat16

D_MODEL = 1024
DEPTH = 2
GRID_W = 64
HEAD_DIM = 64
ATTN_HEADS = 8
KV_HEADS = 2
WINDOW = 128
BLOCK = 128
ROPE_BASE = 10000.0
AXIS_DIM = HEAD_DIM // 2
F_GROUPS = 4
F_DIM = 64
C_HEADS = 4
C_DIM = 64
CHUNK = 128
D_FF = 2816
ATTN_W = ATTN_HEADS * HEAD_DIM
KV_W = KV_HEADS * HEAD_DIM
F_W = F_GROUPS * F_DIM
C_W = C_HEADS * C_DIM
COL_K = ATTN_W
COL_V = COL_K + KV_W
COL_F = COL_V + KV_W
COL_U = COL_F + 2 * F_W
COL_G = COL_U + C_W
D_PROJ = COL_G + C_W
N_COND = 2
GQA_GROUP = ATTN_HEADS // KV_HEADS
HEAD_ORDER = tuple(kv * GQA_GROUP + g for g in range(GQA_GROUP) for kv in range(KV_HEADS))
Q_SCALE = HEAD_DIM ** -0.5 * math.log2(math.e)
ALPHA = (2 * DEPTH) ** 0.25
LN_EPS = 1e-6
NEG_BIG = -1e30

LANES = 128
SUBLANES = 8
MXU_TILE = 256
FFN_CHUNKS = (0, 4 * MXU_TILE, 8 * MXU_TILE, D_FF)
PREP_STEPS = 6
DFT_B = 128
B_BLOCKS = DFT_B // SUBLANES
DFT_BLOCKS_PER_STEP = 2
VMEM_LIMIT = 56 * 1024 * 1024


def _dot(a, b):
    return jnp.dot(a, b, preferred_element_type=F32)


def _split(a):
    hi = a.astype(BF16)
    lo = (a - hi.astype(F32)).astype(BF16)
    return hi, lo


def _dot3(a_cat, b_hi, b_lo):
    rows = a_cat.shape[0] // 2
    both = _dot(a_cat, b_hi)
    return both[:rows] + both[rows:] + _dot(a_cat[:rows], b_lo)


def _ln(x):
    mu = jnp.mean(x, axis=-1, keepdims=True)
    xc = x - mu
    var = jnp.mean(xc * xc, axis=-1, keepdims=True)
    return xc * lax.rsqrt(var + LN_EPS)


def _gelu(x):
    return 0.5 * x * (1.0 + jnp.tanh(math.sqrt(2.0 / math.pi) * (x + 0.044715 * (x * x * x))))


def _silu(x):
    return x / (1.0 + jnp.exp(-x))


def _mod_rows(mod, row):
    return [mod[row:row + 1, i * D_MODEL:(i + 1) * D_MODEL] for i in range(6)]


def _rope(t, tabs):
    cos, sin_a, sin_b = tabs
    return (t * cos + pltpu.roll(t, LANES - AXIS_DIM // 2, 1) * sin_a
            + pltpu.roll(t, AXIS_DIM // 2, 1) * sin_b)


def _rope_tile(rt_ref, ct_ref, first_grid_row, rows):
    tabs = []
    for t in range(3):
        groups = [rt_ref[t, pl.ds(first_grid_row + r, 1), :] + ct_ref[t]
                  for r in range(rows // GRID_W)]
        tabs.append(jnp.concatenate(groups, axis=0) if len(groups) > 1 else groups[0])
    return tabs


def _project(h, w_in_ref, lo, hi):
    return _dot(h, w_in_ref[:, lo:hi])


def _spatial_gate_stages(u, g, avg_ref, lng_ref, lnb_ref, wscat_ref, bs_ref):
    rows = u.shape[0]
    ug = _gelu(u)
    vg = _gelu(g)
    avg = avg_ref[...]
    mu = _dot(jnp.concatenate(_split(vg), axis=1), avg)
    yield None
    vc = vg - mu
    var = _dot(jnp.concatenate(_split(vc * vc), axis=1), avg)
    yield None
    vn = vc * lax.rsqrt(var + LN_EPS) * lng_ref[...] + lnb_ref[...]
    lane = lax.broadcasted_iota(jnp.int32, (1, C_W), 1)
    vnb = vn.astype(BF16)
    zero = jnp.zeros_like(vnb)
    outs = []
    for c in range(rows // CHUNK):
        blk = vnb[c * CHUNK:(c + 1) * CHUNK]
        rhs = jnp.concatenate(
            [jnp.where((lane >= h * C_DIM) & (lane < (h + 1) * C_DIM), blk, zero[:CHUNK])
             for h in range(C_HEADS)], axis=0)
        vs = _dot(wscat_ref[...], rhs) + bs_ref[...]
        outs.append(ug[c * CHUNK:(c + 1) * CHUNK] * vs)
    yield jnp.concatenate(outs, axis=0) if len(outs) > 1 else outs[0]


def _spatial_gate(*args):
    return list(_spatial_gate_stages(*args))[-1]


def _low_lanes():
    return lax.broadcasted_iota(jnp.int32, (1, LANES), 1) < HEAD_DIM


def _split_keys(k):
    low = _low_lanes()
    z = jnp.zeros_like(k)
    return jnp.where(low, k, z), jnp.where(low, z, k)


def _split_values(v):
    top, bot = _split_keys(v)
    low_ones = jnp.where(_low_lanes(), 1.0, 0.0)
    return (jnp.concatenate([top, jnp.broadcast_to(low_ones, v.shape).astype(BF16)], axis=1),
            jnp.concatenate([bot, jnp.broadcast_to(1.0 - low_ones, v.shape).astype(BF16)], axis=1))


def _stack_groups(q):
    return jnp.concatenate([q[:, g * LANES:(g + 1) * LANES] for g in range(ATTN_W // LANES)], axis=0)


def _sink_cols(sink_ref, rows):
    return [jnp.concatenate([jnp.broadcast_to(sink_ref[4 * kh + g:4 * kh + g + 1, 0:1], (rows, 1))
                             for g in range(4)], axis=0) for kh in range(KV_HEADS)]


def _pair_softmax(s, sinks):
    half = s.shape[1] // 2
    es, ts = [], []
    for kh in range(KV_HEADS):
        sh = s[:, kh * half:(kh + 1) * half]
        m = jnp.max(sh, axis=1, keepdims=True)
        es.append(jnp.exp2(sh - m).astype(BF16))
        ts.append(jnp.exp2(sinks[kh] - m))
    return jnp.concatenate(es, axis=1), ts


def _pair_normalize(o, ts):
    return o[:, :LANES] / (o[:, LANES:] + jnp.where(_low_lanes(), ts[0], ts[1]))


def _out_ffn_stages(x, o_cat, mod_row, w_out_ref, ln1g, ln1b, w1_ref, w2_ref, ln2g, ln2b):
    g_m, sh_f, sc_f, g_f = mod_row[2], mod_row[3], mod_row[4], mod_row[5]
    y = _dot(o_cat, w_out_ref[...])
    yield None
    x1 = _ln(ALPHA * x + g_m * y) * ln1g + ln1b
    h2 = (_ln(x1) * (1.0 + sc_f) + sh_f).astype(BF16)
    yield None
    y2 = None
    for lo, hi in zip(FFN_CHUNKS[:-1], FFN_CHUNKS[1:]):
        a = _dot(h2, w1_ref[:, lo:hi])
        b = _dot(h2, w1_ref[:, D_FF + lo:D_FF + hi])
        t = (_silu(a) * b).astype(BF16)
        part = _dot(t, w2_ref[lo:hi, :])
        y2 = part if y2 is None else y2 + part
        yield None
    yield _ln(ALPHA * x1 + g_f * y2) * ln2g + ln2b


OUT_FFN_STAGES = len(FFN_CHUNKS) + 2


def _run_skewed(gens, n_stages, skew):
    results = [None] * len(gens)
    for t in range(n_stages + skew * (len(gens) - 1)):
        for r, gen in enumerate(gens):
            if 0 <= t - r * skew < n_stages:
                results[r] = next(gen)
    return results


def _out_ffn(*args):
    return _run_skewed([_out_ffn_stages(*args)], OUT_FFN_STAGES, 0)[0]


def _layer_rows(layer, *refs):
    return [r.at[pl.ds(layer, 1)] for r in refs]


def _ada_kernel(cvt_ref, w_ref, b_ref, o_ref):
    s = _silu(cvt_ref[...])
    w = w_ref[...]
    bias = b_ref[pl.ds(pl.program_id(0), 1), :]
    o_ref[...] = jnp.zeros_like(o_ref)
    for r in range(N_COND):
        o_ref[r:r + 1, :] = jnp.sum(w * s[:, r:r + 1], axis=0, keepdims=True) + bias


def _wprep_kernel(w_in_ref, bdc_ref, wf_ref, o_ref):
    hp = functools.partial(jnp.dot, precision=lax.Precision.HIGHEST, preferred_element_type=F32)
    wf = wf_ref[...]
    fold = jnp.concatenate([hp(bdc_ref[:, :F_W], wf), hp(bdc_ref[:, F_W:], wf)], axis=1)
    for j, head in enumerate(HEAD_ORDER):
        o_ref[:, j * HEAD_DIM:(j + 1) * HEAD_DIM] = (
            w_in_ref[:, head * HEAD_DIM:(head + 1) * HEAD_DIM] * Q_SCALE).astype(BF16)
    o_ref[:, COL_K:COL_F] = w_in_ref[:, COL_K:COL_F].astype(BF16)
    o_ref[:, COL_F:COL_U] = hp(w_in_ref[:, COL_F:COL_F + F_W], fold).astype(BF16)
    o_ref[:, COL_U:] = w_in_ref[:, COL_F + F_W:].astype(BF16)


def _prep_kernel(cvt_ref, w_ada_ref, b_ada_ref, w_in_ref, bdc_ref, wf_ref, mods_ref, wcat_ref):
    _ada_kernel(cvt_ref, w_ada_ref, b_ada_ref, mods_ref)

    @pl.when(pl.program_id(1) == 0)
    def _():
        _wprep_kernel(w_in_ref, bdc_ref, wf_ref, wcat_ref)


def _cast_stages(pairs, n_stages):
    for s in range(n_stages):
        for src, dst in pairs:
            step = -(-src.shape[0] // (n_stages * 2 * SUBLANES)) * 2 * SUBLANES
            lo, hi = min(s * step, src.shape[0]), min((s + 1) * step, src.shape[0])
            if hi > lo:
                dst[lo:hi, :] = src[lo:hi, :].astype(BF16)
        yield


def _inproj_stages(x, sh, sc, w_ref, tabs, avg_ref, lng_ref, lnb_ref, wscat_ref, bs_ref,
                   q_ref, kd_ref, vd_ref, fx_ref, sg_ref, row0):
    rows = x.shape[0]
    rs = slice(row0, row0 + rows)
    h = (_ln(x) * (1.0 + sc) + sh).astype(BF16)
    yield
    for p in range(ATTN_W // MXU_TILE):
        qq = _project(h, w_ref, p * MXU_TILE, (p + 1) * MXU_TILE)
        for j in range(MXU_TILE // LANES):
            c0 = p * MXU_TILE + j * LANES
            q_ref[rs, c0:c0 + LANES] = _rope(qq[:, j * LANES:(j + 1) * LANES], tabs).astype(BF16)
    kv = _project(h, w_ref, COL_K, COL_F)
    kd_ref[rs, :] = _rope(kv[:, :KV_W], tabs).astype(BF16)
    vd_ref[rs, :] = kv[:, KV_W:].astype(BF16)
    yield
    f = _project(h, w_ref, COL_F, COL_U)
    for al in range(rows // DFT_B):
        for bb in range(B_BLOCKS):
            r0 = al * DFT_B + bb * SUBLANES
            a0 = (row0 // DFT_B + al) * SUBLANES
            for p in range(4):
                fx_ref[p, bb, a0:a0 + SUBLANES, :] = f[r0:r0 + SUBLANES, p * LANES:(p + 1) * LANES]
    u = _project(h, w_ref, COL_U, COL_G)
    g = _project(h, w_ref, COL_G, D_PROJ)
    gate = _spatial_gate_stages(u, g, avg_ref, lng_ref, lnb_ref, wscat_ref, bs_ref)
    next(gate)
    yield
    next(gate)
    yield
    sg_ref[rs, :] = next(gate).astype(BF16)
    yield


INPROJ_STAGES = 5


def _with_cast_refs(refs, n_out):
    if len(refs) == n_out:
        return refs, []
    return refs[2:2 + n_out], [(refs[0], refs[-2]), (refs[1], refs[-1])]


def _inproj_kernel(x_ref, mod_ref, w_ref, rt_ref, ct_ref, avg_ref, lng_ref, lnb_ref, wscat_ref,
                   bs_ref, *refs, n_sub, layer):
    (q_ref, kd_ref, vd_ref, fx_ref, sg_ref), cast = _with_cast_refs(refs, 5)
    lng_ref, lnb_ref = _layer_rows(layer, lng_ref, lnb_ref)
    tm = x_ref.shape[0]
    sub = tm // n_sub
    sh, sc = _mod_rows(mod_ref[...], 0)[:2]
    gens = []
    for r in range(n_sub):
        grid_row = pl.program_id(0) * (tm // GRID_W) + r * (sub // GRID_W)
        gens.append(_inproj_stages(
            x_ref[r * sub:(r + 1) * sub, :], sh, sc, w_ref, _rope_tile(rt_ref, ct_ref, grid_row, sub),
            avg_ref, lng_ref, lnb_ref, wscat_ref, bs_ref, q_ref, kd_ref, vd_ref, fx_ref, sg_ref,
            r * sub))
    if cast:
        gens.append(_cast_stages(cast, INPROJ_STAGES))
    _run_skewed(gens, INPROJ_STAGES, 1)


def _halo_rows(refs, row0):
    prev_ref, main_ref, next_ref = refs
    lo, hi = row0 - BLOCK, row0 + 2 * BLOCK
    parts = ([prev_ref[...]] if lo < 0 else []) + [main_ref[max(lo, 0):min(hi, main_ref.shape[0]), :]]
    parts += [next_ref[...]] if hi > main_ref.shape[0] else []
    return jnp.concatenate(parts, axis=0) if len(parts) > 1 else parts[0]


def _attn_block_stages(q_ref, k_refs, v_refs, kc_ref, vc_ref, sinks, o_ref, blk, row0, n_tokens):
    span = 3 * BLOCK
    n_ctx = kc_ref.shape[1]
    half = span + n_ctx
    start = (blk - 1) * BLOCK
    k_top, k_bot = _split_keys(_halo_rows(k_refs, row0))
    v_top, v_bot = _split_values(_halo_rows(v_refs, row0))
    kk = jnp.concatenate([k_top, kc_ref[0], k_bot, kc_ref[1]], axis=0)
    vv = jnp.concatenate([v_top, vc_ref[0], v_bot, vc_ref[1]], axis=0)
    q4 = _stack_groups(q_ref[row0:row0 + BLOCK, :])
    yield
    s = lax.dot_general(q4, kk, (((1,), (1,)), ((), ())), preferred_element_type=F32)
    yield
    qpos = blk * BLOCK + lax.broadcasted_iota(jnp.int32, (BLOCK, span), 0)
    kpos = start + lax.broadcasted_iota(jnp.int32, (BLOCK, span), 1)
    ok = (jnp.abs(kpos - qpos) <= WINDOW) & (kpos >= 0) & (kpos < n_tokens)
    ok = jnp.concatenate([ok] * 4, axis=0)
    s = jnp.concatenate([jnp.where(ok, s[:, :span], NEG_BIG), s[:, span:half],
                         jnp.where(ok, s[:, half:half + span], NEG_BIG), s[:, half + span:]], axis=1)
    e, ts = _pair_softmax(s, sinks)
    yield
    o = _dot(e, vv)
    yield
    res = _pair_normalize(o, ts).astype(BF16)
    for g in range(ATTN_W // LANES):
        o_ref[row0:row0 + BLOCK, g * LANES:(g + 1) * LANES] = res[g * BLOCK:(g + 1) * BLOCK]
    yield


ATTN_STAGES = 5


def _attn_kernel(q_ref, kp_ref, km_ref, kn_ref, vp_ref, vm_ref, vn_ref, kc_ref, vc_ref, sink_ref,
                 *refs, n_tokens):
    (o_ref,), cast = _with_cast_refs(refs, 1)
    n_blocks = q_ref.shape[0] // BLOCK
    sinks = _sink_cols(sink_ref, BLOCK)
    k_ref, v_ref = (kp_ref, km_ref, kn_ref), (vp_ref, vm_ref, vn_ref)
    gens = [_attn_block_stages(q_ref, k_ref, v_ref, kc_ref, vc_ref, sinks, o_ref,
                               pl.program_id(0) * n_blocks + r, r * BLOCK, n_tokens)
            for r in range(n_blocks)]
    if cast:
        gens.append(_cast_stages(cast, ATTN_STAGES))
    _run_skewed(gens, ATTN_STAGES, 1)


def _complex_rows(parts):
    return _split(jnp.concatenate([jnp.concatenate(parts[0:2], axis=1),
                                   jnp.concatenate(parts[2:4], axis=1)], axis=0))


def _dft_stage1_stages(x_ref, m_ref, tc_ref, ts_ref, y_scr, j, bl, row0):
    a = m_ref.shape[0] // 4
    x_hi, x_lo = _complex_rows([x_ref[i, j, pl.ds(bl, a, stride=SUBLANES), :] for i in range(4)])
    yield
    y = _dot3(m_ref[...], x_hi, x_lo)
    yield
    yr, yi = y[:a], y[a:]
    tc, ts = tc_ref[j, :, bl:bl + 1], ts_ref[j, :, bl:bl + 1]
    wr = yr * tc + yi * ts
    wi = yi * tc - yr * ts
    parts = (wr[:, :LANES], wr[:, LANES:], wi[:, :LANES], wi[:, LANES:])
    for kb in range(a // SUBLANES):
        for i in range(4):
            y_scr[i, kb, pl.ds(row0, SUBLANES), :] = parts[i][kb * SUBLANES:(kb + 1) * SUBLANES]
    yield


def _dft_stage2_stages(y_scr, kb, cs_ref, bf_ref, o_ref, j, kl):
    r_hi, r_lo = _complex_rows([y_scr[i, kb, pl.ds(kl, DFT_B, stride=SUBLANES), :] for i in range(4)])
    yield
    out = _dot3(cs_ref[...], r_hi, r_lo) + bf_ref[...]
    yield
    for hf in range(F_W // LANES):
        o_ref[hf, j, pl.ds(kl, DFT_B, stride=SUBLANES), :] = out[:, hf * LANES:(hf + 1) * LANES]
    yield


DFT_STAGES = 3


def _dft_kernel(x_ref, m_ref, tc_ref, ts_ref, cs_ref, bf_ref, o_ref, y_scr, *, layer):
    bf_ref, = _layer_rows(layer, bf_ref)
    step = pl.program_id(0)
    bps = x_ref.shape[1]
    n1 = B_BLOCKS // bps

    @pl.when(step < n1)
    def _():
        gens = [_dft_stage1_stages(
            x_ref, m_ref, tc_ref, ts_ref, y_scr, j, bl,
            pl.multiple_of(((step * bps + j) * SUBLANES + bl) * SUBLANES, SUBLANES))
            for j in range(bps) for bl in range(SUBLANES)]
        _run_skewed(gens, DFT_STAGES, 1)

    @pl.when(step >= n1)
    def _():
        gens = [_dft_stage2_stages(y_scr, (step - n1) * bps + j, cs_ref, bf_ref, o_ref, j, kl)
                for j in range(bps) for kl in range(SUBLANES)]
        _run_skewed(gens, DFT_STAGES, 1)


def _outffn_kernel(x_ref, oa_ref, of_ref, sg_ref, mod_ref, w_out_ref, ln1g_ref, ln1b_ref,
                   w1_ref, w2_ref, ln2g_ref, ln2b_ref, o_ref, *, n_sub, layer):
    ln1g_ref, ln1b_ref, ln2g_ref, ln2b_ref = _layer_rows(layer, ln1g_ref, ln1b_ref, ln2g_ref, ln2b_ref)
    n_kb, k2_rows = of_ref.shape[1], of_ref.shape[2]
    halves = []
    for hf in range(F_W // LANES):
        pieces = [of_ref[hf, kb, k2 * SUBLANES:(k2 + 1) * SUBLANES, :]
                  for k2 in range(k2_rows // SUBLANES) for kb in range(n_kb)]
        halves.append(jnp.concatenate(pieces, axis=0).astype(BF16))
    o_cat = jnp.concatenate([oa_ref[...]] + halves + [sg_ref[...]], axis=1)
    tm = x_ref.shape[0]
    sub = tm // n_sub
    mod_row = _mod_rows(mod_ref[...], 0)
    gens = [_out_ffn_stages(x_ref[r * sub:(r + 1) * sub, :], o_cat[r * sub:(r + 1) * sub], mod_row,
                            w_out_ref, ln1g_ref[...], ln1b_ref[...], w1_ref, w2_ref,
                            ln2g_ref[...], ln2b_ref[...]) for r in range(n_sub)]
    for r, res in enumerate(_run_skewed(gens, OUT_FFN_STAGES, 1)):
        o_ref[r * sub:(r + 1) * sub, :] = res


def _ctx_kernel(xc_ref, mod0_ref, mod1_ref, w_in0_ref, w_in1_ref, avg_ref,
                lng_ref, lnb_ref, wscat_ref, bs_ref, sink_ref, cs_ref, bf_ref,
                w_out_ref, ln1g_ref, ln1b_ref, w1_ref, w2_ref, ln2g_ref, ln2b_ref,
                kc0_ref, vc0_ref, kc1_ref, vc1_ref):
    lng_ref, lnb_ref, bf_ref, ln1g_ref, ln1b_ref, ln2g_ref, ln2b_ref = _layer_rows(
        0, lng_ref, lnb_ref, bf_ref, ln1g_ref, ln1b_ref, ln2g_ref, ln2b_ref)
    xc = xc_ref[...]
    n_ctx = xc.shape[0]
    m0 = _mod_rows(mod0_ref[...], 1)
    h = (_ln(xc) * (1.0 + m0[1]) + m0[0]).astype(BF16)
    q4 = _stack_groups(_project(h, w_in0_ref, 0, COL_K).astype(BF16))
    k_parts = _split_keys(_project(h, w_in0_ref, COL_K, COL_V).astype(BF16))
    v_parts = _split_values(_project(h, w_in0_ref, COL_V, COL_F).astype(BF16))
    kc0_ref[0], kc0_ref[1] = k_parts
    vc0_ref[0], vc0_ref[1] = v_parts
    kk = jnp.concatenate(k_parts, axis=0)
    vv = jnp.concatenate(v_parts, axis=0)
    s = lax.dot_general(q4, kk, (((1,), (1,)), ((), ())), preferred_element_type=F32)
    e, ts = _pair_softmax(s, _sink_cols(sink_ref, n_ctx))
    o4 = _pair_normalize(_dot(e, vv), ts)
    pairs = [o4[g * n_ctx:(g + 1) * n_ctx] for g in range(ATTN_W // LANES)]
    f = _project(h, w_in0_ref, COL_F, COL_U)
    y_hi, y_lo = _split(jnp.concatenate([f[:, :F_W], f[:, F_W:]], axis=0))
    o_f = _dot3(cs_ref[...], y_hi, y_lo) + bf_ref[...]
    u = _project(h, w_in0_ref, COL_U, COL_G)
    g = _project(h, w_in0_ref, COL_G, D_PROJ)
    sg = _spatial_gate(u, g, avg_ref, lng_ref, lnb_ref, wscat_ref, bs_ref)
    o_cat = jnp.concatenate([p.astype(BF16) for p in pairs] + [o_f.astype(BF16), sg.astype(BF16)],
                            axis=1)
    xc1 = _out_ffn(xc, o_cat, m0, w_out_ref, ln1g_ref[...], ln1b_ref[...], w1_ref, w2_ref,
                   ln2g_ref[...], ln2b_ref[...])
    m1 = _mod_rows(mod1_ref[...], 1)
    h1 = (_ln(xc1) * (1.0 + m1[1]) + m1[0]).astype(BF16)
    kc1_ref[0], kc1_ref[1] = _split_keys(_project(h1, w_in1_ref, COL_K, COL_V).astype(BF16))
    vc1_ref[0], vc1_ref[1] = _split_values(_project(h1, w_in1_ref, COL_V, COL_F).astype(BF16))


def _hi_lo(a):
    a32 = jnp.asarray(np.asarray(a, np.float64), F32)
    hi = a32.astype(BF16)
    return jnp.concatenate([hi, (a32 - hi.astype(F32)).astype(BF16)], axis=0)


def _dft_cos_sin(n):
    idx = np.arange(n, dtype=np.int64)
    ang = 2.0 * np.pi * ((idx[:, None] * idx[None, :]) % n) / n
    return np.cos(ang), np.sin(ang)


def _channel_dft_tables():
    c, s = _dft_cos_sin(F_DIM)
    bd = np.zeros((F_W, 2 * F_W))
    for g in range(F_GROUPS):
        sl = slice(g * F_DIM, (g + 1) * F_DIM)
        bd[sl, sl] = c / math.sqrt(F_DIM)
        bd[sl, F_W + g * F_DIM:F_W + (g + 1) * F_DIM] = -s / math.sqrt(F_DIM)
    return jnp.asarray(bd, F32)


def _group_avg_table():
    a = np.zeros((C_W, C_W))
    for h in range(C_HEADS):
        a[h * C_DIM:(h + 1) * C_DIM, h * C_DIM:(h + 1) * C_DIM] = 1.0 / C_DIM
    return jnp.asarray(np.concatenate([a, a], axis=0), BF16)


def _rope_tables(n):
    freqs = jnp.asarray(ROPE_BASE, F32) ** (-jnp.arange(0, AXIS_DIM, 2, dtype=F32) / AXIS_DIM)
    reps = LANES // HEAD_DIM

    def tables(pos, row_axis):
        ang = pos[:, None] * freqs
        c, s = jnp.cos(ang), jnp.sin(ang)
        z = jnp.zeros_like(s)
        pad = [z, z]
        pick = (lambda t: t + pad) if row_axis else (lambda t: pad + t)
        return jnp.stack([jnp.tile(jnp.concatenate(pick(t), axis=1), (1, reps))
                          for t in ([c, c], [-s, z], [z, s])])

    rt = tables(jnp.arange(n // GRID_W, dtype=F32), True)
    ct = tables(jnp.arange(GRID_W, dtype=F32), False)
    return rt, ct


def _stage1_tables(a):
    n = a * DFT_B
    c, s = _dft_cos_sin(a)
    m = _hi_lo(np.block([[c, s], [-s, c]]) / math.sqrt(a))
    k1 = np.arange(a, dtype=np.int64)[None, :, None]
    b = (np.arange(B_BLOCKS, dtype=np.int64)[:, None, None] * SUBLANES
         + np.arange(SUBLANES, dtype=np.int64)[None, None, :])
    ang = 2.0 * np.pi * ((k1 * b) % n) / n
    return m, jnp.asarray(np.cos(ang), F32), jnp.asarray(np.sin(ang), F32)


def _stage2_tables():
    c, s = _dft_cos_sin(DFT_B)
    return _hi_lo(np.concatenate([c, s], axis=1) / math.sqrt(DFT_B))


def _ctx_dft_tables(n_ctx):
    c, s = _dft_cos_sin(n_ctx)
    return _hi_lo(np.concatenate([c, s], axis=1) / math.sqrt(n_ctx))


def _const_spec(shape):
    nd = len(shape)
    return pl.BlockSpec(shape, lambda *_: (0,) * nd, pipeline_mode=pl.Buffered(1))


def _layer_spec(shape, layer):
    if len(shape) == 2:
        return _const_spec(shape)
    nd = len(shape) - 1
    return pl.BlockSpec((None,) + tuple(shape[1:]), lambda *_: (layer,) + (0,) * nd,
                        pipeline_mode=pl.Buffered(1))


def _params(*sem):
    return pltpu.CompilerParams(dimension_semantics=sem, vmem_limit_bytes=VMEM_LIMIT)


def _prep(cvt, w_ada, b_ada, w_in, bdc, wf_bd):
    depth = w_in.shape[0]
    tn = 6 * D_MODEL // PREP_STEPS
    per_layer = lambda shape: pl.BlockSpec((None,) + tuple(shape[1:]), lambda l, j: (l, 0, 0))
    return pl.pallas_call(
        _prep_kernel,
        grid=(depth, PREP_STEPS),
        in_specs=[pl.BlockSpec((D_MODEL, SUBLANES), lambda l, j: (0, 0)),
                  pl.BlockSpec((None, D_MODEL, tn), lambda l, j: (l, 0, j)),
                  pl.BlockSpec((depth, tn), lambda l, j: (0, j)),
                  per_layer(w_in.shape), pl.BlockSpec(bdc.shape, lambda l, j: (0, 0)),
                  per_layer(wf_bd.shape)],
        out_specs=[pl.BlockSpec((None, SUBLANES, tn), lambda l, j: (l, 0, j)),
                   per_layer((depth, D_MODEL, D_PROJ))],
        out_shape=[jax.ShapeDtypeStruct((depth, SUBLANES, 6 * D_MODEL), F32),
                   jax.ShapeDtypeStruct((depth, D_MODEL, D_PROJ), BF16)],
        compiler_params=_params("arbitrary", "arbitrary"),
        name="prep",
    )(cvt, w_ada, b_ada, w_in, bdc, wf_bd)


def _cast_plumbing(cast, steps):
    if cast is None:
        return [], [], [], []
    *ws, layer = cast
    rows = [w.shape[1] // steps for w in ws]
    assert all(w.shape[1] == r * steps and r % (2 * SUBLANES) == 0 for w, r in zip(ws, rows))
    return (ws,
            [pl.BlockSpec((None, r, w.shape[2]), lambda i: (layer, i, 0)) for w, r in zip(ws, rows)],
            [pl.BlockSpec((r, w.shape[2]), lambda i: (i, 0)) for w, r in zip(ws, rows)],
            [jax.ShapeDtypeStruct(w.shape[1:], BF16) for w in ws])


def _inproj(x, mods, w_in, layer, rope, shared, sgu, tm, n_sub, cast=None):
    n = x.shape[0]
    a = n // DFT_B
    steps = n // tm
    row = lambda w: pl.BlockSpec((tm, w), lambda i: (i, 0))
    consts = list(rope) + list(shared)
    fx_rows = tm // DFT_B * SUBLANES
    c_in, c_in_specs, c_out_specs, c_out_shapes = _cast_plumbing(cast, steps)
    return pl.pallas_call(
        functools.partial(_inproj_kernel, n_sub=n_sub, layer=layer),
        grid=(steps,),
        in_specs=[row(D_MODEL), _layer_spec(mods.shape, layer), _layer_spec(w_in.shape, layer)]
        + [_const_spec(c.shape) for c in consts] + [_layer_spec(c.shape, layer) for c in sgu]
        + c_in_specs,
        out_specs=[row(ATTN_W), row(KV_W), row(KV_W),
                   pl.BlockSpec((4, B_BLOCKS, fx_rows, LANES), lambda i: (0, 0, i, 0)), row(C_W)]
        + c_out_specs,
        out_shape=[jax.ShapeDtypeStruct((n, ATTN_W), BF16), jax.ShapeDtypeStruct((n, KV_W), BF16),
                   jax.ShapeDtypeStruct((n, KV_W), BF16),
                   jax.ShapeDtypeStruct((4, B_BLOCKS, a * SUBLANES, LANES), F32),
                   jax.ShapeDtypeStruct((n, C_W), BF16)]
        + c_out_shapes,
        compiler_params=_params("arbitrary"),
        name="inproj",
    )(x, mods, w_in, *consts, *sgu, *c_in)


def _attention(q, kd, vd, kc, vc, sink_tab, layer, tq, cast=None):
    n = q.shape[0]
    c_in, c_in_specs, c_out_specs, c_out_shapes = _cast_plumbing(cast, n // tq)
    nb, last = tq // BLOCK, n // BLOCK - 1
    halo = [pl.BlockSpec((BLOCK, KV_W), lambda i: (jnp.maximum(i * nb - 1, 0), 0)),
            pl.BlockSpec((tq, KV_W), lambda i: (i, 0)),
            pl.BlockSpec((BLOCK, KV_W), lambda i: (jnp.minimum((i + 1) * nb, last), 0))]
    return pl.pallas_call(
        functools.partial(_attn_kernel, n_tokens=n),
        grid=(n // tq,),
        in_specs=[pl.BlockSpec((tq, ATTN_W), lambda i: (i, 0))] + halo + halo
        + [_const_spec(kc.shape), _const_spec(vc.shape),
           _layer_spec(sink_tab.shape, layer)] + c_in_specs,
        out_specs=[pl.BlockSpec((tq, ATTN_W), lambda i: (i, 0))] + c_out_specs,
        out_shape=[jax.ShapeDtypeStruct((n, ATTN_W), BF16)] + c_out_shapes,
        compiler_params=_params("arbitrary"),
        name="attention",
    )(q, kd, kd, kd, vd, vd, vd, kc, vc, sink_tab, *c_in)


def _fourier(fx, tabs1, tabs2, bf_row, layer):
    a = fx.shape[2] // SUBLANES
    kb = a // SUBLANES
    m, tc, ts = tabs1
    bps = math.gcd(DFT_BLOCKS_PER_STEP, kb)
    n1 = B_BLOCKS // bps
    in_blk = lambda s: jnp.minimum(s, n1 - 1)
    out_blk = lambda s: jnp.maximum(s - n1, 0)
    tw = pl.BlockSpec((bps, a, SUBLANES), lambda s: (in_blk(s), 0, 0))
    return pl.pallas_call(
        functools.partial(_dft_kernel, layer=layer),
        grid=(n1 + kb // bps,),
        in_specs=[pl.BlockSpec((4, bps, a * SUBLANES, LANES), lambda s: (0, in_blk(s), 0, 0)),
                  _const_spec(m.shape), tw, tw, _const_spec(tabs2.shape),
                  _layer_spec(bf_row.shape, layer)],
        out_specs=pl.BlockSpec((F_W // LANES, bps, DFT_B * SUBLANES, LANES),
                               lambda s: (0, out_blk(s), 0, 0)),
        out_shape=jax.ShapeDtypeStruct((F_W // LANES, kb, DFT_B * SUBLANES, LANES), F32),
        scratch_shapes=[pltpu.VMEM((4, kb, DFT_B * SUBLANES, LANES), F32)],
        compiler_params=_params("arbitrary"),
        name="dft",
    )(fx, m, tc, ts, tabs2, bf_row)


def _outffn(x, oa, of, sg, mods, weights, layer, tm, n_sub):
    n = x.shape[0]
    a = n // DFT_B
    row = lambda w: pl.BlockSpec((tm, w), lambda i: (i, 0))
    of_spec = pl.BlockSpec((F_W // LANES, a // SUBLANES, tm // a * SUBLANES, LANES),
                           lambda i: (0, 0, i, 0))
    consts = [mods] + list(weights)
    return pl.pallas_call(
        functools.partial(_outffn_kernel, n_sub=n_sub, layer=layer),
        grid=(n // tm,),
        in_specs=[row(D_MODEL), row(ATTN_W), of_spec, row(C_W)]
        + [_layer_spec(c.shape, layer) for c in consts],
        out_specs=row(D_MODEL),
        out_shape=jax.ShapeDtypeStruct((n, D_MODEL), F32),
        compiler_params=_params("arbitrary"),
        name="outffn",
    )(x, oa, of, sg, *consts)


def _context(xc, mods, w_in, shared, sgu, sink_tab, ctx_tabs, bf_row, weights):
    n_ctx = xc.shape[0]
    args = ([(xc, None), (mods, 0), (mods, 1), (w_in, 0), (w_in, 1)] + [(t, None) for t in shared]
            + [(t, 0) for t in sgu] + [(sink_tab, 0), (ctx_tabs, None)]
            + [(bf_row, 0)] + [(t, 0) for t in weights])
    outs = [jax.ShapeDtypeStruct((KV_HEADS, n_ctx, w), BF16) for w in (LANES, 2 * LANES)] * 2
    return pl.pallas_call(
        _ctx_kernel,
        grid=(1,),
        in_specs=[_const_spec(t.shape) if l is None else _layer_spec(t.shape, l) for t, l in args],
        out_specs=[pl.BlockSpec(o.shape, lambda i: (0, 0, 0)) for o in outs],
        out_shape=outs,
        compiler_params=_params("arbitrary"),
        name="context",
    )(*[t for t, _ in args])


def _block_diag_wf(w_f):
    same = jnp.eye(F_GROUPS, dtype=bool)[None, :, None, :, None]
    return jnp.where(same, w_f[:, :, :, None, :], 0.0).reshape(w_f.shape[0], F_W, F_W)


def _forward(x, c, ctx, c_ctx, w_ada, b_ada, w_in, w_out, attn_sink, w_fourier, b_fourier,
             sgu_ln_g, sgu_ln_b, w_spatial, b_spatial, ln1_g, ln1_b, w_ffn_in, w_ffn_out,
             ln2_g, ln2_b, tm_in=2048, n_sub_in=8, tm_in_cast=1024, n_sub_in_cast=4, tq=2048,
             tm_out=1024, n_sub=4):
    n = x.shape[1]
    n_ctx = ctx.shape[1]
    depth = w_in.shape[0]
    assert x.shape[0] == 1 and n % max(tm_in, tq, tm_out) == 0 and n >= 3 * BLOCK
    assert (n // DFT_B) % SUBLANES == 0 and tm_out % (n // DFT_B) == 0 and n_ctx % CHUNK == 0
    xs = x[0]
    cvt = jnp.concatenate([c[0][:, None], c_ctx[:, None],
                           jnp.zeros((D_MODEL, SUBLANES - N_COND), F32)], axis=1)
    mods, w_in_b = _prep(cvt, w_ada, b_ada, w_in, _channel_dft_tables(), _block_diag_wf(w_fourier))
    w_out_b = jnp.concatenate([w_out[:, h * HEAD_DIM:(h + 1) * HEAD_DIM, :] for h in HEAD_ORDER]
                              + [w_out[:, ATTN_W:, :]], axis=1).astype(BF16)
    shared = (_group_avg_table(),)
    rope = _rope_tables(n)
    tabs1 = _stage1_tables(n // DFT_B)
    tabs2 = _stage2_tables()
    ctx_tabs = _ctx_dft_tables(n_ctx)
    sgu = (sgu_ln_g.reshape(depth, C_W), sgu_ln_b.reshape(depth, C_W),
           jnp.concatenate([w_spatial[:, h] for h in range(C_HEADS)], axis=2).astype(BF16),
           jnp.repeat(jnp.swapaxes(b_spatial, 1, 2), C_DIM, axis=2))
    sink_tab = jnp.broadcast_to((attn_sink * math.log2(math.e))[:, :, None], (depth, ATTN_HEADS, LANES))
    bf_row = b_fourier.reshape(depth, F_W)

    ffn_b = None
    for l in range(depth):
        if l == 0:
            q, kd, vd, fx, sg, *ffn_b = _inproj(xs, mods, w_in_b, l, rope, shared, sgu, tm_in_cast,
                                                n_sub_in_cast, cast=(w_ffn_in, w_ffn_out, l))
        else:
            q, kd, vd, fx, sg = _inproj(xs, mods, w_in_b, l, rope, shared, sgu, tm_in, n_sub_in)
        weights = (w_out_b, ln1_g, ln1_b, *ffn_b, ln2_g, ln2_b)
        if l == 0:
            ctx_kv = _context(ctx[0], mods, w_in_b, shared, sgu, sink_tab, ctx_tabs, bf_row, weights)
        nxt = (w_ffn_in, w_ffn_out, l + 1) if l + 1 < depth else None
        oa, *ffn_b = _attention(q, kd, vd, ctx_kv[2 * l], ctx_kv[2 * l + 1], sink_tab, l, tq, cast=nxt)
        of = _fourier(fx, tabs1, tabs2, bf_row, l)
        xs = _outffn(xs, oa, of, sg, mods, weights, l, tm_out, n_sub)
    return xs[None]


def kernel(x, c, ctx, c_ctx, w_ada, b_ada, w_in, w_out, attn_sink, w_fourier, b_fourier, sgu_ln_g,
           sgu_ln_b, w_spatial, b_spatial, ln1_g, ln1_b, w_ffn_in, w_ffn_out, ln2_g, ln2_b):
    return _forward(x, c, ctx, c_ctx, w_ada, b_ada, w_in, w_out, attn_sink, w_fourier, b_fourier,
                    sgu_ln_g, sgu_ln_b, w_spatial, b_spatial, ln1_g, ln1_b, w_ffn_in, w_ffn_out,
                    ln2_g, ln2_b)
```
